```python
import math
import jax, jax.numpy as jnp
from jax import lax
import numpy as np

D_MODEL = 1024
BATCH = 8
SEQ = 2048
DEPTH = 2
DEC_BATCH = 128
DEC_SEQ = 8
PAST_LEN = 16384
PAGE_SIZE = 128

MIX_W = D_MODEL // 2
N_BRANCH = 4
RET_HEADS = 4
RET_DK = MIX_W // RET_HEADS
RET_DV = MIX_W // RET_HEADS
RET_CHUNK = 64
ROPE_BASE = 10000.0
SSD_HEADDIM = 64
SSD_HEADS = MIX_W // SSD_HEADDIM
SSD_GROUPS = 2
SSD_HPG = SSD_HEADS // SSD_GROUPS
SSD_STATE = 128
SSD_CONV = 4
SSD_CHUNK = 64
SSD_CONV_DIM = MIX_W + 2 * SSD_GROUPS * SSD_STATE
HG_HEADS = 4
HG_DK = MIX_W // HG_HEADS
HG_DV = MIX_W // HG_HEADS
HG_CHUNK = 64
S5_GROUP = 16
S5_GROUPS = MIX_W // S5_GROUP
S5_STATE = 64
D_FF = 4 * D_MODEL
GATE_COLS = N_BRANCH * D_MODEL
IN_COLS = 4 * MIX_W + (MIX_W + SSD_CONV_DIM + SSD_HEADS) + 4 * MIX_W + MIX_W + GATE_COLS
EPS = 1e-6

kernel_name = 'hybrid_ret_ssd_hgrn2_s5_gated_step'


def _rmsnorm(x, g):
    xf = x.astype(jnp.float32)
    y = xf * lax.rsqrt(jnp.mean(xf * xf, axis=-1, keepdims=True) + EPS)
    return (y * g.astype(jnp.float32)).astype(x.dtype)


def _head_groupnorm(o, g):
    mu = jnp.mean(o, axis=-1, keepdims=True)
    var = jnp.mean(jnp.square(o - mu), axis=-1, keepdims=True)
    return (o - mu) * lax.rsqrt(var + EPS) * g.astype(jnp.float32)


def _chunk_len(L, c):
    return c if L % c == 0 else L


def _to_chunks(a, c):
    B, L = a.shape[:2]
    return jnp.moveaxis(a.reshape((B, L // c, c) + a.shape[2:]), 1, 0)


def _from_chunks(a):
    a = jnp.moveaxis(a, 0, 1)
    return a.reshape((a.shape[0], a.shape[1] * a.shape[2]) + a.shape[3:])


def _masked_exp(mask, seg):
    return jnp.where(mask, jnp.exp(jnp.where(mask, seg, 0.0)), 0.0)


def _rope(x, pos):
    half = x.shape[-1] // 2
    inv = ROPE_BASE ** (-jnp.arange(half, dtype=jnp.float32) / half)
    ang = pos.astype(jnp.float32)[:, None] * inv[None, :]
    cos = jnp.cos(ang)[None, :, None, :]
    sin = jnp.sin(ang)[None, :, None, :]
    x1, x2 = x[..., :half], x[..., half:]
    return jnp.concatenate([x1 * cos - x2 * sin, x1 * sin + x2 * cos], axis=-1)


def _retention(q, k, v, pos, s0):
    L = q.shape[1]
    c = _chunk_len(L, RET_CHUNK)
    q = _rope(q, pos)
    k = _rope(k, pos) * (RET_DK ** -0.5)
    log_g = jnp.log(1.0 - 2.0 ** (-5.0 - jnp.arange(RET_HEADS, dtype=jnp.float32)))
    idx = jnp.arange(c, dtype=jnp.float32)
    diff = idx[:, None] - idx[None, :]
    dmat = jnp.where(diff >= 0, jnp.exp(log_g[:, None, None] * jnp.maximum(diff, 0.0)), 0.0)
    q_dec = jnp.exp(log_g[None, :] * (idx[:, None] + 1.0))
    k_dec = jnp.exp(log_g[None, :] * (c - 1.0 - idx[:, None]))
    chunk_dec = jnp.exp(log_g * c)

    def step(S, blk):
        qc, kc, vc = blk
        att = jnp.einsum('bihd,bjhd->bhij', qc, kc) * dmat
        o = jnp.einsum('bhij,bjhe->bihe', att, vc)
        o = o + jnp.einsum('bihd,bhde->bihe', qc * q_dec[None, :, :, None], S)
        S = S * chunk_dec[None, :, None, None] + jnp.einsum('bjhd,bjhe->bhde', kc * k_dec[None, :, :, None], vc)
        return S, o

    S, o = lax.scan(step, s0, (_to_chunks(q, c), _to_chunks(k, c), _to_chunks(v, c)))
    return _from_chunks(o), S


def _causal_conv(xbc, buf, w, b):
    L = xbc.shape[1]
    full = jnp.concatenate([buf, xbc], axis=1)
    w = w.astype(jnp.float32)
    out = b.astype(jnp.float32) + full[:, 0:L] * w[0]
    for j in range(1, SSD_CONV):
        out = out + full[:, j:j + L] * w[j]
    return jax.nn.silu(out), full[:, -(SSD_CONV - 1):]


def _ssd(x, dt_raw, bm, cm, s0, a_log, dt_bias, d_skip):
    L = x.shape[1]
    c = _chunk_len(L, SSD_CHUNK)
    A = -jnp.exp(a_log.astype(jnp.float32))
    dt = jax.nn.softplus(dt_raw + dt_bias.astype(jnp.float32))
    dA = dt * A
    xdt = x * dt[..., None]
    mask = jnp.tril(jnp.ones((c, c), dtype=bool))

    def step(S, blk):
        xdtc, dAc, bc, cc = blk
        cum = jnp.cumsum(dAc, axis=1)
        cum_t = jnp.moveaxis(cum, 1, -1)
        seg = cum_t[..., :, None] - cum_t[..., None, :]
        lmat = _masked_exp(mask, seg)
        cb = jnp.einsum('bign,bjgn->bgij', cc, bc)
        y = jnp.einsum('bgrij,bjgrp->bigrp', cb[:, :, None] * lmat, xdtc)
        y = y + jnp.einsum('bign,bgrpn->bigrp', cc, S) * jnp.exp(cum)[..., None]
        decay_end = jnp.exp(cum[:, -1:] - cum)
        S = S * jnp.exp(cum[:, -1])[..., None, None] + jnp.einsum('bjgn,bjgrp->bgrpn', bc, xdtc * decay_end[..., None])
        return S, y

    S, y = lax.scan(step, s0, tuple(_to_chunks(a, c) for a in (xdt, dA, bm, cm)))
    y = _from_chunks(y) + x * d_skip.astype(jnp.float32)[..., None]
    return y, S


def _hgrn2(q, f_logit, i, lb, s0):
    L = q.shape[1]
    c = _chunk_len(L, HG_CHUNK)
    q = jax.nn.silu(q)
    f = lb + (1.0 - lb) * jax.nn.sigmoid(f_logit)
    log_f = jnp.log(f)
    k = (1.0 - lb) * jax.nn.sigmoid(-f_logit)
    mask = jnp.tril(jnp.ones((c, c), dtype=bool))[:, :, None, None]

    def step(S, blk):
        qc, kc, ic, lfc = blk
        G = jnp.cumsum(lfc, axis=1)
        seg = G[:, :, None] - G[:, None, :]
        dec = _masked_exp(mask, seg)
        att = jnp.einsum('bijhd,bjhd->bhij', qc[:, :, None] * dec, kc)
        o = jnp.einsum('bhij,bjhe->bihe', att, ic)
        o = o + jnp.einsum('bihd,bhde->bihe', qc * jnp.exp(G), S)
        g_last = G[:, -1]
        S = S * jnp.exp(g_last)[..., None] + jnp.einsum('bjhd,bjhe->bhde', kc * jnp.exp(g_last[:, None] - G), ic)
        return S, o

    S, o = lax.scan(step, s0, tuple(_to_chunks(a, c) for a in (q, k, i, log_f)))
    return _from_chunks(o), S


def _cmul(ar, ai, br, bi):
    return ar * br - ai * bi, ar * bi + ai * br


def _s5_combine(e1, e2):
    a1r, a1i, b1r, b1i = e1
    a2r, a2i, b2r, b2i = e2
    ar, ai = _cmul(a2r, a2i, a1r, a1i)
    br, bi = _cmul(a2r, a2i, b1r, b1i)
    return ar, ai, br + b2r, bi + b2i


def _s5(u, h0_re, h0_im, a_re, a_im, b_re, b_im, c_re, c_im, d, log_dt, w_glu):
    B, L = u.shape[:2]
    a_re, a_im, b_re, b_im, c_re, c_im, d, log_dt = (t.astype(jnp.float32) for t in (a_re, a_im, b_re, b_im, c_re, c_im, d, log_dt))
    dt = jnp.exp(log_dt)[:, None]
    mag = jnp.exp(dt * a_re)
    ab_re, ab_im = mag * jnp.cos(dt * a_im), mag * jnp.sin(dt * a_im)
    den = a_re * a_re + a_im * a_im
    n_re, n_im = ab_re - 1.0, ab_im
    f_re = (n_re * a_re + n_im * a_im) / den
    f_im = (n_im * a_re - n_re * a_im) / den
    bb_re = f_re[..., None] * b_re - f_im[..., None] * b_im
    bb_im = f_re[..., None] * b_im + f_im[..., None] * b_re
    bu_re = jnp.einsum('blgm,gpm->blgp', u, bb_re)
    bu_im = jnp.einsum('blgm,gpm->blgp', u, bb_im)
    i_re, i_im = _cmul(ab_re, ab_im, h0_re, h0_im)
    bu_re = bu_re.at[:, 0].add(i_re)
    bu_im = bu_im.at[:, 0].add(i_im)
    a_re_t = jnp.broadcast_to(ab_re, bu_re.shape)
    a_im_t = jnp.broadcast_to(ab_im, bu_im.shape)
    _, _, h_re, h_im = lax.associative_scan(_s5_combine, (a_re_t, a_im_t, bu_re, bu_im), axis=1)
    y = jnp.einsum('gmp,blgp->blgm', c_re, h_re) - jnp.einsum('gmp,blgp->blgm', c_im, h_im) + d * u
    z = jax.nn.gelu(y.reshape(B, L, MIX_W))
    out = z * jax.nn.sigmoid(jnp.matmul(z, w_glu.astype(jnp.float32)))
    return out, h_re[:, -1], h_im[:, -1]


def _layer(x, st, p, lb, pos):
    B, L, _ = x.shape
    h = _rmsnorm(x, p['g_pre_mix'])
    proj = jnp.matmul(h, p['w_in']).astype(jnp.float32)
    sizes = (MIX_W,) * 4 + (MIX_W, SSD_CONV_DIM, SSD_HEADS) + (MIX_W,) * 4 + (MIX_W, GATE_COLS)
    (rq, rk, rv, rg, sz, sxbc, sdt, hq, hf, hi, hg, su, gl) = jnp.split(proj, np.cumsum(sizes)[:-1].tolist(), axis=-1)

    def heads(a, n):
        return a.reshape(B, L, n, a.shape[-1] // n)

    o_ret, s_ret = _retention(heads(rq, RET_HEADS), heads(rk, RET_HEADS), heads(rv, RET_HEADS), pos, st['ret'])
    o_ret = _head_groupnorm(o_ret, p['ret_gn']).reshape(B, L, MIX_W) * jax.nn.silu(rg)

    xbc, conv_new = _causal_conv(sxbc, st['conv'], p['ssd_conv_w'], p['ssd_conv_b'])
    xs, bm, cm = jnp.split(xbc, [MIX_W, MIX_W + SSD_GROUPS * SSD_STATE], axis=-1)
    grp = (SSD_GROUPS, SSD_HPG)
    y, s_ssd = _ssd(xs.reshape(B, L, SSD_GROUPS, SSD_HPG, SSD_HEADDIM), sdt.reshape(B, L, SSD_GROUPS, SSD_HPG),
                    bm.reshape(B, L, SSD_GROUPS, SSD_STATE), cm.reshape(B, L, SSD_GROUPS, SSD_STATE),
                    st['ssd'].reshape(B, SSD_GROUPS, SSD_HPG, SSD_HEADDIM, SSD_STATE),
                    p['ssd_a_log'].reshape(grp), p['ssd_dt_bias'].reshape(grp), p['ssd_d'].reshape(grp))
    y = y.reshape(B, L, MIX_W) * jax.nn.silu(sz)
    o_ssd = _rmsnorm(y.reshape(B, L, SSD_GROUPS, MIX_W // SSD_GROUPS), p['ssd_norm'].reshape(SSD_GROUPS, MIX_W // SSD_GROUPS)).reshape(B, L, MIX_W)
    s_ssd = s_ssd.reshape(B, SSD_HEADS, SSD_HEADDIM, SSD_STATE)

    o_hg, s_hg = _hgrn2(heads(hq, HG_HEADS), heads(hf, HG_HEADS), heads(hi, HG_HEADS), lb.reshape(HG_HEADS, HG_DK), st['hgrn'])
    o_hg = _rmsnorm(o_hg, p['hg_norm']).reshape(B, L, MIX_W) * jax.nn.sigmoid(hg)

    o_s5, s5_re, s5_im = _s5(su.reshape(B, L, S5_GROUPS, S5_GROUP), st['s5_re'], st['s5_im'],
                             p['s5_a_re'], p['s5_a_im'], p['s5_b_re'], p['s5_b_im'], p['s5_c_re'], p['s5_c_im'],
                             p['s5_d'], p['s5_log_dt'], p['s5_w_glu'])

    branches = jnp.stack([o_ret, o_ssd, o_hg, o_s5], axis=2).astype(x.dtype)
    gates = jax.nn.sigmoid(gl.reshape(B, L, N_BRANCH, D_MODEL)).astype(x.dtype)
    merged = jnp.sum(gates * jnp.einsum('blmw,mwd->blmd', branches, p['w_branch']), axis=2)
    x = x + _rmsnorm(jnp.matmul(merged, p['w_out']), p['g_post_mix']).astype(x.dtype)

    h = _rmsnorm(x, p['g_pre_ffn'])
    u = jnp.square(jax.nn.relu(jnp.matmul(h, p['w_ff1'])))
    x = x + _rmsnorm(jnp.matmul(u, p['w_ff2']), p['g_post_ffn']).astype(x.dtype)
    new = {'ret': s_ret, 'ssd': s_ssd, 'conv': conv_new, 'hgrn': s_hg, 's5_re': s5_re, 's5_im': s5_im}
    return x, new


def _trunk(x, states, params, lb_logits, pos):
    w = jax.nn.softmax(lb_logits.astype(jnp.float32), axis=0)
    lbs = jnp.cumsum(w, axis=0) - w[0]
    names = ('ret', 'ssd', 'conv', 'hgrn', 's5_re', 's5_im')
    collected = {n: [] for n in names}
    for l in range(DEPTH):
        p = {name: arr[l] for name, arr in params.items()}
        st = {n: states[n][l].astype(jnp.float32) for n in names}
        x, new = _layer(x, st, p, lbs[l], pos)
        for n in names:
            collected[n].append(new[n])
    return x, {n: jnp.stack(collected[n]) for n in names}


def setup_inputs(seed: int = 0) -> dict:
    key = jax.random.key(seed)
    k = jax.random.split(key, 35)
    f32 = jnp.float32

    def nrm(i, shape, s=1.0):
        return jax.random.normal(k[i], shape, f32) * s

    def unif(i, shape, lo, hi):
        return jax.random.uniform(k[i], shape, f32, lo, hi)

    dt0 = jnp.exp(unif(17, (DEPTH, SSD_HEADS), math.log(1e-3), math.log(1e-1)))
    a_im0 = math.pi * jnp.arange(S5_STATE, dtype=f32)
    return {
        'x_prompt': nrm(0, (BATCH, SEQ, D_MODEL)),
        'x_sample': nrm(1, (DEC_BATCH, DEC_SEQ, D_MODEL)),
        'state_ret': nrm(2, (DEPTH, DEC_BATCH, RET_HEADS, RET_DK, RET_DV), 0.5),
        'state_ssd': nrm(3, (DEPTH, DEC_BATCH, SSD_HEADS, SSD_HEADDIM, SSD_STATE), 0.5),
        'state_conv': nrm(4, (DEPTH, DEC_BATCH, SSD_CONV - 1, SSD_CONV_DIM)),
        'state_hgrn': nrm(5, (DEPTH, DEC_BATCH, HG_HEADS, HG_DK, HG_DV), 0.5),
        'state_s5_re': nrm(6, (DEPTH, DEC_BATCH, S5_GROUPS, S5_STATE), 0.5),
        'state_s5_im': nrm(7, (DEPTH, DEC_BATCH, S5_GROUPS, S5_STATE), 0.5),
        'g_pre_mix': 1.0 + nrm(8, (DEPTH, D_MODEL), 0.05),
        'g_post_mix': 1.0 + nrm(9, (DEPTH, D_MODEL), 0.05),
        'g_pre_ffn': 1.0 + nrm(10, (DEPTH, D_MODEL), 0.05),
        'g_post_ffn': 1.0 + nrm(11, (DEPTH, D_MODEL), 0.05),
        'w_in': nrm(12, (DEPTH, D_MODEL, IN_COLS), D_MODEL ** -0.5),
        'ret_gn': 1.0 + nrm(13, (DEPTH, RET_HEADS, RET_DV), 0.05),
        'ssd_conv_w': nrm(14, (DEPTH, SSD_CONV, SSD_CONV_DIM), SSD_CONV ** -0.5),
        'ssd_conv_b': nrm(15, (DEPTH, SSD_CONV_DIM), 0.01),
        'ssd_a_log': jnp.log(unif(16, (DEPTH, SSD_HEADS), 1.0, 16.0)),
        'ssd_dt_bias': dt0 + jnp.log(-jnp.expm1(-dt0)),
        'ssd_d': 1.0 + nrm(18, (DEPTH, SSD_HEADS), 0.1),
        'ssd_norm': 1.0 + nrm(19, (DEPTH, MIX_W), 0.05),
        'hg_lb_logits': nrm(20, (DEPTH, HG_HEADS * HG_DK), 0.1),
        'hg_norm': 1.0 + nrm(21, (DEPTH, HG_DV), 0.05),
        's5_a_re': -0.5 + nrm(22, (DEPTH, S5_GROUPS, S5_STATE), 0.01),
        's5_a_im': a_im0 + nrm(23, (DEPTH, S5_GROUPS, S5_STATE), 0.01),
        's5_b_re': nrm(24, (DEPTH, S5_GROUPS, S5_STATE, S5_GROUP), (2 * S5_GROUP) ** -0.5),
        's5_b_im': nrm(25, (DEPTH, S5_GROUPS, S5_STATE, S5_GROUP), (2 * S5_GROUP) ** -0.5),
        's5_c_re': nrm(26, (DEPTH, S5_GROUPS, S5_GROUP, S5_STATE), S5_STATE ** -0.5),
        's5_c_im': nrm(27, (DEPTH, S5_GROUPS, S5_GROUP, S5_STATE), S5_STATE ** -0.5),
        's5_d': nrm(28, (DEPTH, S5_GROUPS, S5_GROUP)),
        's5_log_dt': unif(29, (DEPTH, S5_GROUPS), math.log(1e-3), math.log(1e-1)),
        's5_w_glu': nrm(30, (DEPTH, MIX_W, MIX_W), MIX_W ** -0.5),
        'w_branch': nrm(31, (DEPTH, N_BRANCH, MIX_W, D_MODEL), MIX_W ** -0.5),
        'w_out': nrm(32, (DEPTH, D_MODEL, D_MODEL), D_MODEL ** -0.5),
        'w_ff1': nrm(33, (DEPTH, D_MODEL, D_FF), D_MODEL ** -0.5),
        'w_ff2': nrm(34, (DEPTH, D_FF, D_MODEL), D_FF ** -0.5),
    }


def reference(x_prompt, x_sample, state_ret, state_ssd, state_conv, state_hgrn, state_s5_re, state_s5_im,
              g_pre_mix, g_post_mix, g_pre_ffn, g_post_ffn, w_in, ret_gn, ssd_conv_w, ssd_conv_b,
              ssd_a_log, ssd_dt_bias, ssd_d, ssd_norm, hg_lb_logits, hg_norm, s5_a_re, s5_a_im,
              s5_b_re, s5_b_im, s5_c_re, s5_c_im, s5_d, s5_log_dt, s5_w_glu, w_branch, w_out, w_ff1, w_ff2):
    params = {
        'g_pre_mix': g_pre_mix, 'g_post_mix': g_post_mix, 'g_pre_ffn': g_pre_ffn, 'g_post_ffn': g_post_ffn,
        'w_in': w_in, 'ret_gn': ret_gn, 'ssd_conv_w': ssd_conv_w, 'ssd_conv_b': ssd_conv_b,
        'ssd_a_log': ssd_a_log, 'ssd_dt_bias': ssd_dt_bias, 'ssd_d': ssd_d, 'ssd_norm': ssd_norm,
        'hg_norm': hg_norm, 's5_a_re': s5_a_re, 's5_a_im': s5_a_im, 's5_b_re': s5_b_re, 's5_b_im': s5_b_im,
        's5_c_re': s5_c_re, 's5_c_im': s5_c_im, 's5_d': s5_d, 's5_log_dt': s5_log_dt, 's5_w_glu': s5_w_glu,
        'w_branch': w_branch, 'w_out': w_out, 'w_ff1': w_ff1, 'w_ff2': w_ff2,
    }
    bp = x_prompt.shape[0]
    f32 = jnp.float32
    zero_states = {
        'ret': jnp.zeros((DEPTH, bp, RET_HEADS, RET_DK, RET_DV), f32),
        'ssd': jnp.zeros((DEPTH, bp, SSD_HEADS, SSD_HEADDIM, SSD_STATE), f32),
        'conv': jnp.zeros((DEPTH, bp, SSD_CONV - 1, SSD_CONV_DIM), f32),
        'hgrn': jnp.zeros((DEPTH, bp, HG_HEADS, HG_DK, HG_DV), f32),
        's5_re': jnp.zeros((DEPTH, bp, S5_GROUPS, S5_STATE), f32),
        's5_im': jnp.zeros((DEPTH, bp, S5_GROUPS, S5_STATE), f32),
    }
    sample_states = {'ret': state_ret, 'ssd': state_ssd, 'conv': state_conv, 'hgrn': state_hgrn,
                     's5_re': state_s5_re, 's5_im': state_s5_im}
    pos_p = jnp.arange(x_prompt.shape[1], dtype=jnp.int32)
    pos_s = PAST_LEN + jnp.arange(x_sample.shape[1], dtype=jnp.int32)
    y_prompt, new_p = _trunk(x_prompt, zero_states, params, hg_lb_logits, pos_p)
    y_sample, new_s = _trunk(x_sample, sample_states, params, hg_lb_logits, pos_s)
    return (y_prompt, y_sample,
            new_p['ret'], new_s['ret'], new_p['ssd'], new_s['ssd'], new_p['conv'], new_s['conv'],
            new_p['hgrn'], new_s['hgrn'], new_p['s5_re'], new_s['s5_re'], new_p['s5_im'], new_s['s5_im'])
```

```python
import functools
import math

import jax
import jax.numpy as jnp
import numpy as np
from jax import lax
from jax.experimental import pallas as pl
from jax.experimental.pallas import tpu as pltpu

F32 = jnp.float32
BF16 = jnp.bfloat16

D_MODEL = 1024
DEPTH = 2
PAST_LEN = 16384
MIX_W = D_MODEL // 2
N_BRANCH = 4
RET_HEADS = 4
RET_DK = MIX_W // RET_HEADS
ROPE_BASE = 10000.0
SSD_HEADDIM = 64
SSD_HEADS = MIX_W // SSD_HEADDIM
SSD_GROUPS = 2
SSD_STATE = 128
SSD_CONV = 4
SSD_CONV_DIM = MIX_W + 2 * SSD_GROUPS * SSD_STATE
SSD_PAIRS = SSD_HEADS // 2
HG_HEADS = 4
HG_DK = MIX_W // HG_HEADS
S5_GROUP = 16
S5_GROUPS = MIX_W // S5_GROUP
S5_STATE = 64
S5_LANES = S5_GROUPS * S5_STATE
D_FF = 4 * D_MODEL
GATE_COLS = N_BRANCH * D_MODEL
CHUNK = 64
EPS = 1e-6

LANE = 128
SUBLANE = 8
VMEM_LIMIT = 48 * 1024 * 1024

COL_GL = 0
COL_RET = GATE_COLS
COL_SZ = COL_RET + 4 * MIX_W
COL_XS = COL_SZ + MIX_W
COL_BC = COL_XS + MIX_W
COL_HG = COL_BC + MIX_W
COL_DT = COL_HG + 4 * MIX_W
PROJ_COLS = COL_DT + MIX_W


def _sigmoid(x):
    return 1.0 / (1.0 + jnp.exp(-x))


def _silu(x):
    return x * _sigmoid(x)


def _softplus(x):
    return jnp.maximum(x, 0.0) + jnp.log1p(jnp.exp(-jnp.abs(x)))


def _gelu_tanh(x):
    c = math.sqrt(2.0 / math.pi)
    return 0.5 * x * (1.0 + jnp.tanh(c * (x + 0.044715 * (x * x * x))))


def _rms(x, g):
    ms = jnp.mean(x * x, axis=-1, keepdims=True)
    return x * lax.rsqrt(ms + EPS) * g


def _dot(a, b):
    return jnp.dot(a, b, preferred_element_type=F32)


def _dot_nt(a, b):
    return lax.dot_general(a, b, (((1,), (1,)), ((), ())), preferred_element_type=F32)


def _split3(a):
    hi = a.astype(BF16)
    r1 = a - hi.astype(F32)
    mid = r1.astype(BF16)
    lo = (r1 - mid.astype(F32)).astype(BF16)
    return hi, mid, lo


def _sel_l(m, a):
    hi, mid, lo = _split3(a)
    return (_dot(m, lo) + _dot(m, mid)) + _dot(m, hi)


def _sel_r(a, m):
    hi, mid, lo = _split3(a)
    return (_dot(lo, m) + _dot(mid, m)) + _dot(hi, m)


def _cparams(sem):
    return pltpu.CompilerParams(dimension_semantics=sem, vmem_limit_bytes=VMEM_LIMIT)


def _const_spec(shape):
    nd = len(shape)
    return pl.BlockSpec(shape, lambda *_: (0,) * nd)


def _proj_kernel(x_ref, g_ref, w_ref, o_ref, h_ref):
    @pl.when(pl.program_id(1) == 0)
    def _():
        h_ref[...] = _rms(x_ref[...], g_ref[...]).astype(BF16)

    o_ref[...] = _dot(h_ref[...], w_ref[...])


def _proj_call(x2, g, w, *, tm, tn):
    t, d = x2.shape
    n = w.shape[1]
    return pl.pallas_call(
        _proj_kernel,
        grid=(t // tm, n // tn),
        in_specs=[
            pl.BlockSpec((tm, d), lambda i, j: (i, 0)),
            pl.BlockSpec((1, d), lambda i, j: (0, 0)),
            pl.BlockSpec((d, tn), lambda i, j: (0, j)),
        ],
        out_specs=pl.BlockSpec((tm, tn), lambda i, j: (i, j)),
        out_shape=jax.ShapeDtypeStruct((t, n), F32),
        scratch_shapes=[pltpu.VMEM((tm, d), BF16)],
        compiler_params=_cparams(("parallel", "arbitrary")),
        name="in_proj",
    )(x2, g, w)


def _ret_consts(c):
    h = np.arange(RET_HEADS, dtype=np.float64)
    log_g = np.log(1.0 - 2.0 ** (-5.0 - h))
    idx = np.arange(c, dtype=np.float64)
    diff = idx[:, None] - idx[None, :]
    dmat = np.where(diff >= 0, np.exp(log_g[:, None, None] * np.maximum(diff, 0.0)), 0.0)
    q_dec = np.exp(log_g[:, None] * (idx[None, :] + 1.0))
    k_dec = np.exp(log_g[:, None] * (c - 1.0 - idx[None, :]))
    chunk_dec = np.exp(log_g * c)
    qd = np.broadcast_to(q_dec[:, :, None], (RET_HEADS, c, LANE))
    kd = np.broadcast_to(k_dec[:, :, None], (RET_HEADS, c, LANE))
    return (jnp.asarray(dmat, F32), jnp.asarray(qd, F32), jnp.asarray(kd, F32),
            tuple(float(v) for v in chunk_dec))


def _rope_tables(pos):
    half = RET_DK // 2
    inv = ROPE_BASE ** (-np.arange(half, dtype=np.float64) / half)
    ang = np.asarray(pos, np.float64)[:, None] * inv[None, :]
    cos, sin = np.cos(ang), np.sin(ang)
    return (jnp.asarray(np.concatenate([cos, cos], axis=1), F32),
            jnp.asarray(np.concatenate([-sin, sin], axis=1), F32))


def _seq_chunk_index(idx, tb, nch):
    if nch == 1:
        return idx, 0
    return 0, idx


def _ret_kernel(q_ref, k_ref, v_ref, g_ref, cos_ref, sin_ref, dmat_ref, qd_ref, kd_ref, gn_ref, s0_ref,
                o_ref, st_ref, *, tb, tl, c, cdec):
    @pl.when(pl.program_id(1) == 0)
    def _():
        st_ref[...] = s0_ref[...]

    nch = tl // c
    scale = RET_DK ** -0.5

    def body(idx, carry):
        s, ch = _seq_chunk_index(idx, tb, nch)
        r0 = pl.multiple_of(idx * c, c)
        t0 = pl.multiple_of(ch * c, c)
        cos = cos_ref[pl.ds(t0, c), :]
        sin = sin_ref[pl.ds(t0, c), :]
        for h in range(RET_HEADS):
            sl = slice(h * LANE, (h + 1) * LANE)
            q = q_ref[pl.ds(r0, c), sl]
            k = k_ref[pl.ds(r0, c), sl]
            v = v_ref[pl.ds(r0, c), sl].astype(BF16)
            qr = q * cos + pltpu.roll(q, LANE // 2, axis=1) * sin
            kr = (k * cos + pltpu.roll(k, LANE // 2, axis=1) * sin) * scale
            att = _dot_nt(qr.astype(BF16), kr.astype(BF16)) * dmat_ref[h]
            st = st_ref[s, h]
            o = _dot(att.astype(BF16), v) + _dot((qr * qd_ref[h]).astype(BF16), st.astype(BF16))
            kdt = (kr * kd_ref[h]).T.astype(BF16)
            st_ref[s, h] = st * cdec[h] + _dot(kdt, v)
            mu = jnp.mean(o, axis=-1, keepdims=True)
            oc = o - mu
            var = jnp.mean(oc * oc, axis=-1, keepdims=True)
            on = oc * lax.rsqrt(var + EPS) * gn_ref[h:h + 1, :]
            o_ref[pl.ds(r0, c), sl] = on * _silu(g_ref[pl.ds(r0, c), sl])
        return carry

    lax.fori_loop(0, tb * nch, body, 0)


def _seq_grid(b, l, tb, tl):
    assert b % tb == 0 and l % tl == 0 and (tb == 1 or tl == l)
    nl = l // tl
    rows = tb * tl

    def row_map(col_blk):
        return lambda bi, li: (bi * nl + li, col_blk)

    return (b // tb, nl), rows, row_map


def _ret_call(proj, cos, sin, gn, s0, *, b, l, tb, tl, c):
    grid, rows, row_map = _seq_grid(b, l, tb, tl)
    dmat, qd, kd, cdec = _ret_consts(c)
    cb = COL_RET // MIX_W
    state_spec = pl.BlockSpec((tb, RET_HEADS, RET_DK, RET_DK), lambda bi, li: (bi, 0, 0, 0))
    return pl.pallas_call(
        functools.partial(_ret_kernel, tb=tb, tl=tl, c=c, cdec=cdec),
        grid=grid,
        in_specs=[pl.BlockSpec((rows, MIX_W), row_map(cb + i)) for i in range(4)] + [
            pl.BlockSpec((tl, LANE), lambda bi, li: (li, 0)),
            pl.BlockSpec((tl, LANE), lambda bi, li: (li, 0)),
            _const_spec(dmat.shape), _const_spec(qd.shape), _const_spec(kd.shape), _const_spec(gn.shape),
            state_spec,
        ],
        out_specs=[pl.BlockSpec((rows, MIX_W), row_map(0)), state_spec],
        out_shape=[jax.ShapeDtypeStruct((b * l, MIX_W), F32), jax.ShapeDtypeStruct(s0.shape, F32)],
        compiler_params=_cparams(("parallel", "arbitrary")),
        name="retention",
    )(proj, proj, proj, proj, cos, sin, dmat, qd, kd, gn, s0)


def _ssd_consts(c):
    tril = np.tril(np.ones((c, c)))
    ones = np.ones((c, c))
    e512 = np.zeros((LANE, MIX_W))
    sel = np.zeros((LANE, SSD_HEADS * LANE))
    for h in range(SSD_HEADS):
        e512[h, h * SSD_HEADDIM:(h + 1) * SSD_HEADDIM] = 1.0
        sel[h, h * LANE:(h + 1) * LANE] = 1.0
    i = np.arange(c)[:, None]
    j = np.arange(SSD_HEADS * LANE)[None, :] % LANE
    eye_rep = (i == j).astype(np.float64)
    tril_rep = ((i >= j) & (j < c)).astype(np.float64)
    return (jnp.asarray(tril, BF16), jnp.asarray(ones, BF16), jnp.asarray(e512, BF16), jnp.asarray(sel, BF16),
            jnp.asarray(eye_rep, F32), jnp.asarray(tril_rep, F32))


def _ssd_kernel(z_ref, xs_ref, bc_ref, dt_ref, cw_ref, cb_ref, alog_ref, dtb_ref, dsk_ref, nw_ref,
                tril_ref, ones_ref, e512_ref, sel_ref, eye_ref, trilrep_ref, cs0_ref, s0_ref,
                o_ref, cs_ref, so_ref, xpad, xc, st, *, tb, tl, c):
    li = pl.program_id(1)
    nl = pl.num_programs(1)

    @pl.when(li == 0)
    def _():
        cs_ref[...] = cs0_ref[...]
        for s in range(tb):
            for p in range(SSD_PAIRS):
                st[s, p] = s0_ref[s, p].T

    kc = SSD_CONV - 1
    off = SUBLANE - kc
    for s in range(tb):
        rows = slice(s * tl, (s + 1) * tl)
        xpad[off:SUBLANE, :] = cs_ref[s]
        xpad[SUBLANE:SUBLANE + tl, 0:MIX_W] = xs_ref[rows, :]
        xpad[SUBLANE:SUBLANE + tl, MIX_W:2 * MIX_W] = bc_ref[rows, :]
        acc = cb_ref[...] + xpad[off:off + tl, :] * cw_ref[0:1, :]
        for j in range(1, SSD_CONV):
            acc = acc + xpad[off + j:off + j + tl, :] * cw_ref[j:j + 1, :]
        xc[rows, :] = _silu(acc)
        cs_ref[s] = xpad[tl + off:tl + SUBLANE, :]

    nch = tl // c
    a_neg = -jnp.exp(alog_ref[...])
    half = lax.broadcasted_iota(jnp.int32, (c, LANE), 1) < SSD_HEADDIM

    def body(idx, carry):
        s, _ = _seq_chunk_index(idx, tb, nch)
        r0 = pl.multiple_of(idx * c, c)
        xs = xc[pl.ds(r0, c), 0:MIX_W]
        bm = xc[pl.ds(r0, c), MIX_W:MIX_W + SSD_GROUPS * SSD_STATE]
        cm = xc[pl.ds(r0, c), MIX_W + SSD_GROUPS * SSD_STATE:2 * MIX_W]
        dt = _softplus(dt_ref[pl.ds(r0, c), :] + dtb_ref[...])
        cum = _sel_l(tril_ref[...], dt * a_neg)
        clast = cum[c - 1:c, :]
        stack = jnp.concatenate(
            [dt, jnp.exp(cum), jnp.exp(clast - cum), jnp.broadcast_to(jnp.exp(clast), (SUBLANE, LANE))], axis=0)
        ex = _sel_r(stack, e512_ref[...])
        dt_x, ecum_x, dend_x, elast_x = ex[0:c], ex[c:2 * c], ex[2 * c:3 * c], ex[3 * c:3 * c + 1]
        c1 = _sel_r(cum, sel_ref[...])
        rr = _sel_l(ones_ref[...], c1 * eye_ref[...])
        msk = trilrep_ref[...]
        lmat = jnp.exp(jnp.where(msk > 0, c1 - rr, 0.0)) * msk
        zz = z_ref[pl.ds(r0, c), :]
        for g in range(SSD_GROUPS):
            gs = slice(g * SSD_STATE, (g + 1) * SSD_STATE)
            bm_g = bm[:, gs]
            cm_g = cm[:, gs].astype(BF16)
            cbm = _dot_nt(cm_g, bm_g.astype(BF16))
            bmt = bm_g.T.astype(BF16)
            ys = []
            for pp in range(SSD_PAIRS // SSD_GROUPS):
                p = g * (SSD_PAIRS // SSD_GROUPS) + pp
                ps = slice(p * LANE, (p + 1) * LANE)
                xs_p = xs[:, ps]
                xdt = xs_p * dt_x[:, ps]
                x_lo = jnp.where(half, xdt, 0.0).astype(BF16)
                x_hi = jnp.where(half, 0.0, xdt).astype(BF16)
                g0 = (cbm * lmat[:, 2 * p * LANE:2 * p * LANE + c]).astype(BF16)
                g1 = (cbm * lmat[:, (2 * p + 1) * LANE:(2 * p + 1) * LANE + c]).astype(BF16)
                y = _dot(g0, x_lo) + _dot(g1, x_hi)
                stp = st[s, p]
                y = y + _dot(cm_g, stp.astype(BF16)) * ecum_x[:, ps]
                st[s, p] = stp * elast_x[:, ps] + _dot(bmt, (xdt * dend_x[:, ps]).astype(BF16))
                ys.append(y + xs_p * dsk_ref[:, ps])
            gw = SSD_HEADDIM * SSD_HEADS // SSD_GROUPS
            zs = slice(g * gw, (g + 1) * gw)
            yg = jnp.concatenate(ys, axis=1) * _silu(zz[:, zs])
            o_ref[pl.ds(r0, c), zs] = _rms(yg, nw_ref[:, zs])
        return carry

    lax.fori_loop(0, tb * nch, body, 0)

    @pl.when(li == nl - 1)
    def _():
        for s in range(tb):
            for p in range(SSD_PAIRS):
                so_ref[s, p] = st[s, p].T


def _ssd_call(proj, cw, cb, alog, dtb, dsk, nw, cs0, s0, *, b, l, tb, tl, c):
    grid, rows, row_map = _seq_grid(b, l, tb, tl)
    consts = _ssd_consts(c)
    s0p = s0.reshape(b, SSD_PAIRS, LANE, SSD_STATE)
    cs_spec = pl.BlockSpec((tb, SSD_CONV - 1, SSD_CONV_DIM), lambda bi, li: (bi, 0, 0))
    st_spec = pl.BlockSpec((tb, SSD_PAIRS, LANE, SSD_STATE), lambda bi, li: (bi, 0, 0, 0))
    params = (cw, cb, alog, dtb, dsk, nw)
    o, cs, so = pl.pallas_call(
        functools.partial(_ssd_kernel, tb=tb, tl=tl, c=c),
        grid=grid,
        in_specs=[
            pl.BlockSpec((rows, MIX_W), row_map(COL_SZ // MIX_W)),
            pl.BlockSpec((rows, MIX_W), row_map(COL_XS // MIX_W)),
            pl.BlockSpec((rows, MIX_W), row_map(COL_BC // MIX_W)),
            pl.BlockSpec((rows, LANE), row_map(COL_DT // LANE)),
        ] + [_const_spec(a.shape) for a in params + consts] + [cs_spec, st_spec],
        out_specs=[pl.BlockSpec((rows, MIX_W), row_map(0)), cs_spec, st_spec],
        out_shape=[jax.ShapeDtypeStruct((b * l, MIX_W), F32), jax.ShapeDtypeStruct(cs0.shape, F32),
                   jax.ShapeDtypeStruct(s0p.shape, F32)],
        scratch_shapes=[pltpu.VMEM((tl + SUBLANE, SSD_CONV_DIM), F32), pltpu.VMEM((rows, SSD_CONV_DIM), F32),
                        pltpu.VMEM((tb, SSD_PAIRS, SSD_STATE, LANE), F32)],
        compiler_params=_cparams(("parallel", "arbitrary")),
        name="ssd",
    )(proj, proj, proj, proj, *params, *consts, cs0, s0p)
    return o, cs, so.reshape(s0.shape)


def _hg_levels(c):
    ms = []
    m = c // 2
    while m >= SUBLANE:
        ms.append(m)
        m //= 2
    return ms


def _hg_consts(c):
    tril = np.tril(np.ones((c, c)))
    mats = [tril]
    masks = []
    i = np.arange(c)
    for m in _hg_levels(c):
        ref = (i // (2 * m)) * (2 * m) + m - 1
        mats.append(tril - tril[ref])
        same = (i[:, None] // (2 * m)) == (i[None, :] // (2 * m))
        qside = ((i // m) % 2 == 1)[:, None]
        kside = ((i // m) % 2 == 0)[None, :]
        masks.append((same & qside & kside).astype(np.float64))
    masks = np.stack(masks) if masks else np.zeros((1, c, c))
    return jnp.asarray(np.concatenate(mats, axis=0), BF16), jnp.asarray(masks, F32)


def _hg_kernel(q_ref, f_ref, i_ref, g_ref, lb_ref, nw_ref, m_ref, mask_ref, s0_ref, o_ref, so_ref, st,
               *, tb, tl, c):
    li = pl.program_id(1)
    nl = pl.num_programs(1)

    @pl.when(li == 0)
    def _():
        for s in range(tb):
            for h in range(HG_HEADS):
                st[s, h] = s0_ref[s, h].T

    nch = tl // c
    levels = _hg_levels(c)
    n8 = c // SUBLANE
    sub = lax.broadcasted_iota(jnp.int32, (n8, SUBLANE, LANE), 1)

    def body(idx, carry):
        s, _ = _seq_chunk_index(idx, tb, nch)
        r0 = pl.multiple_of(idx * c, c)
        for h in range(HG_HEADS):
            sl = slice(h * LANE, (h + 1) * LANE)
            lb = lb_ref[h:h + 1, :]
            q = _silu(q_ref[pl.ds(r0, c), sl])
            fz = f_ref[pl.ds(r0, c), sl]
            logf = jnp.log(lb + (1.0 - lb) * _sigmoid(fz))
            k = (1.0 - lb) * _sigmoid(-fz)
            v = i_ref[pl.ds(r0, c), sl]
            vb = v.astype(BF16)
            sums = _sel_l(m_ref[...], logf)
            gc = sums[0:c]
            att = None
            for n, _m in enumerate(levels):
                d = sums[(n + 1) * c:(n + 2) * c]
                e = jnp.exp(-jnp.abs(d))
                a = _dot_nt((q * e).astype(BF16), (k * e).astype(BF16)) * mask_ref[n]
                att = a if att is None else att + a
            g3 = gc.reshape(n8, SUBLANE, LANE)
            q3 = q.reshape(n8, SUBLANE, LANE)
            k3 = k.reshape(n8, SUBLANE, LANE)
            v3 = v.reshape(n8, SUBLANE, LANE)
            od = jnp.zeros((n8, SUBLANE, LANE), F32)
            for j in range(SUBLANE):
                keep = sub >= j
                e = jnp.where(keep, jnp.exp(jnp.where(keep, g3 - g3[:, j:j + 1, :], 0.0)), 0.0)
                col = jnp.sum(q3 * e * k3[:, j:j + 1, :], axis=-1, keepdims=True)
                od = od + col * v3[:, j:j + 1, :]
            o = od.reshape(c, LANE)
            if att is not None:
                o = o + _dot(att.astype(BF16), vb)
            stt = st[s, h]
            o = o + _dot_nt((q * jnp.exp(gc)).astype(BF16), stt.astype(BF16))
            glast = gc[c - 1:c, :]
            kd = (k * jnp.exp(glast - gc)).astype(BF16)
            st[s, h] = stt * jnp.exp(glast) + _dot(v.T.astype(BF16), kd)
            o_ref[pl.ds(r0, c), sl] = _rms(o, nw_ref[...]) * _sigmoid(g_ref[pl.ds(r0, c), sl])
        return carry

    lax.fori_loop(0, tb * nch, body, 0)

    @pl.when(li == nl - 1)
    def _():
        for s in range(tb):
            for h in range(HG_HEADS):
                so_ref[s, h] = st[s, h].T


def _hg_call(proj, lb, nw, s0, *, b, l, tb, tl, c):
    grid, rows, row_map = _seq_grid(b, l, tb, tl)
    mstack, masks = _hg_consts(c)
    cb = COL_HG // MIX_W
    st_spec = pl.BlockSpec((tb, HG_HEADS, HG_DK, HG_DK), lambda bi, li: (bi, 0, 0, 0))
    return pl.pallas_call(
        functools.partial(_hg_kernel, tb=tb, tl=tl, c=c),
        grid=grid,
        in_specs=[pl.BlockSpec((rows, MIX_W), row_map(cb + i)) for i in range(4)] + [
            _const_spec(lb.shape), _const_spec(nw.shape), _const_spec(mstack.shape), _const_spec(masks.shape), st_spec],
        out_specs=[pl.BlockSpec((rows, MIX_W), row_map(0)), st_spec],
        out_shape=[jax.ShapeDtypeStruct((b * l, MIX_W), F32), jax.ShapeDtypeStruct(s0.shape, F32)],
        scratch_shapes=[pltpu.VMEM((tb, HG_HEADS, HG_DK, HG_DK), F32)],
        compiler_params=_cparams(("parallel", "arbitrary")),
        name="hgrn2",
    )(proj, proj, proj, proj, lb, nw, mstack, masks, s0)


S5_TILES = MIX_W // LANE
S5_TLANES = S5_LANES // S5_TILES


def _s5_kernel(x_ref, g_ref, wsu_ref, wbu_ref, are_ref, aim_ref, wc_ref, d_ref, wglu_ref, h0r_ref, h0i_ref,
               o_ref, hr_ref, hi_ref, bur, bui, *, tb, tl):
    @pl.when(pl.program_id(1) == 0)
    def _():
        hr_ref[...] = h0r_ref[...]
        hi_ref[...] = h0i_ref[...]

    rows = tl * tb
    x = x_ref[...].reshape(rows, D_MODEL)
    u = _dot(_rms(x, g_ref[...]).astype(BF16), wsu_ref[...])
    for t in range(S5_TILES):
        bu = _dot(u[:, t * LANE:(t + 1) * LANE].astype(BF16), wbu_ref[t])
        ls = slice(t * S5_TLANES, (t + 1) * S5_TLANES)
        bur[:, ls] = bu[:, :S5_TLANES]
        bui[:, ls] = bu[:, S5_TLANES:]

    for t in range(S5_TILES):
        ls = slice(t * S5_TLANES, (t + 1) * S5_TLANES)
        ar = jnp.broadcast_to(are_ref[:, ls], (tb, S5_TLANES))
        ai = jnp.broadcast_to(aim_ref[:, ls], (tb, S5_TLANES))

        def step(i, carry, ls=ls, ar=ar, ai=ai):
            hr, hi = carry
            r = pl.multiple_of(i * tb, tb)
            nr = (ar * hr - ai * hi) + bur[pl.ds(r, tb), ls]
            ni = (ar * hi + ai * hr) + bui[pl.ds(r, tb), ls]
            bur[pl.ds(r, tb), ls] = nr
            bui[pl.ds(r, tb), ls] = ni
            return nr, ni

        hr, hi = lax.fori_loop(0, tl, step, (hr_ref[:, ls], hi_ref[:, ls]))
        hr_ref[:, ls] = hr
        hi_ref[:, ls] = hi

    ys = []
    for t in range(S5_TILES):
        ls = slice(t * S5_TLANES, (t + 1) * S5_TLANES)
        hcat = jnp.concatenate([bur[:, ls], bui[:, ls]], axis=1).astype(BF16)
        cs = slice(t * LANE, (t + 1) * LANE)
        ys.append(_dot(hcat, wc_ref[t]) + d_ref[:, cs] * u[:, cs])
    z = _gelu_tanh(jnp.concatenate(ys, axis=1))
    out = z * _sigmoid(_dot(z.astype(BF16), wglu_ref[...]))
    o_ref[...] = out.reshape(tl, tb, MIX_W)


def _s5_call(xt, g, wsu, wbu, are, aim, wc, d, wglu, h0r, h0i, *, tb, tl):
    l, b, _ = xt.shape
    assert l % tl == 0 and b % tb == 0 and tb == SUBLANE
    h_spec = pl.BlockSpec((tb, S5_LANES), lambda bi, li: (bi, 0))
    params = (g, wsu, wbu, are, aim, wc, d, wglu)
    return pl.pallas_call(
        functools.partial(_s5_kernel, tb=tb, tl=tl),
        grid=(b // tb, l // tl),
        in_specs=[pl.BlockSpec((tl, tb, D_MODEL), lambda bi, li: (li, bi, 0))]
        + [_const_spec(a.shape) for a in params] + [h_spec, h_spec],
        out_specs=[pl.BlockSpec((tl, tb, MIX_W), lambda bi, li: (li, bi, 0)), h_spec, h_spec],
        out_shape=[jax.ShapeDtypeStruct((l, b, MIX_W), F32), jax.ShapeDtypeStruct(h0r.shape, F32),
                   jax.ShapeDtypeStruct(h0i.shape, F32)],
        scratch_shapes=[pltpu.VMEM((tl * tb, S5_LANES), F32), pltpu.VMEM((tl * tb, S5_LANES), F32)],
        compiler_params=_cparams(("parallel", "arbitrary")),
        name="s5",
    )(xt, *params, h0r, h0i)


def _s5_params(p):
    a_re, a_im = p['s5_a_re'], p['s5_a_im']
    dt = jnp.exp(p['s5_log_dt'])[:, None]
    mag = jnp.exp(dt * a_re)
    ab_re, ab_im = mag * jnp.cos(dt * a_im), mag * jnp.sin(dt * a_im)
    den = a_re * a_re + a_im * a_im
    n_re, n_im = ab_re - 1.0, ab_im
    f_re = (n_re * a_re + n_im * a_im) / den
    f_im = (n_im * a_re - n_re * a_im) / den
    b_re, b_im = p['s5_b_re'], p['s5_b_im']
    bb_re = f_re[..., None] * b_re - f_im[..., None] * b_im
    bb_im = f_re[..., None] * b_im + f_im[..., None] * b_re
    gpt = S5_GROUPS // S5_TILES
    eye = jnp.eye(gpt, dtype=F32)

    def pack_b(bb):
        bt = bb.reshape(S5_TILES, gpt, S5_STATE, S5_GROUP)
        w = jnp.einsum('tgpm,gh->tgmhp', bt, eye)
        return w.reshape(S5_TILES, gpt * S5_GROUP, gpt * S5_STATE)

    def pack_c(cc):
        ct = cc.reshape(S5_TILES, gpt, S5_GROUP, S5_STATE)
        w = jnp.einsum('tgmp,gh->tgphm', ct, eye)
        return w.reshape(S5_TILES, gpt * S5_STATE, gpt * S5_GROUP)

    wbu = jnp.concatenate([pack_b(bb_re), pack_b(bb_im)], axis=2).astype(BF16)
    wc = jnp.concatenate([pack_c(p['s5_c_re']), -pack_c(p['s5_c_im'])], axis=1).astype(BF16)
    return (wbu, ab_re.reshape(1, S5_LANES), ab_im.reshape(1, S5_LANES), wc, p['s5_d'].reshape(1, MIX_W))


def _merge_kernel(x_ref, gl_ref, o0, o1, o2, o3, wb_ref, wo_ref, g_ref, out_ref):
    m = None
    for n, o in enumerate((o0, o1, o2, o3)):
        t = _dot(o[...].astype(BF16), wb_ref[n])
        t = _sigmoid(gl_ref[:, n * D_MODEL:(n + 1) * D_MODEL]) * t
        m = t if m is None else m + t
    mo = _dot(m.astype(BF16), wo_ref[...])
    out_ref[...] = x_ref[...] + _rms(mo, g_ref[...])


def _merge_call(x2, proj, branches, wb, wo, g, *, tm):
    t = x2.shape[0]
    row = lambda i: (i, 0)
    return pl.pallas_call(
        _merge_kernel,
        grid=(t // tm,),
        in_specs=[pl.BlockSpec((tm, D_MODEL), row), pl.BlockSpec((tm, GATE_COLS), row)]
        + [pl.BlockSpec((tm, MIX_W), row)] * N_BRANCH
        + [_const_spec(wb.shape), _const_spec(wo.shape), _const_spec(g.shape)],
        out_specs=pl.BlockSpec((tm, D_MODEL), row),
        out_shape=jax.ShapeDtypeStruct((t, D_MODEL), F32),
        compiler_params=_cparams(("parallel",)),
        name="merge",
    )(x2, proj, *branches, wb, wo, g)


def _ffn_kernel(x_ref, g1_ref, w1_ref, w2_ref, g2_ref, o_ref, h_ref, acc_ref):
    j = pl.program_id(1)

    @pl.when(j == 0)
    def _():
        h_ref[...] = _rms(x_ref[...], g1_ref[...]).astype(BF16)
        acc_ref[...] = jnp.zeros_like(acc_ref)

    u = jnp.maximum(_dot(h_ref[...], w1_ref[...]), 0.0)
    acc_ref[...] += _dot((u * u).astype(BF16), w2_ref[...])

    @pl.when(j == pl.num_programs(1) - 1)
    def _():
        o_ref[...] = x_ref[...] + _rms(acc_ref[...], g2_ref[...])


def _ffn_call(x2, g1, w1, w2, g2, *, tm, tf):
    t = x2.shape[0]
    return pl.pallas_call(
        _ffn_kernel,
        grid=(t // tm, D_FF // tf),
        in_specs=[
            pl.BlockSpec((tm, D_MODEL), lambda i, j: (i, 0)),
            pl.BlockSpec((1, D_MODEL), lambda i, j: (0, 0)),
            pl.BlockSpec((D_MODEL, tf), lambda i, j: (0, j)),
            pl.BlockSpec((tf, D_MODEL), lambda i, j: (j, 0)),
            pl.BlockSpec((1, D_MODEL), lambda i, j: (0, 0)),
        ],
        out_specs=pl.BlockSpec((tm, D_MODEL), lambda i, j: (i, 0)),
        out_shape=jax.ShapeDtypeStruct((t, D_MODEL), F32),
        scratch_shapes=[pltpu.VMEM((tm, D_MODEL), BF16), pltpu.VMEM((tm, D_MODEL), F32)],
        compiler_params=_cparams(("parallel", "arbitrary")),
        name="ffn",
    )(x2, g1, w1, w2, g2)


def _prep_layer(p, lb):
    w_in = p['w_in']
    o_sdt = 4 * MIX_W + MIX_W + SSD_CONV_DIM
    o_hq = o_sdt + SSD_HEADS
    o_su = o_hq + 4 * MIX_W
    o_gl = o_su + MIX_W
    w_main = jnp.concatenate([
        w_in[:, o_gl:o_gl + GATE_COLS], w_in[:, 0:o_sdt], w_in[:, o_hq:o_su], w_in[:, o_sdt:o_hq],
        jnp.zeros((D_MODEL, MIX_W - SSD_HEADS), F32)], axis=1).astype(BF16)
    assert w_main.shape[1] == PROJ_COLS

    def pad_heads(v):
        return jnp.pad(v.astype(F32), (0, LANE - SSD_HEADS)).reshape(1, LANE)

    q = dict(
        w_main=w_main,
        w_su=w_in[:, o_su:o_su + MIX_W].astype(BF16),
        g_pre_mix=p['g_pre_mix'].reshape(1, D_MODEL),
        g_post_mix=p['g_post_mix'].reshape(1, D_MODEL),
        g_pre_ffn=p['g_pre_ffn'].reshape(1, D_MODEL),
        g_post_ffn=p['g_post_ffn'].reshape(1, D_MODEL),
        ret_gn=p['ret_gn'],
        conv_w=p['ssd_conv_w'],
        conv_b=p['ssd_conv_b'].reshape(1, SSD_CONV_DIM),
        a_log=pad_heads(p['ssd_a_log']),
        dt_bias=pad_heads(p['ssd_dt_bias']),
        d_skip=jnp.repeat(p['ssd_d'].astype(F32), SSD_HEADDIM).reshape(1, MIX_W),
        ssd_norm=p['ssd_norm'].reshape(1, MIX_W),
        hg_lb=lb.reshape(HG_HEADS, HG_DK),
        hg_norm=p['hg_norm'].reshape(1, HG_DK),
        s5=_s5_params(p),
        w_glu=p['s5_w_glu'].astype(BF16),
        w_branch=p['w_branch'].astype(BF16),
        w_out=p['w_out'].astype(BF16),
        w_ff1=p['w_ff1'].astype(BF16),
        w_ff2=p['w_ff2'].astype(BF16),
    )
    return q


def _layer(x3, st, q, pos, cfg):
    b, l, _ = x3.shape
    x2 = x3.reshape(b * l, D_MODEL)
    tb, tl, c = cfg['tb'], cfg['tl'], cfg['c']
    proj = _proj_call(x2, q['g_pre_mix'], q['w_main'], tm=cfg['tm'], tn=cfg['tn'])
    cos, sin = _rope_tables(pos)
    o_ret, s_ret = _ret_call(proj, cos, sin, q['ret_gn'], st['ret'], b=b, l=l, tb=tb, tl=tl, c=c)
    o_ssd, conv_new, s_ssd = _ssd_call(proj, q['conv_w'], q['conv_b'], q['a_log'], q['dt_bias'], q['d_skip'],
                                       q['ssd_norm'], st['conv'], st['ssd'], b=b, l=l, tb=tb, tl=tl, c=c)
    o_hg, s_hg = _hg_call(proj, q['hg_lb'], q['hg_norm'], st['hgrn'], b=b, l=l, tb=tb, tl=tl, c=c)
    wbu, are, aim, wc, d5 = q['s5']
    xt = jnp.transpose(x3, (1, 0, 2))
    o_s5t, s5_re, s5_im = _s5_call(xt, q['g_pre_mix'], q['w_su'], wbu, are, aim, wc, d5, q['w_glu'],
                                   st['s5_re'].reshape(b, S5_LANES), st['s5_im'].reshape(b, S5_LANES),
                                   tb=SUBLANE, tl=cfg['s5_tl'])
    o_s5 = jnp.transpose(o_s5t, (1, 0, 2)).reshape(b * l, MIX_W)
    x2 = _merge_call(x2, proj, (o_ret, o_ssd, o_hg, o_s5), q['w_branch'], q['w_out'], q['g_post_mix'], tm=cfg['tm_merge'])
    x2 = _ffn_call(x2, q['g_pre_ffn'], q['w_ff1'], q['w_ff2'], q['g_post_ffn'], tm=cfg['tm'], tf=cfg['tf'])
    new = {'ret': s_ret, 'ssd': s_ssd, 'conv': conv_new, 'hgrn': s_hg,
           's5_re': s5_re.reshape(b, S5_GROUPS, S5_STATE), 's5_im': s5_im.reshape(b, S5_GROUPS, S5_STATE)}
    return x2.reshape(b, l, D_MODEL), new


_NAMES = ('ret', 'ssd', 'conv', 'hgrn', 's5_re', 's5_im')


def _trunk(x, states, layers, pos, cfg):
    collected = {n: [] for n in _NAMES}
    for li, q in enumerate(layers):
        st = {n: states[n][li] for n in _NAMES}
        x, new = _layer(x, st, q, pos, cfg)
        for n in _NAMES:
            collected[n].append(new[n])
    return x, {n: jnp.stack(collected[n]) for n in _NAMES}


def _group_cfg(b, l):
    if l % CHUNK == 0:
        return dict(tb=1, tl=min(l, 512), c=CHUNK, tm=1024, tn=1024, tm_merge=256, tf=1024, s5_tl=min(l, 64))
    return dict(tb=SUBLANE, tl=l, c=l, tm=min(b * l, 1024), tn=1024, tm_merge=min(b * l, 256), tf=1024, s5_tl=l)


def kernel(x_prompt, x_sample, state_ret, state_ssd, state_conv, state_hgrn, state_s5_re, state_s5_im, g_pre_mix, g_post_mix, g_pre_ffn, g_post_ffn, w_in, ret_gn, ssd_conv_w, ssd_conv_b, ssd_a_log, ssd_dt_bias, ssd_d, ssd_norm, hg_lb_logits, hg_norm, s5_a_re, s5_a_im, s5_b_re, s5_b_im, s5_c_re, s5_c_im, s5_d, s5_log_dt, s5_w_glu, w_branch, w_out, w_ff1, w_ff2):
    params = {
        'g_pre_mix': g_pre_mix, 'g_post_mix': g_post_mix, 'g_pre_ffn': g_pre_ffn, 'g_post_ffn': g_post_ffn,
        'w_in': w_in, 'ret_gn': ret_gn, 'ssd_conv_w': ssd_conv_w, 'ssd_conv_b': ssd_conv_b,
        'ssd_a_log': ssd_a_log, 'ssd_dt_bias': ssd_dt_bias, 'ssd_d': ssd_d, 'ssd_norm': ssd_norm,
        'hg_norm': hg_norm, 's5_a_re': s5_a_re, 's5_a_im': s5_a_im, 's5_b_re': s5_b_re, 's5_b_im': s5_b_im,
        's5_c_re': s5_c_re, 's5_c_im': s5_c_im, 's5_d': s5_d, 's5_log_dt': s5_log_dt, 's5_w_glu': s5_w_glu,
        'w_branch': w_branch, 'w_out': w_out, 'w_ff1': w_ff1, 'w_ff2': w_ff2,
    }
    depth = w_in.shape[0]
    w = jax.nn.softmax(hg_lb_logits.astype(F32), axis=0)
    lbs = jnp.cumsum(w, axis=0) - w[0]
    layers = [_prep_layer({k: v[i] for k, v in params.items()}, lbs[i]) for i in range(depth)]

    bp, lp, _ = x_prompt.shape
    bs, ls, _ = x_sample.shape
    zero_states = {
        'ret': jnp.zeros((depth, bp) + state_ret.shape[2:], F32),
        'ssd': jnp.zeros((depth, bp) + state_ssd.shape[2:], F32),
        'conv': jnp.zeros((depth, bp) + state_conv.shape[2:], F32),
        'hgrn': jnp.zeros((depth, bp) + state_hgrn.shape[2:], F32),
        's5_re': jnp.zeros((depth, bp) + state_s5_re.shape[2:], F32),
        's5_im': jnp.zeros((depth, bp) + state_s5_im.shape[2:], F32),
    }
    sample_states = {'ret': state_ret, 'ssd': state_ssd, 'conv': state_conv, 'hgrn': state_hgrn,
                     's5_re': state_s5_re, 's5_im': state_s5_im}
    pos_p = np.arange(lp)
    pos_s = PAST_LEN + np.arange(ls)
    y_p, new_p = _trunk(x_prompt, zero_states, layers, pos_p, _group_cfg(bp, lp))
    y_s, new_s = _trunk(x_sample, sample_states, layers, pos_s, _group_cfg(bs, ls))
    return (y_p, y_s,
            new_p['ret'], new_s['ret'], new_p['ssd'], new_s['ssd'], new_p['conv'], new_s['conv'],
            new_p['hgrn'], new_s['hgrn'], new_p['s5_re'], new_s['s5_re'], new_p['s5_im'], new_s['s5_im'])
```

```python
import functools
import math

import jax
import jax.numpy as jnp
import numpy as np
from jax import lax
from jax.experimental import pallas as pl
from jax.experimental.pallas import tpu as pltpu

F32 = jnp.float32
BF16 = jnp.bfloat16

D_MODEL = 1024
DEPTH = 2
PAST_LEN = 16384
MIX_W = D_MODEL // 2
N_BRANCH = 4
RET_HEADS = 4
RET_DK = MIX_W // RET_HEADS
ROPE_BASE = 10000.0
SSD_HEADDIM = 64
SSD_HEADS = MIX_W // SSD_HEADDIM
SSD_GROUPS = 2
SSD_STATE = 128
SSD_CONV = 4
SSD_CONV_DIM = MIX_W + 2 * SSD_GROUPS * SSD_STATE
SSD_PAIRS = SSD_HEADS // 2
HG_HEADS = 4
HG_DK = MIX_W // HG_HEADS
S5_GROUP = 16
S5_GROUPS = MIX_W // S5_GROUP
S5_STATE = 64
S5_LANES = S5_GROUPS * S5_STATE
D_FF = 4 * D_MODEL
GATE_COLS = N_BRANCH * D_MODEL
CHUNK = 64
CR = 64
EPS = 1e-6

LANE = 128
SUBLANE = 8
VMEM_LIMIT = 48 * 1024 * 1024

COL_GL = 0
COL_RET = GATE_COLS
COL_SZ = COL_RET + 4 * MIX_W
COL_XS = COL_SZ + MIX_W
COL_BC = COL_XS + MIX_W
COL_HG = COL_BC + MIX_W
COL_DT = COL_HG + 4 * MIX_W
PROJ_COLS = COL_DT + MIX_W


def _sigmoid(x):
    return 1.0 / (1.0 + jnp.exp(-x))


def _silu(x):
    return x * _sigmoid(x)


def _softplus(x):
    return jnp.maximum(x, 0.0) + jnp.log1p(jnp.exp(-jnp.abs(x)))


def _gelu_tanh(x):
    c = math.sqrt(2.0 / math.pi)
    return 0.5 * x * (1.0 + jnp.tanh(c * (x + 0.044715 * (x * x * x))))


def _rms(x, g):
    ms = jnp.mean(x * x, axis=-1, keepdims=True)
    return x * lax.rsqrt(ms + EPS) * g


def _dot(a, b):
    return jnp.dot(a, b, preferred_element_type=F32)


def _dot_nt(a, b):
    return lax.dot_general(a, b, (((1,), (1,)), ((), ())), preferred_element_type=F32)


def _split3(a):
    hi = a.astype(BF16)
    r1 = a - hi.astype(F32)
    mid = r1.astype(BF16)
    lo = (r1 - mid.astype(F32)).astype(BF16)
    return hi, mid, lo


def _sel_l(m, a):
    hi, mid, lo = _split3(a)
    return (_dot(m, lo) + _dot(m, mid)) + _dot(m, hi)


def _sel2_l(m, a):
    hi = a.astype(BF16)
    mid = (a - hi.astype(F32)).astype(BF16)
    return _dot(m, mid) + _dot(m, hi)


def _sel_r(a, m):
    hi, mid, lo = _split3(a)
    return (_dot(lo, m) + _dot(mid, m)) + _dot(hi, m)


def _cparams(sem):
    return pltpu.CompilerParams(dimension_semantics=sem, vmem_limit_bytes=VMEM_LIMIT)


def _const_spec(shape):
    nd = len(shape)
    return pl.BlockSpec(shape, lambda *_: (0,) * nd)


def _proj_kernel(x_ref, g_ref, w_ref, o_ref, h_ref):
    @pl.when(pl.program_id(1) == 0)
    def _():
        h_ref[...] = _rms(x_ref[...], g_ref[...]).astype(BF16)

    o_ref[...] = _dot(h_ref[...], w_ref[...])


def _proj_call(x2, g, w, *, tm, tn):
    t, d = x2.shape
    n = w.shape[1]
    return pl.pallas_call(
        _proj_kernel,
        grid=(t // tm, n // tn),
        in_specs=[
            pl.BlockSpec((tm, d), lambda i, j: (i, 0)),
            pl.BlockSpec((1, d), lambda i, j: (0, 0)),
            pl.BlockSpec((d, tn), lambda i, j: (0, j)),
        ],
        out_specs=pl.BlockSpec((tm, tn), lambda i, j: (i, j)),
        out_shape=jax.ShapeDtypeStruct((t, n), F32),
        scratch_shapes=[pltpu.VMEM((tm, d), BF16)],
        compiler_params=_cparams(("parallel", "arbitrary")),
        name="in_proj",
    )(x2, g, w)


def _ret_consts(c):
    h = np.arange(RET_HEADS, dtype=np.float64)
    log_g = np.log(1.0 - 2.0 ** (-5.0 - h))
    idx = np.arange(c, dtype=np.float64)
    diff = idx[:, None] - idx[None, :]
    dmat = np.where(diff >= 0, np.exp(log_g[:, None, None] * np.maximum(diff, 0.0)), 0.0)
    q_dec = np.exp(log_g[:, None] * (idx[None, :] + 1.0))
    k_dec = np.exp(log_g[:, None] * (c - 1.0 - idx[None, :]))
    chunk_dec = np.exp(log_g * c)
    qd = np.broadcast_to(q_dec[:, :, None], (RET_HEADS, c, LANE))
    kd = np.broadcast_to(k_dec[:, :, None], (RET_HEADS, c, LANE))
    return (jnp.asarray(dmat, F32), jnp.asarray(qd, F32), jnp.asarray(kd, F32),
            tuple(float(v) for v in chunk_dec))


def _rope_tables(pos):
    half = RET_DK // 2
    inv = ROPE_BASE ** (-np.arange(half, dtype=np.float64) / half)
    ang = np.asarray(pos, np.float64)[:, None] * inv[None, :]
    cos, sin = np.cos(ang), np.sin(ang)
    return (jnp.asarray(np.concatenate([cos, cos], axis=1), F32),
            jnp.asarray(np.concatenate([-sin, sin], axis=1), F32))


def _seq_chunk_index(idx, tb, nch):
    if nch == 1:
        return idx, 0
    return 0, idx


def _ret_kernel(q_ref, k_ref, v_ref, g_ref, cos_ref, sin_ref, dmat_ref, qd_ref, kd_ref, gn_ref, s0_ref,
                o_ref, st_ref, *, tb, tl, c, cdec, unroll):
    @pl.when(pl.program_id(1) == 0)
    def _():
        st_ref[...] = s0_ref[...]

    nch = tl // c
    scale = RET_DK ** -0.5

    def body(idx, carry):
        s, ch = _seq_chunk_index(idx, tb, nch)
        r0 = pl.multiple_of(idx * c, c)
        t0 = pl.multiple_of(ch * c, c)
        cos = cos_ref[pl.ds(t0, c), :]
        sin = sin_ref[pl.ds(t0, c), :]
        for h in range(RET_HEADS):
            sl = slice(h * LANE, (h + 1) * LANE)
            q = q_ref[pl.ds(r0, c), sl]
            k = k_ref[pl.ds(r0, c), sl]
            v = v_ref[pl.ds(r0, c), sl].astype(BF16)
            qr = q * cos + pltpu.roll(q, LANE // 2, axis=1) * sin
            kr = (k * cos + pltpu.roll(k, LANE // 2, axis=1) * sin) * scale
            att = _dot_nt(qr.astype(BF16), kr.astype(BF16)) * dmat_ref[h]
            st = st_ref[s, h]
            o = _dot(att.astype(BF16), v) + _dot((qr * qd_ref[h]).astype(BF16), st.astype(BF16))
            kdt = (kr * kd_ref[h]).T.astype(BF16)
            st_ref[s, h] = st * cdec[h] + _dot(kdt, v)
            mu = jnp.mean(o, axis=-1, keepdims=True)
            oc = o - mu
            var = jnp.mean(oc * oc, axis=-1, keepdims=True)
            on = oc * lax.rsqrt(var + EPS) * gn_ref[h:h + 1, :]
            o_ref[pl.ds(r0, c), sl] = on * _silu(g_ref[pl.ds(r0, c), sl])
        return carry

    lax.fori_loop(0, tb * nch, body, 0, unroll=unroll)


def _seq_grid(b, l, tb, tl):
    assert b % tb == 0 and l % tl == 0 and (tb == 1 or tl == l)
    nl = l // tl
    rows = tb * tl

    def row_map(col_blk):
        return lambda bi, li: (bi * nl + li, col_blk)

    return (b // tb, nl), rows, row_map


def _ret_call(proj, cos, sin, gn, s0, *, b, l, tb, tl, c, unroll):
    grid, rows, row_map = _seq_grid(b, l, tb, tl)
    dmat, qd, kd, cdec = _ret_consts(c)
    cb = COL_RET // MIX_W
    state_spec = pl.BlockSpec((tb, RET_HEADS, RET_DK, RET_DK), lambda bi, li: (bi, 0, 0, 0))
    return pl.pallas_call(
        functools.partial(_ret_kernel, tb=tb, tl=tl, c=c, cdec=cdec, unroll=unroll),
        grid=grid,
        in_specs=[pl.BlockSpec((rows, MIX_W), row_map(cb + i)) for i in range(4)] + [
            pl.BlockSpec((tl, LANE), lambda bi, li: (li, 0)),
            pl.BlockSpec((tl, LANE), lambda bi, li: (li, 0)),
            _const_spec(dmat.shape), _const_spec(qd.shape), _const_spec(kd.shape), _const_spec(gn.shape),
            state_spec,
        ],
        out_specs=[pl.BlockSpec((rows, MIX_W), row_map(0)), state_spec],
        out_shape=[jax.ShapeDtypeStruct((b * l, MIX_W), F32), jax.ShapeDtypeStruct(s0.shape, F32)],
        compiler_params=_cparams(("parallel", "arbitrary")),
        name="retention",
    )(proj, proj, proj, proj, cos, sin, dmat, qd, kd, gn, s0)


def _ssd_consts(sl):
    nseq = CR // sl
    tril = np.kron(np.eye(nseq), np.tril(np.ones((sl, sl))))
    sumall = np.kron(np.eye(nseq), np.ones((sl, sl)))
    ones = np.ones((CR, CR))
    i = np.arange(CR)[:, None]
    j = np.arange(MIX_W)[None, :] % SSD_HEADDIM
    eye = (i == j).astype(np.float64)
    mask = ((i // sl == j // sl) & (i >= j)).astype(np.float64)
    return (jnp.asarray(np.concatenate([tril, sumall], axis=0), BF16), jnp.asarray(ones, BF16),
            jnp.asarray(eye, F32), jnp.asarray(mask, F32))


def _ssd_kernel(z_ref, xs_ref, bc_ref, dtx_ref, cw_ref, cb_ref, alog_ref, dtb_ref, dsk_ref, nw_ref,
                mcum_ref, ones_ref, eye_ref, mask_ref, cs0_ref, s0_ref,
                o_ref, cs_ref, so_ref, xpad, st, *, tb, tl, sl, unroll):
    li = pl.program_id(1)
    nl = pl.num_programs(1)
    nseq = CR // sl
    kc = SSD_CONV - 1
    off = SUBLANE - kc
    grp = 2 * SUBLANE

    @pl.when(li == 0)
    def _():
        for s in range(tb):
            for p in range(SSD_PAIRS):
                st[s, p] = s0_ref[s, p].T
        if nseq == 1:
            xpad[0:SUBLANE, :] = jnp.zeros((SUBLANE, SSD_CONV_DIM), F32)
            xpad[off:SUBLANE, :] = cs0_ref[0]

    if nseq == 1:
        xpad[SUBLANE:SUBLANE + tl, 0:MIX_W] = xs_ref[...]
        xpad[SUBLANE:SUBLANE + tl, MIX_W:2 * MIX_W] = bc_ref[...]
    else:
        for s in range(tb):
            xpad[s * grp:s * grp + SUBLANE, :] = jnp.zeros((SUBLANE, SSD_CONV_DIM), F32)
            xpad[s * grp + off:s * grp + SUBLANE, :] = cs0_ref[s]
            xpad[s * grp + SUBLANE:(s + 1) * grp, 0:MIX_W] = xs_ref[s * sl:(s + 1) * sl, :]
            xpad[s * grp + SUBLANE:(s + 1) * grp, MIX_W:2 * MIX_W] = bc_ref[s * sl:(s + 1) * sl, :]
            cs_ref[s] = xpad[(s + 1) * grp - kc:(s + 1) * grp, :]

    a_neg = -jnp.exp(alog_ref[...])
    half = lax.broadcasted_iota(jnp.int32, (CR, LANE), 1) < SSD_HEADDIM

    def body(pc, carry):
        r0 = pl.multiple_of(pc * CR, CR)
        if nseq == 1:
            win = xpad[pl.ds(r0, CR + SUBLANE), :]
            take = lambda a: a[SUBLANE:, :]
        else:
            win = xpad[pl.ds(pl.multiple_of(pc * nseq * grp, nseq * grp), nseq * grp), :]
            take = lambda a: a.reshape(nseq, grp, SSD_CONV_DIM)[:, SUBLANE:, :].reshape(CR, SSD_CONV_DIM)
        acc = cb_ref[...] + take(pltpu.roll(win, kc, axis=0)) * cw_ref[0:1, :]
        for j in range(1, kc):
            acc = acc + take(pltpu.roll(win, kc - j, axis=0)) * cw_ref[j:j + 1, :]
        xc = _silu(acc + take(win) * cw_ref[kc:kc + 1, :])
        xs = xc[:, 0:MIX_W]
        bm = xc[:, MIX_W:MIX_W + SSD_GROUPS * SSD_STATE]
        cm = xc[:, MIX_W + SSD_GROUPS * SSD_STATE:2 * MIX_W]

        dtx = _softplus(dtx_ref[pl.ds(r0, CR), :] + dtb_ref[...])
        both = _sel_l(mcum_ref[...], dtx * a_neg)
        cum, clast = both[0:CR], both[CR:2 * CR]
        ecum = jnp.exp(cum)
        dend = jnp.exp(clast - cum)
        elast = jnp.exp(clast)
        rr = _sel_l(ones_ref[...], cum * eye_ref[...])
        msk = mask_ref[...]
        lmat = jnp.exp(jnp.where(msk > 0, cum - rr, 0.0)) * msk
        zz = z_ref[pl.ds(r0, CR), :]
        for g in range(SSD_GROUPS):
            gs = slice(g * SSD_STATE, (g + 1) * SSD_STATE)
            bm_g = bm[:, gs]
            cm_g = cm[:, gs]
            cm_gb = cm_g.astype(BF16)
            cb2 = _dot_nt(cm_gb, jnp.concatenate([bm_g, bm_g], axis=0).astype(BF16))
            ys = []
            for pp in range(SSD_PAIRS // SSD_GROUPS):
                p = g * (SSD_PAIRS // SSD_GROUPS) + pp
                ps = slice(p * LANE, (p + 1) * LANE)
                xs_p = xs[:, ps]
                xdt = xs_p * dtx[:, ps]
                xst = jnp.concatenate([jnp.where(half, xdt, 0.0), jnp.where(half, 0.0, xdt)], axis=0).astype(BF16)
                y = _dot((cb2 * lmat[:, ps]).astype(BF16), xst)
                xd = xdt * dend[:, ps]
                yis = []
                for s in range(nseq):
                    seq = pc * nseq + s if nseq > 1 else 0
                    if nseq == 1:
                        cm_s, bm_s, xd_s = cm_gb, bm_g, xd
                    else:
                        rs = slice(s * sl, (s + 1) * sl)
                        cm_s, bm_s, xd_s = cm_g[rs].astype(BF16), bm_g[rs], xd[rs]
                    stp = st[seq, p]
                    yis.append(_dot(cm_s, stp.astype(BF16)))
                    st[seq, p] = stp * elast[s * sl:s * sl + 1, ps] + _dot(bm_s.T.astype(BF16), xd_s.astype(BF16))
                yi = yis[0] if nseq == 1 else jnp.concatenate(yis, axis=0)
                ys.append(y + yi * ecum[:, ps] + xs_p * dsk_ref[:, ps])
            gw = SSD_HEADDIM * SSD_HEADS // SSD_GROUPS
            zs = slice(g * gw, (g + 1) * gw)
            yg = jnp.concatenate(ys, axis=1) * _silu(zz[:, zs])
            o_ref[pl.ds(r0, CR), zs] = _rms(yg, nw_ref[:, zs])
        return carry

    lax.fori_loop(0, tb * tl // CR, body, 0, unroll=unroll)

    if nseq == 1:
        xpad[0:SUBLANE, :] = xpad[tl:tl + SUBLANE, :]

    @pl.when(li == nl - 1)
    def _():
        if nseq == 1:
            cs_ref[0] = xpad[tl + off:tl + SUBLANE, :]
        for s in range(tb):
            for p in range(SSD_PAIRS):
                so_ref[s, p] = st[s, p].T


def _ssd_call(proj, cw, cb, alog, dtb, dsk, nw, cs0, s0, *, b, l, tb, tl, sl, unroll):
    grid, rows, row_map = _seq_grid(b, l, tb, tl)
    assert rows % CR == 0 and CR % sl == 0 and (sl == CR and tb == 1 or sl == tl == SUBLANE)
    consts = _ssd_consts(sl)
    s0p = s0.reshape(b, SSD_PAIRS, LANE, SSD_STATE)
    cs_spec = pl.BlockSpec((tb, SSD_CONV - 1, SSD_CONV_DIM), lambda bi, li: (bi, 0, 0))
    st_spec = pl.BlockSpec((tb, SSD_PAIRS, LANE, SSD_STATE), lambda bi, li: (bi, 0, 0, 0))
    params = (cw, cb, alog, dtb, dsk, nw)
    xpad_rows = SUBLANE + tl if sl == CR else tb * 2 * SUBLANE
    o, cs, so = pl.pallas_call(
        functools.partial(_ssd_kernel, tb=tb, tl=tl, sl=sl, unroll=unroll),
        grid=grid,
        in_specs=[pl.BlockSpec((rows, MIX_W), row_map(cb_)) for cb_ in
                  (COL_SZ // MIX_W, COL_XS // MIX_W, COL_BC // MIX_W, COL_DT // MIX_W)]
        + [_const_spec(a.shape) for a in params + consts] + [cs_spec, st_spec],
        out_specs=[pl.BlockSpec((rows, MIX_W), row_map(0)), cs_spec, st_spec],
        out_shape=[jax.ShapeDtypeStruct((b * l, MIX_W), F32), jax.ShapeDtypeStruct(cs0.shape, F32),
                   jax.ShapeDtypeStruct(s0p.shape, F32)],
        scratch_shapes=[pltpu.VMEM((xpad_rows, SSD_CONV_DIM), F32),
                        pltpu.VMEM((tb, SSD_PAIRS, SSD_STATE, LANE), F32)],
        compiler_params=_cparams(("parallel", "arbitrary")),
        name="ssd",
    )(proj, proj, proj, proj, *params, *consts, cs0, s0p)
    return o, cs, so.reshape(s0.shape)


def _hg_levels(sl):
    ms = []
    m = sl // 2
    while m >= 1:
        ms.append(m)
        m //= 2
    return ms


def _hg_consts(sl):
    nseq = CR // sl
    tril = np.kron(np.eye(nseq), np.tril(np.ones((sl, sl))))
    sumall = np.kron(np.eye(nseq), np.ones((sl, sl)))
    mats = []
    masks = []
    i = np.arange(CR)
    for m in _hg_levels(sl):
        ref = (i // (2 * m)) * (2 * m) + m - 1
        if m > 1:
            mats.append(tril - tril[ref])
        same = (i[:, None] // (2 * m)) == (i[None, :] // (2 * m))
        qside = ((i // m) % 2 == 1)[:, None]
        kside = ((i // m) % 2 == 0)[None, :]
        masks.append((same & qside & kside).astype(np.float64))
    return (jnp.asarray(np.concatenate([tril, sumall], axis=0), BF16), jnp.asarray(np.concatenate(mats, axis=0), BF16),
            jnp.asarray(np.stack(masks), F32))


def _hg_kernel(q_ref, f_ref, i_ref, g_ref, lb_ref, nw_ref, mcum_ref, mlev_ref, mask_ref, s0_ref, o_ref, so_ref, st,
               *, tb, tl, sl, unroll):
    li = pl.program_id(1)
    nl = pl.num_programs(1)

    @pl.when(li == 0)
    def _():
        for s in range(tb):
            for h in range(HG_HEADS):
                st[s, h] = s0_ref[s, h].T

    nseq = CR // sl
    levels = _hg_levels(sl)

    odd_row = (lax.broadcasted_iota(jnp.int32, (CR, MIX_W), 0) & 1) == 1

    def body(pc, carry):
        r0 = pl.multiple_of(pc * CR, CR)
        lb = lb_ref[...]
        q_all = _silu(q_ref[pl.ds(r0, CR), :])
        sig = _sigmoid(f_ref[pl.ds(r0, CR), :])
        fdec = lb + (1.0 - lb) * sig
        logf = jnp.log(fdec)
        k_all = (1.0 - lb) * (1.0 - sig)
        v_all = i_ref[pl.ds(r0, CR), :]
        cums = _sel_l(mcum_ref[...], logf)
        gc_all = cums[0:CR]
        glast_all = cums[CR:2 * CR]
        es = []
        if len(levels) > 1:
            dl = _sel2_l(mlev_ref[...], logf)
            es = [jnp.exp(-jnp.abs(dl[n * CR:(n + 1) * CR])) for n in range(len(levels) - 1)]
        es.append(jnp.where(odd_row, fdec, 1.0))
        qg_all = q_all * jnp.exp(gc_all)
        kd_all = k_all * jnp.exp(glast_all - gc_all)
        elast_all = jnp.exp(glast_all)
        for h in range(HG_HEADS):
            hs = slice(h * LANE, (h + 1) * LANE)
            q, k, v = q_all[:, hs], k_all[:, hs], v_all[:, hs]
            att = None
            for n, e_all in enumerate(es):
                e = e_all[:, hs]
                a = _dot_nt((q * e).astype(BF16), (k * e).astype(BF16)) * mask_ref[n]
                att = a if att is None else att + a
            o = _dot(att.astype(BF16), v.astype(BF16)) + jnp.sum(q * k, axis=-1, keepdims=True) * v
            qg, kd, elast = qg_all[:, hs], kd_all[:, hs], elast_all[:, hs]
            ois = []
            for s in range(nseq):
                seq = pc * nseq + s if nseq > 1 else 0
                rs = slice(s * sl, (s + 1) * sl)
                stt = st[seq, h]
                ois.append(_dot_nt(qg[rs].astype(BF16), stt.astype(BF16)))
                st[seq, h] = stt * elast[s * sl:s * sl + 1, :] + _dot(v[rs].T.astype(BF16), kd[rs].astype(BF16))
            o = o + (ois[0] if nseq == 1 else jnp.concatenate(ois, axis=0))
            o_ref[pl.ds(r0, CR), hs] = _rms(o, nw_ref[...]) * _sigmoid(g_ref[pl.ds(r0, CR), hs])
        return carry

    lax.fori_loop(0, tb * tl // CR, body, 0, unroll=unroll)

    @pl.when(li == nl - 1)
    def _():
        for s in range(tb):
            for h in range(HG_HEADS):
                so_ref[s, h] = st[s, h].T


def _hg_call(proj, lb, nw, s0, *, b, l, tb, tl, sl, unroll):
    grid, rows, row_map = _seq_grid(b, l, tb, tl)
    assert rows % CR == 0 and CR % sl == 0 and (sl == CR and tb == 1 or sl == tl)
    consts = _hg_consts(sl)
    cb = COL_HG // MIX_W
    st_spec = pl.BlockSpec((tb, HG_HEADS, HG_DK, HG_DK), lambda bi, li: (bi, 0, 0, 0))
    return pl.pallas_call(
        functools.partial(_hg_kernel, tb=tb, tl=tl, sl=sl, unroll=unroll),
        grid=grid,
        in_specs=[pl.BlockSpec((rows, MIX_W), row_map(cb + i)) for i in range(4)]
        + [_const_spec(a.shape) for a in (lb, nw) + consts] + [st_spec],
        out_specs=[pl.BlockSpec((rows, MIX_W), row_map(0)), st_spec],
        out_shape=[jax.ShapeDtypeStruct((b * l, MIX_W), F32), jax.ShapeDtypeStruct(s0.shape, F32)],
        scratch_shapes=[pltpu.VMEM((tb, HG_HEADS, HG_DK, HG_DK), F32)],
        compiler_params=_cparams(("parallel", "arbitrary")),
        name="hgrn2",
    )(proj, proj, proj, proj, lb, nw, *consts, s0)


S5_TILES = MIX_W // LANE
S5_TLANES = S5_LANES // S5_TILES


def _s5_kernel(x_ref, g_ref, wsu_ref, wbu_ref, are_ref, aim_ref, wc_ref, d_ref, wglu_ref, h0r_ref, h0i_ref,
               o_ref, hr_ref, hi_ref, bur, bui, *, tb, tl):
    @pl.when(pl.program_id(1) == 0)
    def _():
        hr_ref[...] = h0r_ref[...]
        hi_ref[...] = h0i_ref[...]

    rows = tl * tb
    x = x_ref[...].reshape(rows, D_MODEL)
    u = _dot(_rms(x, g_ref[...]).astype(BF16), wsu_ref[...])
    for t in range(S5_TILES):
        bu = _dot(u[:, t * LANE:(t + 1) * LANE].astype(BF16), wbu_ref[t])
        ls = slice(t * S5_TLANES, (t + 1) * S5_TLANES)
        bur[:, ls] = bu[:, :S5_TLANES]
        bui[:, ls] = bu[:, S5_TLANES:]

    for t in range(S5_TILES):
        ls = slice(t * S5_TLANES, (t + 1) * S5_TLANES)
        ar = jnp.broadcast_to(are_ref[:, ls], (tb, S5_TLANES))
        ai = jnp.broadcast_to(aim_ref[:, ls], (tb, S5_TLANES))

        def step(i, carry, ls=ls, ar=ar, ai=ai):
            hr, hi = carry
            r = pl.multiple_of(i * tb, tb)
            nr = (ar * hr - ai * hi) + bur[pl.ds(r, tb), ls]
            ni = (ar * hi + ai * hr) + bui[pl.ds(r, tb), ls]
            bur[pl.ds(r, tb), ls] = nr
            bui[pl.ds(r, tb), ls] = ni
            return nr, ni

        hr, hi = lax.fori_loop(0, tl, step, (hr_ref[:, ls], hi_ref[:, ls]))
        hr_ref[:, ls] = hr
        hi_ref[:, ls] = hi

    ys = []
    for t in range(S5_TILES):
        ls = slice(t * S5_TLANES, (t + 1) * S5_TLANES)
        hcat = jnp.concatenate([bur[:, ls], bui[:, ls]], axis=1).astype(BF16)
        cs = slice(t * LANE, (t + 1) * LANE)
        ys.append(_dot(hcat, wc_ref[t]) + d_ref[:, cs] * u[:, cs])
    z = _gelu_tanh(jnp.concatenate(ys, axis=1))
    out = z * _sigmoid(_dot(z.astype(BF16), wglu_ref[...]))
    o_ref[...] = out.reshape(tl, tb, MIX_W)


def _s5_call(xt, g, wsu, wbu, are, aim, wc, d, wglu, h0r, h0i, *, tb, tl):
    l, b, _ = xt.shape
    assert l % tl == 0 and b % tb == 0 and tb == SUBLANE
    h_spec = pl.BlockSpec((tb, S5_LANES), lambda bi, li: (bi, 0))
    params = (g, wsu, wbu, are, aim, wc, d, wglu)
    return pl.pallas_call(
        functools.partial(_s5_kernel, tb=tb, tl=tl),
        grid=(b // tb, l // tl),
        in_specs=[pl.BlockSpec((tl, tb, D_MODEL), lambda bi, li: (li, bi, 0))]
        + [_const_spec(a.shape) for a in params] + [h_spec, h_spec],
        out_specs=[pl.BlockSpec((tl, tb, MIX_W), lambda bi, li: (li, bi, 0)), h_spec, h_spec],
        out_shape=[jax.ShapeDtypeStruct((l, b, MIX_W), F32), jax.ShapeDtypeStruct(h0r.shape, F32),
                   jax.ShapeDtypeStruct(h0i.shape, F32)],
        scratch_shapes=[pltpu.VMEM((tl * tb, S5_LANES), F32), pltpu.VMEM((tl * tb, S5_LANES), F32)],
        compiler_params=_cparams(("parallel", "arbitrary")),
        name="s5",
    )(xt, *params, h0r, h0i)


def _s5_params(p):
    a_re, a_im = p['s5_a_re'], p['s5_a_im']
    dt = jnp.exp(p['s5_log_dt'])[:, None]
    mag = jnp.exp(dt * a_re)
    ab_re, ab_im = mag * jnp.cos(dt * a_im), mag * jnp.sin(dt * a_im)
    den = a_re * a_re + a_im * a_im
    n_re, n_im = ab_re - 1.0, ab_im
    f_re = (n_re * a_re + n_im * a_im) / den
    f_im = (n_im * a_re - n_re * a_im) / den
    b_re, b_im = p['s5_b_re'], p['s5_b_im']
    bb_re = f_re[..., None] * b_re - f_im[..., None] * b_im
    bb_im = f_re[..., None] * b_im + f_im[..., None] * b_re
    gpt = S5_GROUPS // S5_TILES
    eye = jnp.eye(gpt, dtype=F32)

    def pack_b(bb):
        bt = bb.reshape(S5_TILES, gpt, S5_STATE, S5_GROUP)
        w = jnp.einsum('tgpm,gh->tgmhp', bt, eye)
        return w.reshape(S5_TILES, gpt * S5_GROUP, gpt * S5_STATE)

    def pack_c(cc):
        ct = cc.reshape(S5_TILES, gpt, S5_GROUP, S5_STATE)
        w = jnp.einsum('tgmp,gh->tgphm', ct, eye)
        return w.reshape(S5_TILES, gpt * S5_STATE, gpt * S5_GROUP)

    wbu = jnp.concatenate([pack_b(bb_re), pack_b(bb_im)], axis=2).astype(BF16)
    wc = jnp.concatenate([pack_c(p['s5_c_re']), -pack_c(p['s5_c_im'])], axis=1).astype(BF16)
    return (wbu, ab_re.reshape(1, S5_LANES), ab_im.reshape(1, S5_LANES), wc, p['s5_d'].reshape(1, MIX_W))


def _merge_kernel(x_ref, gl_ref, o0, o1, o2, o3, wb_ref, wo_ref, g_ref, out_ref):
    m = None
    for n, o in enumerate((o0, o1, o2, o3)):
        t = _dot(o[...].astype(BF16), wb_ref[n])
        t = _sigmoid(gl_ref[:, n * D_MODEL:(n + 1) * D_MODEL]) * t
        m = t if m is None else m + t
    mo = _dot(m.astype(BF16), wo_ref[...])
    out_ref[...] = x_ref[...] + _rms(mo, g_ref[...])


def _merge_call(x2, proj, branches, wb, wo, g, *, tm):
    t = x2.shape[0]
    row = lambda i: (i, 0)
    return pl.pallas_call(
        _merge_kernel,
        grid=(t // tm,),
        in_specs=[pl.BlockSpec((tm, D_MODEL), row), pl.BlockSpec((tm, GATE_COLS), row)]
        + [pl.BlockSpec((tm, MIX_W), row)] * N_BRANCH
        + [_const_spec(wb.shape), _const_spec(wo.shape), _const_spec(g.shape)],
        out_specs=pl.BlockSpec((tm, D_MODEL), row),
        out_shape=jax.ShapeDtypeStruct((t, D_MODEL), F32),
        compiler_params=_cparams(("parallel",)),
        name="merge",
    )(x2, proj, *branches, wb, wo, g)


def _ffn_kernel(x_ref, g1_ref, w1_ref, w2_ref, g2_ref, o_ref, h_ref, acc_ref):
    j = pl.program_id(1)

    @pl.when(j == 0)
    def _():
        h_ref[...] = _rms(x_ref[...], g1_ref[...]).astype(BF16)
        acc_ref[...] = jnp.zeros_like(acc_ref)

    u = jnp.maximum(_dot(h_ref[...], w1_ref[...]), 0.0)
    acc_ref[...] += _dot((u * u).astype(BF16), w2_ref[...])

    @pl.when(j == pl.num_programs(1) - 1)
    def _():
        o_ref[...] = x_ref[...] + _rms(acc_ref[...], g2_ref[...])


def _ffn_call(x2, g1, w1, w2, g2, *, tm, tf):
    t = x2.shape[0]
    return pl.pallas_call(
        _ffn_kernel,
        grid=(t // tm, D_FF // tf),
        in_specs=[
            pl.BlockSpec((tm, D_MODEL), lambda i, j: (i, 0)),
            pl.BlockSpec((1, D_MODEL), lambda i, j: (0, 0)),
            pl.BlockSpec((D_MODEL, tf), lambda i, j: (0, j)),
            pl.BlockSpec((tf, D_MODEL), lambda i, j: (j, 0)),
            pl.BlockSpec((1, D_MODEL), lambda i, j: (0, 0)),
        ],
        out_specs=pl.BlockSpec((tm, D_MODEL), lambda i, j: (i, 0)),
        out_shape=jax.ShapeDtypeStruct((t, D_MODEL), F32),
        scratch_shapes=[pltpu.VMEM((tm, D_MODEL), BF16), pltpu.VMEM((tm, D_MODEL), F32)],
        compiler_params=_cparams(("parallel", "arbitrary")),
        name="ffn",
    )(x2, g1, w1, w2, g2)


def _prep_layer(p, lb):
    w_in = p['w_in']
    o_sdt = 4 * MIX_W + MIX_W + SSD_CONV_DIM
    o_hq = o_sdt + SSD_HEADS
    o_su = o_hq + 4 * MIX_W
    o_gl = o_su + MIX_W
    w_main = jnp.concatenate([
        w_in[:, o_gl:o_gl + GATE_COLS], w_in[:, 0:o_sdt], w_in[:, o_hq:o_su],
        jnp.repeat(w_in[:, o_sdt:o_hq], SSD_HEADDIM, axis=1)], axis=1).astype(BF16)
    assert w_main.shape[1] == PROJ_COLS

    def head_lanes(v):
        return jnp.repeat(v.astype(F32), SSD_HEADDIM).reshape(1, MIX_W)

    q = dict(
        w_main=w_main,
        w_su=w_in[:, o_su:o_su + MIX_W].astype(BF16),
        g_pre_mix=p['g_pre_mix'].reshape(1, D_MODEL),
        g_post_mix=p['g_post_mix'].reshape(1, D_MODEL),
        g_pre_ffn=p['g_pre_ffn'].reshape(1, D_MODEL),
        g_post_ffn=p['g_post_ffn'].reshape(1, D_MODEL),
        ret_gn=p['ret_gn'],
        conv_w=p['ssd_conv_w'],
        conv_b=p['ssd_conv_b'].reshape(1, SSD_CONV_DIM),
        a_log=head_lanes(p['ssd_a_log']),
        dt_bias=head_lanes(p['ssd_dt_bias']),
        d_skip=head_lanes(p['ssd_d']),
        ssd_norm=p['ssd_norm'].reshape(1, MIX_W),
        hg_lb=lb.reshape(1, MIX_W),
        hg_norm=p['hg_norm'].reshape(1, HG_DK),
        s5=_s5_params(p),
        w_glu=p['s5_w_glu'].astype(BF16),
        w_branch=p['w_branch'].astype(BF16),
        w_out=p['w_out'].astype(BF16),
        w_ff1=p['w_ff1'].astype(BF16),
        w_ff2=p['w_ff2'].astype(BF16),
    )
    return q


def _layer(x3, st, q, pos, cfg):
    b, l, _ = x3.shape
    x2 = x3.reshape(b * l, D_MODEL)
    tb, tl, c = cfg['tb'], cfg['tl'], cfg['c']
    proj = _proj_call(x2, q['g_pre_mix'], q['w_main'], tm=cfg['tm'], tn=cfg['tn'])
    cos, sin = _rope_tables(pos)
    o_ret, s_ret = _ret_call(proj, cos, sin, q['ret_gn'], st['ret'], b=b, l=l, tb=tb, tl=tl, c=c, unroll=cfg['u_ret'])
    o_ssd, conv_new, s_ssd = _ssd_call(proj, q['conv_w'], q['conv_b'], q['a_log'], q['dt_bias'], q['d_skip'],
                                       q['ssd_norm'], st['conv'], st['ssd'], b=b, l=l, tb=tb, tl=tl, sl=c,
                                       unroll=cfg['u_ssd'])
    o_hg, s_hg = _hg_call(proj, q['hg_lb'], q['hg_norm'], st['hgrn'], b=b, l=l, tb=tb, tl=tl, sl=c, unroll=cfg['u_hg'])
    wbu, are, aim, wc, d5 = q['s5']
    xt = jnp.transpose(x3, (1, 0, 2))
    o_s5t, s5_re, s5_im = _s5_call(xt, q['g_pre_mix'], q['w_su'], wbu, are, aim, wc, d5, q['w_glu'],
                                   st['s5_re'].reshape(b, S5_LANES), st['s5_im'].reshape(b, S5_LANES),
                                   tb=SUBLANE, tl=cfg['s5_tl'])
    o_s5 = jnp.transpose(o_s5t, (1, 0, 2)).reshape(b * l, MIX_W)
    x2 = _merge_call(x2, proj, (o_ret, o_ssd, o_hg, o_s5), q['w_branch'], q['w_out'], q['g_post_mix'], tm=cfg['tm_merge'])
    x2 = _ffn_call(x2, q['g_pre_ffn'], q['w_ff1'], q['w_ff2'], q['g_post_ffn'], tm=cfg['tm'], tf=cfg['tf'])
    new = {'ret': s_ret, 'ssd': s_ssd, 'conv': conv_new, 'hgrn': s_hg,
           's5_re': s5_re.reshape(b, S5_GROUPS, S5_STATE), 's5_im': s5_im.reshape(b, S5_GROUPS, S5_STATE)}
    return x2.reshape(b, l, D_MODEL), new


_NAMES = ('ret', 'ssd', 'conv', 'hgrn', 's5_re', 's5_im')


def _trunk(x, states, layers, pos, cfg):
    collected = {n: [] for n in _NAMES}
    for li, q in enumerate(layers):
        st = {n: states[n][li] for n in _NAMES}
        x, new = _layer(x, st, q, pos, cfg)
        for n in _NAMES:
            collected[n].append(new[n])
    return x, {n: jnp.stack(collected[n]) for n in _NAMES}


def _group_cfg(b, l):
    if l % CHUNK == 0:
        return dict(tb=1, tl=min(l, 512), c=CHUNK, tm=1024, tn=1024, tm_merge=256, tf=1024, s5_tl=min(l, 64),
                    u_ret=2, u_ssd=2, u_hg=2)
    return dict(tb=SUBLANE, tl=l, c=l, tm=min(b * l, 1024), tn=1024, tm_merge=min(b * l, 256), tf=1024, s5_tl=l,
                u_ret=4, u_ssd=1, u_hg=1)


def kernel(x_prompt, x_sample, state_ret, state_ssd, state_conv, state_hgrn, state_s5_re, state_s5_im, g_pre_mix, g_post_mix, g_pre_ffn, g_post_ffn, w_in, ret_gn, ssd_conv_w, ssd_conv_b, ssd_a_log, ssd_dt_bias, ssd_d, ssd_norm, hg_lb_logits, hg_norm, s5_a_re, s5_a_im, s5_b_re, s5_b_im, s5_c_re, s5_c_im, s5_d, s5_log_dt, s5_w_glu, w_branch, w_out, w_ff1, w_ff2):
    params = {
        'g_pre_mix': g_pre_mix, 'g_post_mix': g_post_mix, 'g_pre_ffn': g_pre_ffn, 'g_post_ffn': g_post_ffn,
        'w_in': w_in, 'ret_gn': ret_gn, 'ssd_conv_w': ssd_conv_w, 'ssd_conv_b': ssd_conv_b,
        'ssd_a_log': ssd_a_log, 'ssd_dt_bias': ssd_dt_bias, 'ssd_d': ssd_d, 'ssd_norm': ssd_norm,
        'hg_norm': hg_norm, 's5_a_re': s5_a_re, 's5_a_im': s5_a_im, 's5_b_re': s5_b_re, 's5_b_im': s5_b_im,
        's5_c_re': s5_c_re, 's5_c_im': s5_c_im, 's5_d': s5_d, 's5_log_dt': s5_log_dt, 's5_w_glu': s5_w_glu,
        'w_branch': w_branch, 'w_out': w_out, 'w_ff1': w_ff1, 'w_ff2': w_ff2,
    }
    depth = w_in.shape[0]
    w = jax.nn.softmax(hg_lb_logits.astype(F32), axis=0)
    lbs = jnp.cumsum(w, axis=0) - w[0]
    layers = [_prep_layer({k: v[i] for k, v in params.items()}, lbs[i]) for i in range(depth)]

    bp, lp, _ = x_prompt.shape
    bs, ls, _ = x_sample.shape
    zero_states = {
        'ret': jnp.zeros((depth, bp) + state_ret.shape[2:], F32),
        'ssd': jnp.zeros((depth, bp) + state_ssd.shape[2:], F32),
        'conv': jnp.zeros((depth, bp) + state_conv.shape[2:], F32),
        'hgrn': jnp.zeros((depth, bp) + state_hgrn.shape[2:], F32),
        's5_re': jnp.zeros((depth, bp) + state_s5_re.shape[2:], F32),
        's5_im': jnp.zeros((depth, bp) + state_s5_im.shape[2:], F32),
    }
    sample_states = {'ret': state_ret, 'ssd': state_ssd, 'conv': state_conv, 'hgrn': state_hgrn,
                     's5_re': state_s5_re, 's5_im': state_s5_im}
    pos_p = np.arange(lp)
    pos_s = PAST_LEN + np.arange(ls)
    y_p, new_p = _trunk(x_prompt, zero_states, layers, pos_p, _group_cfg(bp, lp))
    y_s, new_s = _trunk(x_sample, sample_states, layers, pos_s, _group_cfg(bs, ls))
    return (y_p, y_s,
            new_p['ret'], new_s['ret'], new_p['ssd'], new_s['ssd'], new_p['conv'], new_s['conv'],
            new_p['hgrn'], new_s['hgrn'], new_p['s5_re'], new_s['s5_re'], new_p['s5_im'], new_s['s5_im'])
```

```python
import functools
import math

import jax
import jax.numpy as jnp
import numpy as np
from jax import lax
from jax.experimental import pallas as pl
from jax.experimental.pallas import tpu as pltpu

F32 = jnp.float32
BF16 = jnp.bfloat16

D_MODEL = 1024
DEPTH = 2
PAST_LEN = 16384
MIX_W = D_MODEL // 2
N_BRANCH = 4
RET_HEADS = 4
RET_DK = MIX_W // RET_HEADS
ROPE_BASE = 10000.0
SSD_HEADDIM = 64
SSD_HEADS = MIX_W // SSD_HEADDIM
SSD_GROUPS = 2
SSD_STATE = 128
SSD_CONV = 4
SSD_CONV_DIM = MIX_W + 2 * SSD_GROUPS * SSD_STATE
SSD_PAIRS = SSD_HEADS // 2
HG_HEADS = 4
HG_DK = MIX_W // HG_HEADS
S5_GROUP = 16
S5_GROUPS = MIX_W // S5_GROUP
S5_STATE = 64
S5_LANES = S5_GROUPS * S5_STATE
D_FF = 4 * D_MODEL
GATE_COLS = N_BRANCH * D_MODEL
CHUNK = 64
CR = 64
EPS = 1e-6

LANE = 128
SUBLANE = 8
VMEM_LIMIT = 48 * 1024 * 1024

COL_GL = 0
COL_RET = GATE_COLS
COL_SZ = COL_RET + 4 * MIX_W
COL_XS = COL_SZ + MIX_W
COL_BC = COL_XS + MIX_W
COL_HG = COL_BC + MIX_W
COL_DT = COL_HG + 4 * MIX_W
COL_SU = COL_DT + MIX_W
PROJ_COLS = COL_SU + MIX_W


def _sigmoid(x):
    return 1.0 / (1.0 + jnp.exp(-x))


def _silu(x):
    return x * _sigmoid(x)


def _softplus(x):
    return jnp.maximum(x, 0.0) + jnp.log1p(jnp.exp(-jnp.abs(x)))


def _gelu_tanh(x):
    c = math.sqrt(2.0 / math.pi)
    return 0.5 * x * (1.0 + jnp.tanh(c * (x + 0.044715 * (x * x * x))))


def _rms(x, g):
    ms = jnp.mean(x * x, axis=-1, keepdims=True)
    return x * lax.rsqrt(ms + EPS) * g


def _dot(a, b):
    return jnp.dot(a, b, preferred_element_type=F32)


def _dot_nt(a, b):
    return lax.dot_general(a, b, (((1,), (1,)), ((), ())), preferred_element_type=F32)


def _split3(a):
    hi = a.astype(BF16)
    r1 = a - hi.astype(F32)
    mid = r1.astype(BF16)
    lo = (r1 - mid.astype(F32)).astype(BF16)
    return hi, mid, lo


def _sel_l(m, a):
    hi, mid, lo = _split3(a)
    return (_dot(m, lo) + _dot(m, mid)) + _dot(m, hi)


def _sel2_l(m, a):
    hi = a.astype(BF16)
    mid = (a - hi.astype(F32)).astype(BF16)
    return _dot(m, mid) + _dot(m, hi)


def _sel_r(a, m):
    hi, mid, lo = _split3(a)
    return (_dot(lo, m) + _dot(mid, m)) + _dot(hi, m)


def _cparams(sem):
    return pltpu.CompilerParams(dimension_semantics=sem, vmem_limit_bytes=VMEM_LIMIT)


def _const_spec(shape):
    nd = len(shape)
    return pl.BlockSpec(shape, lambda *_: (0,) * nd)


def _proj_kernel(x_ref, g_ref, w_ref, o_ref, h_ref):
    @pl.when(pl.program_id(1) == 0)
    def _():
        h_ref[...] = _rms(x_ref[...], g_ref[...]).astype(BF16)

    o_ref[...] = _dot(h_ref[...], w_ref[...]).astype(BF16)


def _proj_call(x2, g, w, *, tm, tn):
    t, d = x2.shape
    n = w.shape[1]
    return pl.pallas_call(
        _proj_kernel,
        grid=(t // tm, n // tn),
        in_specs=[
            pl.BlockSpec((tm, d), lambda i, j: (i, 0)),
            pl.BlockSpec((1, d), lambda i, j: (0, 0)),
            pl.BlockSpec((d, tn), lambda i, j: (0, j)),
        ],
        out_specs=pl.BlockSpec((tm, tn), lambda i, j: (i, j)),
        out_shape=jax.ShapeDtypeStruct((t, n), BF16),
        scratch_shapes=[pltpu.VMEM((tm, d), BF16)],
        compiler_params=_cparams(("parallel", "arbitrary")),
        name="in_proj",
    )(x2, g, w)


def _ret_consts(sl):
    nseq = CR // sl
    h = np.arange(RET_HEADS, dtype=np.float64)
    log_g = np.log(1.0 - 2.0 ** (-5.0 - h))
    idx = np.arange(sl, dtype=np.float64)
    diff = idx[:, None] - idx[None, :]
    dmat = np.where(diff >= 0, np.exp(log_g[:, None, None] * np.maximum(diff, 0.0)), 0.0)
    dmat = np.stack([np.kron(np.eye(nseq), d) for d in dmat])
    q_dec = np.tile(np.exp(log_g[:, None] * (idx[None, :] + 1.0)), (1, nseq))
    k_dec = np.tile(np.exp(log_g[:, None] * (sl - 1.0 - idx[None, :])), (1, nseq))
    chunk_dec = np.exp(log_g * sl)
    qd = np.broadcast_to(q_dec[:, :, None], (RET_HEADS, CR, LANE))
    kd = np.broadcast_to(k_dec[:, :, None], (RET_HEADS, CR, LANE))
    return (jnp.asarray(dmat, F32), jnp.asarray(qd, F32), jnp.asarray(kd, F32),
            tuple(float(v) for v in chunk_dec))


def _rope_tables(pos):
    half = RET_DK // 2
    inv = ROPE_BASE ** (-np.arange(half, dtype=np.float64) / half)
    ang = np.asarray(pos, np.float64)[:, None] * inv[None, :]
    cos, sin = np.cos(ang), np.sin(ang)
    return (jnp.asarray(np.concatenate([cos, cos], axis=1), F32),
            jnp.asarray(np.concatenate([-sin, sin], axis=1), F32))


def _ret_kernel(q_ref, k_ref, v_ref, g_ref, cos_ref, sin_ref, dmat_ref, qd_ref, kd_ref, gn_ref, s0_ref,
                o_ref, st_ref, *, tb, tl, sl, cdec, unroll):
    @pl.when(pl.program_id(1) == 0)
    def _():
        st_ref[...] = s0_ref[...]

    nseq = CR // sl
    scale = RET_DK ** -0.5

    def body(pc, carry):
        r0 = pl.multiple_of(pc * CR, CR)
        t0 = r0 if nseq == 1 else 0
        cos = cos_ref[pl.ds(t0, CR), :]
        sin = sin_ref[pl.ds(t0, CR), :]
        for h in range(RET_HEADS):
            hs = slice(h * LANE, (h + 1) * LANE)
            q = q_ref[pl.ds(r0, CR), hs].astype(F32)
            k = k_ref[pl.ds(r0, CR), hs].astype(F32)
            v = v_ref[pl.ds(r0, CR), hs]
            qr = q * cos + pltpu.roll(q, LANE // 2, axis=1) * sin
            kr = (k * cos + pltpu.roll(k, LANE // 2, axis=1) * sin) * scale
            att = _dot_nt(qr.astype(BF16), kr.astype(BF16)) * dmat_ref[h]
            o = _dot(att.astype(BF16), v)
            qd = qr * qd_ref[h]
            kd = kr * kd_ref[h]
            ois = []
            for s in range(nseq):
                seq = pc * nseq + s if nseq > 1 else 0
                rs = slice(s * sl, (s + 1) * sl)
                st = st_ref[seq, h]
                ois.append(_dot(qd[rs].astype(BF16), st.astype(BF16)))
                v_s = v if nseq == 1 else v.astype(F32)[rs].astype(BF16)
                st_ref[seq, h] = st * cdec[h] + _dot(kd[rs].T.astype(BF16), v_s)
            o = o + (ois[0] if nseq == 1 else jnp.concatenate(ois, axis=0))
            mu = jnp.mean(o, axis=-1, keepdims=True)
            oc = o - mu
            var = jnp.mean(oc * oc, axis=-1, keepdims=True)
            on = oc * lax.rsqrt(var + EPS) * gn_ref[h:h + 1, :]
            o_ref[pl.ds(r0, CR), hs] = (on * _silu(g_ref[pl.ds(r0, CR), hs].astype(F32))).astype(BF16)
        return carry

    lax.fori_loop(0, tb * tl // CR, body, 0, unroll=unroll)


def _seq_grid(b, l, tb, tl):
    assert b % tb == 0 and l % tl == 0 and (tb == 1 or tl == l)
    nl = l // tl
    rows = tb * tl

    def row_map(col_blk):
        return lambda bi, li: (bi * nl + li, col_blk)

    return (b // tb, nl), rows, row_map


def _ret_call(proj, pos, gn, s0, *, b, l, tb, tl, sl, unroll):
    grid, rows, row_map = _seq_grid(b, l, tb, tl)
    assert rows % CR == 0 and CR % sl == 0 and (sl == CR and tb == 1 or sl == tl)
    dmat, qd, kd, cdec = _ret_consts(sl)
    cos, sin = _rope_tables(pos)
    if sl < CR:
        cos, sin = jnp.tile(cos, (CR // sl, 1)), jnp.tile(sin, (CR // sl, 1))
        tab_spec = _const_spec(cos.shape)
    else:
        tab_spec = pl.BlockSpec((tl, LANE), lambda bi, li: (li, 0))
    cb = COL_RET // MIX_W
    state_spec = pl.BlockSpec((tb, RET_HEADS, RET_DK, RET_DK), lambda bi, li: (bi, 0, 0, 0))
    return pl.pallas_call(
        functools.partial(_ret_kernel, tb=tb, tl=tl, sl=sl, cdec=cdec, unroll=unroll),
        grid=grid,
        in_specs=[pl.BlockSpec((rows, MIX_W), row_map(cb + i)) for i in range(4)] + [
            tab_spec, tab_spec,
            _const_spec(dmat.shape), _const_spec(qd.shape), _const_spec(kd.shape), _const_spec(gn.shape),
            state_spec,
        ],
        out_specs=[pl.BlockSpec((rows, MIX_W), row_map(0)), state_spec],
        out_shape=[jax.ShapeDtypeStruct((b * l, MIX_W), BF16), jax.ShapeDtypeStruct(s0.shape, F32)],
        compiler_params=_cparams(("parallel", "arbitrary")),
        name="retention",
    )(proj, proj, proj, proj, cos, sin, dmat, qd, kd, gn, s0)


def _ssd_consts(sl):
    nseq = CR // sl
    tril = np.kron(np.eye(nseq), np.tril(np.ones((sl, sl))))
    sumall = np.kron(np.eye(nseq), np.ones((sl, sl)))
    ones = np.ones((CR, CR))
    i = np.arange(CR)[:, None]
    j = np.arange(MIX_W)[None, :] % SSD_HEADDIM
    eye = (i == j).astype(np.float64)
    mask = ((i // sl == j // sl) & (i >= j)).astype(np.float64)
    return (jnp.asarray(np.concatenate([tril, sumall], axis=0), BF16), jnp.asarray(ones, BF16),
            jnp.asarray(eye, F32), jnp.asarray(mask, F32))


def _ssd_kernel(z_ref, xs_ref, bc_ref, dtx_ref, cw_ref, cb_ref, alog_ref, dtb_ref, dsk_ref, nw_ref,
                mcum_ref, ones_ref, eye_ref, mask_ref, cs0_ref, s0_ref,
                o_ref, cs_ref, so_ref, xpad, st, *, tb, tl, sl, unroll):
    li = pl.program_id(1)
    nl = pl.num_programs(1)
    nseq = CR // sl
    kc = SSD_CONV - 1
    off = SUBLANE - kc
    grp = 2 * SUBLANE

    @pl.when(li == 0)
    def _():
        for s in range(tb):
            for p in range(SSD_PAIRS):
                st[s, p] = s0_ref[s, p].T
        if nseq == 1:
            xpad[0:SUBLANE, :] = jnp.zeros((SUBLANE, SSD_CONV_DIM), F32)
            xpad[off:SUBLANE, :] = cs0_ref[0]

    if nseq == 1:
        xpad[SUBLANE:SUBLANE + tl, 0:MIX_W] = xs_ref[...].astype(F32)
        xpad[SUBLANE:SUBLANE + tl, MIX_W:2 * MIX_W] = bc_ref[...].astype(F32)
    else:
        xs_new = xs_ref[...].astype(F32)
        bc_new = bc_ref[...].astype(F32)
        for s in range(tb):
            xpad[s * grp:s * grp + SUBLANE, :] = jnp.zeros((SUBLANE, SSD_CONV_DIM), F32)
            xpad[s * grp + off:s * grp + SUBLANE, :] = cs0_ref[s]
            xpad[s * grp + SUBLANE:(s + 1) * grp, 0:MIX_W] = xs_new[s * sl:(s + 1) * sl, :]
            xpad[s * grp + SUBLANE:(s + 1) * grp, MIX_W:2 * MIX_W] = bc_new[s * sl:(s + 1) * sl, :]
            cs_ref[s] = xpad[(s + 1) * grp - kc:(s + 1) * grp, :]

    a_neg = -jnp.exp(alog_ref[...])
    half = lax.broadcasted_iota(jnp.int32, (CR, LANE), 1) < SSD_HEADDIM

    def body(pc, carry):
        r0 = pl.multiple_of(pc * CR, CR)
        if nseq == 1:
            win = xpad[pl.ds(r0, CR + SUBLANE), :]
            take = lambda a: a[SUBLANE:, :]
        else:
            win = xpad[pl.ds(pl.multiple_of(pc * nseq * grp, nseq * grp), nseq * grp), :]
            take = lambda a: a.reshape(nseq, grp, SSD_CONV_DIM)[:, SUBLANE:, :].reshape(CR, SSD_CONV_DIM)
        acc = cb_ref[...] + take(pltpu.roll(win, kc, axis=0)) * cw_ref[0:1, :]
        for j in range(1, kc):
            acc = acc + take(pltpu.roll(win, kc - j, axis=0)) * cw_ref[j:j + 1, :]
        xc = _silu(acc + take(win) * cw_ref[kc:kc + 1, :])
        xs = xc[:, 0:MIX_W]
        bm = xc[:, MIX_W:MIX_W + SSD_GROUPS * SSD_STATE]
        cm = xc[:, MIX_W + SSD_GROUPS * SSD_STATE:2 * MIX_W]

        dtx = _softplus(dtx_ref[pl.ds(r0, CR), :].astype(F32) + dtb_ref[...])
        both = _sel_l(mcum_ref[...], dtx * a_neg)
        cum, clast = both[0:CR], both[CR:2 * CR]
        ecum = jnp.exp(cum)
        dend = jnp.exp(clast - cum)
        elast = jnp.exp(clast)
        rr = _sel_l(ones_ref[...], cum * eye_ref[...])
        msk = mask_ref[...]
        lmat = jnp.exp(jnp.where(msk > 0, cum - rr, 0.0)) * msk
        zz = z_ref[pl.ds(r0, CR), :].astype(F32)
        for g in range(SSD_GROUPS):
            gs = slice(g * SSD_STATE, (g + 1) * SSD_STATE)
            bm_g = bm[:, gs]
            cm_g = cm[:, gs]
            cm_gb = cm_g.astype(BF16)
            cb2 = _dot_nt(cm_gb, jnp.concatenate([bm_g, bm_g], axis=0).astype(BF16))
            ys = []
            for pp in range(SSD_PAIRS // SSD_GROUPS):
                p = g * (SSD_PAIRS // SSD_GROUPS) + pp
                ps = slice(p * LANE, (p + 1) * LANE)
                xs_p = xs[:, ps]
                xdt = xs_p * dtx[:, ps]
                xst = jnp.concatenate([jnp.where(half, xdt, 0.0), jnp.where(half, 0.0, xdt)], axis=0).astype(BF16)
                y = _dot((cb2 * lmat[:, ps]).astype(BF16), xst)
                xd = xdt * dend[:, ps]
                yis = []
                for s in range(nseq):
                    seq = pc * nseq + s if nseq > 1 else 0
                    if nseq == 1:
                        cm_s, bm_s, xd_s = cm_gb, bm_g, xd
                    else:
                        rs = slice(s * sl, (s + 1) * sl)
                        cm_s, bm_s, xd_s = cm_g[rs].astype(BF16), bm_g[rs], xd[rs]
                    stp = st[seq, p]
                    yis.append(_dot(cm_s, stp.astype(BF16)))
                    st[seq, p] = stp * elast[s * sl:s * sl + 1, ps] + _dot(bm_s.T.astype(BF16), xd_s.astype(BF16))
                yi = yis[0] if nseq == 1 else jnp.concatenate(yis, axis=0)
                ys.append(y + yi * ecum[:, ps] + xs_p * dsk_ref[:, ps])
            gw = SSD_HEADDIM * SSD_HEADS // SSD_GROUPS
            zs = slice(g * gw, (g + 1) * gw)
            yg = jnp.concatenate(ys, axis=1) * _silu(zz[:, zs])
            o_ref[pl.ds(r0, CR), zs] = _rms(yg, nw_ref[:, zs]).astype(BF16)
        return carry

    lax.fori_loop(0, tb * tl // CR, body, 0, unroll=unroll)

    if nseq == 1:
        xpad[0:SUBLANE, :] = xpad[tl:tl + SUBLANE, :]

    @pl.when(li == nl - 1)
    def _():
        if nseq == 1:
            cs_ref[0] = xpad[tl + off:tl + SUBLANE, :]
        for s in range(tb):
            for p in range(SSD_PAIRS):
                so_ref[s, p] = st[s, p].T


def _ssd_call(proj, cw, cb, alog, dtb, dsk, nw, cs0, s0, *, b, l, tb, tl, sl, unroll):
    grid, rows, row_map = _seq_grid(b, l, tb, tl)
    assert rows % CR == 0 and CR % sl == 0 and (sl == CR and tb == 1 or sl == tl == SUBLANE)
    consts = _ssd_consts(sl)
    s0p = s0.reshape(b, SSD_PAIRS, LANE, SSD_STATE)
    cs_spec = pl.BlockSpec((tb, SSD_CONV - 1, SSD_CONV_DIM), lambda bi, li: (bi, 0, 0))
    st_spec = pl.BlockSpec((tb, SSD_PAIRS, LANE, SSD_STATE), lambda bi, li: (bi, 0, 0, 0))
    params = (cw, cb, alog, dtb, dsk, nw)
    xpad_rows = SUBLANE + tl if sl == CR else tb * 2 * SUBLANE
    o, cs, so = pl.pallas_call(
        functools.partial(_ssd_kernel, tb=tb, tl=tl, sl=sl, unroll=unroll),
        grid=grid,
        in_specs=[pl.BlockSpec((rows, MIX_W), row_map(cb_)) for cb_ in
                  (COL_SZ // MIX_W, COL_XS // MIX_W, COL_BC // MIX_W, COL_DT // MIX_W)]
        + [_const_spec(a.shape) for a in params + consts] + [cs_spec, st_spec],
        out_specs=[pl.BlockSpec((rows, MIX_W), row_map(0)), cs_spec, st_spec],
        out_shape=[jax.ShapeDtypeStruct((b * l, MIX_W), BF16), jax.ShapeDtypeStruct(cs0.shape, F32),
                   jax.ShapeDtypeStruct(s0p.shape, F32)],
        scratch_shapes=[pltpu.VMEM((xpad_rows, SSD_CONV_DIM), F32),
                        pltpu.VMEM((tb, SSD_PAIRS, SSD_STATE, LANE), F32)],
        compiler_params=_cparams(("parallel", "arbitrary")),
        name="ssd",
    )(proj, proj, proj, proj, *params, *consts, cs0, s0p)
    return o, cs, so.reshape(s0.shape)


def _hg_levels(sl):
    ms = []
    m = sl // 2
    while m >= 1:
        ms.append(m)
        m //= 2
    return ms


def _hg_consts(sl):
    nseq = CR // sl
    tril = np.kron(np.eye(nseq), np.tril(np.ones((sl, sl))))
    sumall = np.kron(np.eye(nseq), np.ones((sl, sl)))
    mats = []
    masks = []
    i = np.arange(CR)
    for m in _hg_levels(sl):
        ref = (i // (2 * m)) * (2 * m) + m - 1
        if m > 1:
            mats.append(tril - tril[ref])
        same = (i[:, None] // (2 * m)) == (i[None, :] // (2 * m))
        qside = ((i // m) % 2 == 1)[:, None]
        kside = ((i // m) % 2 == 0)[None, :]
        masks.append((same & qside & kside).astype(np.float64))
    return (jnp.asarray(np.concatenate([tril, sumall], axis=0), BF16), jnp.asarray(np.concatenate(mats, axis=0), BF16),
            jnp.asarray(np.stack(masks), F32))


def _hg_kernel(q_ref, f_ref, i_ref, g_ref, lb_ref, nw_ref, mcum_ref, mlev_ref, mask_ref, s0_ref, o_ref, so_ref, st,
               *, tb, tl, sl, unroll):
    li = pl.program_id(1)
    nl = pl.num_programs(1)

    @pl.when(li == 0)
    def _():
        for s in range(tb):
            for h in range(HG_HEADS):
                st[s, h] = s0_ref[s, h].T

    nseq = CR // sl
    levels = _hg_levels(sl)

    odd_row = (lax.broadcasted_iota(jnp.int32, (CR, MIX_W), 0) & 1) == 1

    def body(pc, carry):
        r0 = pl.multiple_of(pc * CR, CR)
        lb = lb_ref[...]
        q_all = _silu(q_ref[pl.ds(r0, CR), :].astype(F32))
        sig = _sigmoid(f_ref[pl.ds(r0, CR), :].astype(F32))
        fdec = lb + (1.0 - lb) * sig
        logf = jnp.log(fdec)
        k_all = (1.0 - lb) * (1.0 - sig)
        v_all = i_ref[pl.ds(r0, CR), :].astype(F32)
        cums = _sel_l(mcum_ref[...], logf)
        gc_all = cums[0:CR]
        glast_all = cums[CR:2 * CR]
        es = []
        if len(levels) > 1:
            dl = _sel2_l(mlev_ref[...], logf)
            es = [jnp.exp(-jnp.abs(dl[n * CR:(n + 1) * CR])) for n in range(len(levels) - 1)]
        es.append(jnp.where(odd_row, fdec, 1.0))
        qg_all = q_all * jnp.exp(gc_all)
        kd_all = k_all * jnp.exp(glast_all - gc_all)
        elast_all = jnp.exp(glast_all)
        for h in range(HG_HEADS):
            hs = slice(h * LANE, (h + 1) * LANE)
            q, k, v = q_all[:, hs], k_all[:, hs], v_all[:, hs]
            att = None
            for n, e_all in enumerate(es):
                e = e_all[:, hs]
                a = _dot_nt((q * e).astype(BF16), (k * e).astype(BF16)) * mask_ref[n]
                att = a if att is None else att + a
            o = _dot(att.astype(BF16), v.astype(BF16)) + jnp.sum(q * k, axis=-1, keepdims=True) * v
            qg, kd, elast = qg_all[:, hs], kd_all[:, hs], elast_all[:, hs]
            ois = []
            for s in range(nseq):
                seq = pc * nseq + s if nseq > 1 else 0
                rs = slice(s * sl, (s + 1) * sl)
                stt = st[seq, h]
                ois.append(_dot_nt(qg[rs].astype(BF16), stt.astype(BF16)))
                st[seq, h] = stt * elast[s * sl:s * sl + 1, :] + _dot(v[rs].T.astype(BF16), kd[rs].astype(BF16))
            o = o + (ois[0] if nseq == 1 else jnp.concatenate(ois, axis=0))
            gate = _sigmoid(g_ref[pl.ds(r0, CR), hs].astype(F32))
            o_ref[pl.ds(r0, CR), hs] = (_rms(o, nw_ref[...]) * gate).astype(BF16)
        return carry

    lax.fori_loop(0, tb * tl // CR, body, 0, unroll=unroll)

    @pl.when(li == nl - 1)
    def _():
        for s in range(tb):
            for h in range(HG_HEADS):
                so_ref[s, h] = st[s, h].T


def _hg_call(proj, lb, nw, s0, *, b, l, tb, tl, sl, unroll):
    grid, rows, row_map = _seq_grid(b, l, tb, tl)
    assert rows % CR == 0 and CR % sl == 0 and (sl == CR and tb == 1 or sl == tl)
    consts = _hg_consts(sl)
    cb = COL_HG // MIX_W
    st_spec = pl.BlockSpec((tb, HG_HEADS, HG_DK, HG_DK), lambda bi, li: (bi, 0, 0, 0))
    return pl.pallas_call(
        functools.partial(_hg_kernel, tb=tb, tl=tl, sl=sl, unroll=unroll),
        grid=grid,
        in_specs=[pl.BlockSpec((rows, MIX_W), row_map(cb + i)) for i in range(4)]
        + [_const_spec(a.shape) for a in (lb, nw) + consts] + [st_spec],
        out_specs=[pl.BlockSpec((rows, MIX_W), row_map(0)), st_spec],
        out_shape=[jax.ShapeDtypeStruct((b * l, MIX_W), BF16), jax.ShapeDtypeStruct(s0.shape, F32)],
        scratch_shapes=[pltpu.VMEM((tb, HG_HEADS, HG_DK, HG_DK), F32)],
        compiler_params=_cparams(("parallel", "arbitrary")),
        name="hgrn2",
    )(proj, proj, proj, proj, lb, nw, *consts, s0)


S5_TILES = MIX_W // LANE
S5_TLANES = S5_LANES // S5_TILES


def _s5_perm(tb, tl):
    rows = tb * tl
    p = np.zeros((rows, rows))
    t, s = np.meshgrid(np.arange(tl), np.arange(tb), indexing='ij')
    p[(s * tl + t).ravel(), (t * tb + s).ravel()] = 1.0
    return jnp.asarray(p, BF16), jnp.asarray(p.T, BF16)


def _s5_kernel(su_ref, perm_ref, permt_ref, wbu_ref, are_ref, aim_ref, wc_ref, d_ref, wglu_ref, h0r_ref, h0i_ref,
               o_ref, hr_ref, hi_ref, bur, bui, *, tb, tl):
    @pl.when(pl.program_id(1) == 0)
    def _():
        hr_ref[...] = h0r_ref[...]
        hi_ref[...] = h0i_ref[...]

    rows = tl * tb
    u = _dot(permt_ref[...], su_ref[...].reshape(rows, MIX_W))
    for t in range(S5_TILES):
        bu = _dot(u[:, t * LANE:(t + 1) * LANE].astype(BF16), wbu_ref[t])
        ls = slice(t * S5_TLANES, (t + 1) * S5_TLANES)
        bur[:, ls] = bu[:, :S5_TLANES]
        bui[:, ls] = bu[:, S5_TLANES:]

    for t in range(S5_TILES):
        ls = slice(t * S5_TLANES, (t + 1) * S5_TLANES)
        ar = jnp.broadcast_to(are_ref[:, ls], (tb, S5_TLANES))
        ai = jnp.broadcast_to(aim_ref[:, ls], (tb, S5_TLANES))

        def step(i, carry, ls=ls, ar=ar, ai=ai):
            hr, hi = carry
            r = pl.multiple_of(i * tb, tb)
            nr = (ar * hr - ai * hi) + bur[pl.ds(r, tb), ls]
            ni = (ar * hi + ai * hr) + bui[pl.ds(r, tb), ls]
            bur[pl.ds(r, tb), ls] = nr
            bui[pl.ds(r, tb), ls] = ni
            return nr, ni

        hr, hi = lax.fori_loop(0, tl, step, (hr_ref[:, ls], hi_ref[:, ls]))
        hr_ref[:, ls] = hr
        hi_ref[:, ls] = hi

    ys = []
    for t in range(S5_TILES):
        ls = slice(t * S5_TLANES, (t + 1) * S5_TLANES)
        hcat = jnp.concatenate([bur[:, ls], bui[:, ls]], axis=1).astype(BF16)
        cs = slice(t * LANE, (t + 1) * LANE)
        ys.append(_dot(hcat, wc_ref[t]) + d_ref[:, cs] * u[:, cs])
    z = _gelu_tanh(jnp.concatenate(ys, axis=1))
    out = z * _sigmoid(_dot(z.astype(BF16), wglu_ref[...]))
    o_ref[...] = _dot(perm_ref[...], out.astype(BF16)).astype(BF16).reshape(o_ref.shape)


def _s5_call(proj, wbu, are, aim, wc, d, wglu, h0r, h0i, *, b, l, tb, tl):
    assert l % tl == 0 and b % tb == 0 and tb == SUBLANE
    perm, permt = _s5_perm(tb, tl)
    h_spec = pl.BlockSpec((tb, S5_LANES), lambda bi, li: (bi, 0))
    params = (perm, permt, wbu, are, aim, wc, d, wglu)
    cb = COL_SU // MIX_W
    if tl == l:
        su, su_spec = proj, pl.BlockSpec((tb * tl, MIX_W), lambda bi, li: (bi, cb))
        o_shape, o_spec = (b * l, MIX_W), pl.BlockSpec((tb * tl, MIX_W), lambda bi, li: (bi, 0))
    else:
        su, su_spec = proj.reshape(b, l, -1), pl.BlockSpec((tb, tl, MIX_W), lambda bi, li: (bi, li, cb))
        o_shape, o_spec = (b, l, MIX_W), pl.BlockSpec((tb, tl, MIX_W), lambda bi, li: (bi, li, 0))
    o, hr, hi = pl.pallas_call(
        functools.partial(_s5_kernel, tb=tb, tl=tl),
        grid=(b // tb, l // tl),
        in_specs=[su_spec] + [_const_spec(a.shape) for a in params] + [h_spec, h_spec],
        out_specs=[o_spec, h_spec, h_spec],
        out_shape=[jax.ShapeDtypeStruct(o_shape, BF16), jax.ShapeDtypeStruct(h0r.shape, F32),
                   jax.ShapeDtypeStruct(h0i.shape, F32)],
        scratch_shapes=[pltpu.VMEM((tl * tb, S5_LANES), F32), pltpu.VMEM((tl * tb, S5_LANES), F32)],
        compiler_params=_cparams(("parallel", "arbitrary")),
        name="s5",
    )(su, *params, h0r, h0i)
    return o.reshape(b * l, MIX_W), hr, hi


def _s5_params(p):
    a_re, a_im = p['s5_a_re'], p['s5_a_im']
    dt = jnp.exp(p['s5_log_dt'])[:, None]
    mag = jnp.exp(dt * a_re)
    ab_re, ab_im = mag * jnp.cos(dt * a_im), mag * jnp.sin(dt * a_im)
    den = a_re * a_re + a_im * a_im
    n_re, n_im = ab_re - 1.0, ab_im
    f_re = (n_re * a_re + n_im * a_im) / den
    f_im = (n_im * a_re - n_re * a_im) / den
    b_re, b_im = p['s5_b_re'], p['s5_b_im']
    bb_re = f_re[..., None] * b_re - f_im[..., None] * b_im
    bb_im = f_re[..., None] * b_im + f_im[..., None] * b_re
    gpt = S5_GROUPS // S5_TILES
    eye = jnp.eye(gpt, dtype=F32)

    def pack_b(bb):
        bt = bb.reshape(S5_TILES, gpt, S5_STATE, S5_GROUP)
        w = jnp.einsum('tgpm,gh->tgmhp', bt, eye)
        return w.reshape(S5_TILES, gpt * S5_GROUP, gpt * S5_STATE)

    def pack_c(cc):
        ct = cc.reshape(S5_TILES, gpt, S5_GROUP, S5_STATE)
        w = jnp.einsum('tgmp,gh->tgphm', ct, eye)
        return w.reshape(S5_TILES, gpt * S5_STATE, gpt * S5_GROUP)

    wbu = jnp.concatenate([pack_b(bb_re), pack_b(bb_im)], axis=2).astype(BF16)
    wc = jnp.concatenate([pack_c(p['s5_c_re']), -pack_c(p['s5_c_im'])], axis=1).astype(BF16)
    return (wbu, ab_re.reshape(1, S5_LANES), ab_im.reshape(1, S5_LANES), wc, p['s5_d'].reshape(1, MIX_W))


def _merge_kernel(x_ref, gl_ref, o0, o1, o2, o3, wb_ref, wo_ref, g_ref, out_ref):
    m = None
    for n, o in enumerate((o0, o1, o2, o3)):
        t = _dot(o[...], wb_ref[n])
        t = _sigmoid(gl_ref[:, n * D_MODEL:(n + 1) * D_MODEL].astype(F32)) * t
        m = t if m is None else m + t
    mo = _dot(m.astype(BF16), wo_ref[...])
    out_ref[...] = x_ref[...] + _rms(mo, g_ref[...])


def _merge_call(x2, proj, branches, wb, wo, g, *, tm):
    t = x2.shape[0]
    row = lambda i: (i, 0)
    return pl.pallas_call(
        _merge_kernel,
        grid=(t // tm,),
        in_specs=[pl.BlockSpec((tm, D_MODEL), row), pl.BlockSpec((tm, GATE_COLS), row)]
        + [pl.BlockSpec((tm, MIX_W), row)] * N_BRANCH
        + [_const_spec(wb.shape), _const_spec(wo.shape), _const_spec(g.shape)],
        out_specs=pl.BlockSpec((tm, D_MODEL), row),
        out_shape=jax.ShapeDtypeStruct((t, D_MODEL), F32),
        compiler_params=_cparams(("parallel",)),
        name="merge",
    )(x2, proj, *branches, wb, wo, g)


def _ffn_kernel(x_ref, g1_ref, w1_ref, w2_ref, g2_ref, o_ref, h_ref, acc_ref):
    j = pl.program_id(1)

    @pl.when(j == 0)
    def _():
        h_ref[...] = _rms(x_ref[...], g1_ref[...]).astype(BF16)
        acc_ref[...] = jnp.zeros_like(acc_ref)

    u = jnp.maximum(_dot(h_ref[...], w1_ref[...]), 0.0)
    acc_ref[...] += _dot((u * u).astype(BF16), w2_ref[...])

    @pl.when(j == pl.num_programs(1) - 1)
    def _():
        o_ref[...] = x_ref[...] + _rms(acc_ref[...], g2_ref[...])


def _ffn_call(x2, g1, w1, w2, g2, *, tm, tf):
    t = x2.shape[0]
    return pl.pallas_call(
        _ffn_kernel,
        grid=(t // tm, D_FF // tf),
        in_specs=[
            pl.BlockSpec((tm, D_MODEL), lambda i, j: (i, 0)),
            pl.BlockSpec((1, D_MODEL), lambda i, j: (0, 0)),
            pl.BlockSpec((D_MODEL, tf), lambda i, j: (0, j)),
            pl.BlockSpec((tf, D_MODEL), lambda i, j: (j, 0)),
            pl.BlockSpec((1, D_MODEL), lambda i, j: (0, 0)),
        ],
        out_specs=pl.BlockSpec((tm, D_MODEL), lambda i, j: (i, 0)),
        out_shape=jax.ShapeDtypeStruct((t, D_MODEL), F32),
        scratch_shapes=[pltpu.VMEM((tm, D_MODEL), BF16), pltpu.VMEM((tm, D_MODEL), F32)],
        compiler_params=_cparams(("parallel", "arbitrary")),
        name="ffn",
    )(x2, g1, w1, w2, g2)


def _prep_layer(p, lb):
    w_in = p['w_in']
    o_sdt = 4 * MIX_W + MIX_W + SSD_CONV_DIM
    o_hq = o_sdt + SSD_HEADS
    o_su = o_hq + 4 * MIX_W
    o_gl = o_su + MIX_W
    w_main = jnp.concatenate([
        w_in[:, o_gl:o_gl + GATE_COLS], w_in[:, 0:o_sdt], w_in[:, o_hq:o_su],
        jnp.repeat(w_in[:, o_sdt:o_hq], SSD_HEADDIM, axis=1), w_in[:, o_su:o_gl]], axis=1).astype(BF16)
    assert w_main.shape[1] == PROJ_COLS

    def head_lanes(v):
        return jnp.repeat(v.astype(F32), SSD_HEADDIM).reshape(1, MIX_W)

    q = dict(
        w_main=w_main,
        g_pre_mix=p['g_pre_mix'].reshape(1, D_MODEL),
        g_post_mix=p['g_post_mix'].reshape(1, D_MODEL),
        g_pre_ffn=p['g_pre_ffn'].reshape(1, D_MODEL),
        g_post_ffn=p['g_post_ffn'].reshape(1, D_MODEL),
        ret_gn=p['ret_gn'],
        conv_w=p['ssd_conv_w'],
        conv_b=p['ssd_conv_b'].reshape(1, SSD_CONV_DIM),
        a_log=head_lanes(p['ssd_a_log']),
        dt_bias=head_lanes(p['ssd_dt_bias']),
        d_skip=head_lanes(p['ssd_d']),
        ssd_norm=p['ssd_norm'].reshape(1, MIX_W),
        hg_lb=lb.reshape(1, MIX_W),
        hg_norm=p['hg_norm'].reshape(1, HG_DK),
        s5=_s5_params(p),
        w_glu=p['s5_w_glu'].astype(BF16),
        w_branch=p['w_branch'].astype(BF16),
        w_out=p['w_out'].astype(BF16),
        w_ff1=p['w_ff1'].astype(BF16),
        w_ff2=p['w_ff2'].astype(BF16),
    )
    return q


def _layer(x3, st, q, pos, cfg):
    b, l, _ = x3.shape
    x2 = x3.reshape(b * l, D_MODEL)
    tb, tl, c = cfg['tb'], cfg['tl'], cfg['c']
    proj = _proj_call(x2, q['g_pre_mix'], q['w_main'], tm=cfg['tm_proj'], tn=cfg['tn'])
    o_ret, s_ret = _ret_call(proj, pos, q['ret_gn'], st['ret'], b=b, l=l, tb=tb, tl=tl, sl=c, unroll=cfg['u_ret'])
    o_ssd, conv_new, s_ssd = _ssd_call(proj, q['conv_w'], q['conv_b'], q['a_log'], q['dt_bias'], q['d_skip'],
                                       q['ssd_norm'], st['conv'], st['ssd'], b=b, l=l, tb=tb, tl=tl, sl=c,
                                       unroll=cfg['u_ssd'])
    o_hg, s_hg = _hg_call(proj, q['hg_lb'], q['hg_norm'], st['hgrn'], b=b, l=l, tb=tb, tl=tl, sl=c, unroll=cfg['u_hg'])
    wbu, are, aim, wc, d5 = q['s5']
    o_s5, s5_re, s5_im = _s5_call(proj, wbu, are, aim, wc, d5, q['w_glu'],
                                  st['s5_re'].reshape(b, S5_LANES), st['s5_im'].reshape(b, S5_LANES),
                                  b=b, l=l, tb=SUBLANE, tl=cfg['s5_tl'])
    x2 = _merge_call(x2, proj, (o_ret, o_ssd, o_hg, o_s5), q['w_branch'], q['w_out'], q['g_post_mix'], tm=cfg['tm_merge'])
    x2 = _ffn_call(x2, q['g_pre_ffn'], q['w_ff1'], q['w_ff2'], q['g_post_ffn'], tm=cfg['tm'], tf=cfg['tf'])
    new = {'ret': s_ret, 'ssd': s_ssd, 'conv': conv_new, 'hgrn': s_hg,
           's5_re': s5_re.reshape(b, S5_GROUPS, S5_STATE), 's5_im': s5_im.reshape(b, S5_GROUPS, S5_STATE)}
    return x2.reshape(b, l, D_MODEL), new


_NAMES = ('ret', 'ssd', 'conv', 'hgrn', 's5_re', 's5_im')


def _trunk(x, states, layers, pos, cfg):
    collected = {n: [] for n in _NAMES}
    for li, q in enumerate(layers):
        st = {n: states[n][li] for n in _NAMES}
        x, new = _layer(x, st, q, pos, cfg)
        for n in _NAMES:
            collected[n].append(new[n])
    return x, {n: jnp.stack(collected[n]) for n in _NAMES}


def _group_cfg(b, l):
    if l % CHUNK == 0:
        return dict(tb=1, tl=min(l, 512), c=CHUNK, tm=1024, tm_proj=2048, tn=1536, tm_merge=512, tf=1024,
                    s5_tl=min(l, 64), u_ret=2, u_ssd=2, u_hg=2)
    return dict(tb=SUBLANE, tl=l, c=l, tm=min(b * l, 1024), tm_proj=min(b * l, 1024), tn=1536,
                tm_merge=min(b * l, 512), tf=1024, s5_tl=l, u_ret=1, u_ssd=1, u_hg=1)


def kernel(x_prompt, x_sample, state_ret, state_ssd, state_conv, state_hgrn, state_s5_re, state_s5_im, g_pre_mix, g_post_mix, g_pre_ffn, g_post_ffn, w_in, ret_gn, ssd_conv_w, ssd_conv_b, ssd_a_log, ssd_dt_bias, ssd_d, ssd_norm, hg_lb_logits, hg_norm, s5_a_re, s5_a_im, s5_b_re, s5_b_im, s5_c_re, s5_c_im, s5_d, s5_log_dt, s5_w_glu, w_branch, w_out, w_ff1, w_ff2):
    params = {
        'g_pre_mix': g_pre_mix, 'g_post_mix': g_post_mix, 'g_pre_ffn': g_pre_ffn, 'g_post_ffn': g_post_ffn,
        'w_in': w_in, 'ret_gn': ret_gn, 'ssd_conv_w': ssd_conv_w, 'ssd_conv_b': ssd_conv_b,
        'ssd_a_log': ssd_a_log, 'ssd_dt_bias': ssd_dt_bias, 'ssd_d': ssd_d, 'ssd_norm': ssd_norm,
        'hg_norm': hg_norm, 's5_a_re': s5_a_re, 's5_a_im': s5_a_im, 's5_b_re': s5_b_re, 's5_b_im': s5_b_im,
        's5_c_re': s5_c_re, 's5_c_im': s5_c_im, 's5_d': s5_d, 's5_log_dt': s5_log_dt, 's5_w_glu': s5_w_glu,
        'w_branch': w_branch, 'w_out': w_out, 'w_ff1': w_ff1, 'w_ff2': w_ff2,
    }
    depth = w_in.shape[0]
    w = jax.nn.softmax(hg_lb_logits.astype(F32), axis=0)
    lbs = jnp.cumsum(w, axis=0) - w[0]
    layers = [_prep_layer({k: v[i] for k, v in params.items()}, lbs[i]) for i in range(depth)]

    bp, lp, _ = x_prompt.shape
    bs, ls, _ = x_sample.shape
    zero_states = {
        'ret': jnp.zeros((depth, bp) + state_ret.shape[2:], F32),
        'ssd': jnp.zeros((depth, bp) + state_ssd.shape[2:], F32),
        'conv': jnp.zeros((depth, bp) + state_conv.shape[2:], F32),
        'hgrn': jnp.zeros((depth, bp) + state_hgrn.shape[2:], F32),
        's5_re': jnp.zeros((depth, bp) + state_s5_re.shape[2:], F32),
        's5_im': jnp.zeros((depth, bp) + state_s5_im.shape[2:], F32),
    }
    sample_states = {'ret': state_ret, 'ssd': state_ssd, 'conv': state_conv, 'hgrn': state_hgrn,
                     's5_re': state_s5_re, 's5_im': state_s5_im}
    pos_p = np.arange(lp)
    pos_s = PAST_LEN + np.arange(ls)
    y_p, new_p = _trunk(x_prompt, zero_states, layers, pos_p, _group_cfg(bp, lp))
    y_s, new_s = _trunk(x_sample, sample_states, layers, pos_s, _group_cfg(bs, ls))
    return (y_p, y_s,
            new_p['ret'], new_s['ret'], new_p['ssd'], new_s['ssd'], new_p['conv'], new_s['conv'],
            new_p['hgrn'], new_s['hgrn'], new_p['s5_re'], new_s['s5_re'], new_p['s5_im'], new_s['s5_im'])
```

```python
import functools
import math

import jax
import jax.numpy as jnp
import numpy as np
from jax import lax
from jax.experimental import pallas as pl
from jax.experimental.pallas import tpu as pltpu

F32 = jnp.float32
BF16 = jnp.bfloat16

D_MODEL = 1024
DEPTH = 2
PAST_LEN = 16384
MIX_W = D_MODEL // 2
N_BRANCH = 4
RET_HEADS = 4
RET_DK = MIX_W // RET_HEADS
ROPE_BASE = 10000.0
SSD_HEADDIM = 64
SSD_HEADS = MIX_W // SSD_HEADDIM
SSD_GROUPS = 2
SSD_STATE = 128
SSD_CONV = 4
SSD_CONV_DIM = MIX_W + 2 * SSD_GROUPS * SSD_STATE
SSD_PAIRS = SSD_HEADS // 2
HG_HEADS = 4
HG_DK = MIX_W // HG_HEADS
S5_GROUP = 16
S5_GROUPS = MIX_W // S5_GROUP
S5_STATE = 64
S5_LANES = S5_GROUPS * S5_STATE
D_FF = 4 * D_MODEL
GATE_COLS = N_BRANCH * D_MODEL
CHUNK = 64
CR = 64
EPS = 1e-6

LANE = 128
SUBLANE = 8
VMEM_LIMIT = 48 * 1024 * 1024

COL_GL = 0
COL_RET = GATE_COLS
COL_SZ = COL_RET + 4 * MIX_W
COL_XS = COL_SZ + MIX_W
COL_BC = COL_XS + MIX_W
COL_HG = COL_BC + MIX_W
COL_DT = COL_HG + 4 * MIX_W
COL_SU = COL_DT + MIX_W
PROJ_COLS = COL_SU + MIX_W


def _sigmoid(x):
    return 1.0 / (1.0 + jnp.exp(-x))


def _silu(x):
    return x * _sigmoid(x)


def _softplus(x):
    return jnp.maximum(x, 0.0) + jnp.log1p(jnp.exp(-jnp.abs(x)))


def _gelu_tanh(x):
    c = math.sqrt(2.0 / math.pi)
    return 0.5 * x * (1.0 + jnp.tanh(c * (x + 0.044715 * (x * x * x))))


def _rms(x, g):
    ms = jnp.mean(x * x, axis=-1, keepdims=True)
    return x * lax.rsqrt(ms + EPS) * g


def _dot(a, b):
    return jnp.dot(a, b, preferred_element_type=F32)


def _dot_nt(a, b):
    return lax.dot_general(a, b, (((1,), (1,)), ((), ())), preferred_element_type=F32)


def _split3(a):
    hi = a.astype(BF16)
    r1 = a - hi.astype(F32)
    mid = r1.astype(BF16)
    lo = (r1 - mid.astype(F32)).astype(BF16)
    return hi, mid, lo


def _sel_l(m, a):
    hi, mid, lo = _split3(a)
    return (_dot(m, lo) + _dot(m, mid)) + _dot(m, hi)


def _sel2_l(m, a):
    hi = a.astype(BF16)
    mid = (a - hi.astype(F32)).astype(BF16)
    return _dot(m, mid) + _dot(m, hi)


def _sel_r(a, m):
    hi, mid, lo = _split3(a)
    return (_dot(lo, m) + _dot(mid, m)) + _dot(hi, m)


def _cparams(sem):
    return pltpu.CompilerParams(dimension_semantics=sem, vmem_limit_bytes=VMEM_LIMIT)


def _const_spec(shape):
    nd = len(shape)
    return pl.BlockSpec(shape, lambda *_: (0,) * nd)


def _state_io(s_all, prev, layer, tb):
    dims = tuple(s_all.shape[2:])
    zeros = (0,) * len(dims)
    in_spec = pl.BlockSpec((None, tb) + dims, lambda bi, li: (layer, bi) + zeros)
    if prev is None:
        nprev, prev, prev_spec = 0, s_all, in_spec
    else:
        nprev = prev.shape[0]
        prev_spec = pl.BlockSpec((nprev, tb) + dims, lambda bi, li: (0, bi) + zeros)
    out_spec = pl.BlockSpec((nprev + 1, tb) + dims, lambda bi, li: (0, bi) + zeros)
    out_shape = jax.ShapeDtypeStruct((nprev + 1, s_all.shape[1]) + dims, F32)
    return nprev, prev, in_spec, prev_spec, out_spec, out_shape


def _proj_kernel(x_ref, g_ref, w_ref, o_ref, h_ref):
    @pl.when(pl.program_id(1) == 0)
    def _():
        h_ref[...] = _rms(x_ref[...], g_ref[...]).astype(BF16)

    o_ref[...] = _dot(h_ref[...], w_ref[...]).astype(BF16)


def _proj_call(x2, g, w, *, tm, tn):
    t, d = x2.shape
    n = w.shape[1]
    return pl.pallas_call(
        _proj_kernel,
        grid=(t // tm, n // tn),
        in_specs=[
            pl.BlockSpec((tm, d), lambda i, j: (i, 0)),
            pl.BlockSpec((1, d), lambda i, j: (0, 0)),
            pl.BlockSpec((d, tn), lambda i, j: (0, j)),
        ],
        out_specs=pl.BlockSpec((tm, tn), lambda i, j: (i, j)),
        out_shape=jax.ShapeDtypeStruct((t, n), BF16),
        scratch_shapes=[pltpu.VMEM((tm, d), BF16)],
        compiler_params=_cparams(("parallel", "arbitrary")),
        name="in_proj",
    )(x2, g, w)


def _ret_consts(sl):
    nseq = CR // sl
    h = np.arange(RET_HEADS, dtype=np.float64)
    log_g = np.log(1.0 - 2.0 ** (-5.0 - h))
    idx = np.arange(sl, dtype=np.float64)
    diff = idx[:, None] - idx[None, :]
    dmat = np.where(diff >= 0, np.exp(log_g[:, None, None] * np.maximum(diff, 0.0)), 0.0)
    dmat = np.stack([np.kron(np.eye(nseq), d) for d in dmat])
    q_dec = np.tile(np.exp(log_g[:, None] * (idx[None, :] + 1.0)), (1, nseq))
    k_dec = np.tile(np.exp(log_g[:, None] * (sl - 1.0 - idx[None, :])), (1, nseq))
    chunk_dec = np.exp(log_g * sl)
    qd = np.broadcast_to(q_dec[:, :, None], (RET_HEADS, CR, LANE))
    kd = np.broadcast_to(k_dec[:, :, None], (RET_HEADS, CR, LANE))
    return (jnp.asarray(dmat, F32), jnp.asarray(qd, F32), jnp.asarray(kd, F32),
            tuple(float(v) for v in chunk_dec))


def _rope_tables(pos):
    half = RET_DK // 2
    inv = ROPE_BASE ** (-np.arange(half, dtype=np.float64) / half)
    ang = np.asarray(pos, np.float64)[:, None] * inv[None, :]
    cos, sin = np.cos(ang), np.sin(ang)
    return (jnp.asarray(np.concatenate([cos, cos], axis=1), F32),
            jnp.asarray(np.concatenate([-sin, sin], axis=1), F32))


def _ret_kernel(q_ref, k_ref, v_ref, g_ref, cos_ref, sin_ref, dmat_ref, qd_ref, kd_ref, gn_ref, s0_ref, prev_ref,
                o_ref, so_ref, *, tb, tl, sl, cdec, nprev, unroll):
    st_ref = so_ref.at[nprev]

    @pl.when(pl.program_id(1) == 0)
    def _():
        st_ref[...] = s0_ref[...]
        if nprev:
            so_ref[0:nprev] = prev_ref[...]

    nseq = CR // sl
    scale = RET_DK ** -0.5

    def body(pc, carry):
        r0 = pl.multiple_of(pc * CR, CR)
        t0 = r0 if nseq == 1 else 0
        cos = cos_ref[pl.ds(t0, CR), :]
        sin = sin_ref[pl.ds(t0, CR), :]
        for h in range(RET_HEADS):
            hs = slice(h * LANE, (h + 1) * LANE)
            q = q_ref[pl.ds(r0, CR), hs].astype(F32)
            k = k_ref[pl.ds(r0, CR), hs].astype(F32)
            v = v_ref[pl.ds(r0, CR), hs]
            qr = q * cos + pltpu.roll(q, LANE // 2, axis=1) * sin
            kr = (k * cos + pltpu.roll(k, LANE // 2, axis=1) * sin) * scale
            att = _dot_nt(qr.astype(BF16), kr.astype(BF16)) * dmat_ref[h]
            o = _dot(att.astype(BF16), v)
            qd = qr * qd_ref[h]
            kd = kr * kd_ref[h]
            ois = []
            for s in range(nseq):
                seq = pc * nseq + s if nseq > 1 else 0
                rs = slice(s * sl, (s + 1) * sl)
                st = st_ref[seq, h]
                ois.append(_dot(qd[rs].astype(BF16), st.astype(BF16)))
                v_s = v if nseq == 1 else v.astype(F32)[rs].astype(BF16)
                st_ref[seq, h] = st * cdec[h] + _dot(kd[rs].T.astype(BF16), v_s)
            o = o + (ois[0] if nseq == 1 else jnp.concatenate(ois, axis=0))
            mu = jnp.mean(o, axis=-1, keepdims=True)
            oc = o - mu
            var = jnp.mean(oc * oc, axis=-1, keepdims=True)
            on = oc * lax.rsqrt(var + EPS) * gn_ref[h:h + 1, :]
            o_ref[pl.ds(r0, CR), hs] = (on * _silu(g_ref[pl.ds(r0, CR), hs].astype(F32))).astype(BF16)
        return carry

    lax.fori_loop(0, tb * tl // CR, body, 0, unroll=unroll)


def _seq_grid(b, l, tb, tl):
    assert b % tb == 0 and l % tl == 0 and (tb == 1 or tl == l)
    nl = l // tl
    rows = tb * tl

    def row_map(col_blk):
        return lambda bi, li: (bi * nl + li, col_blk)

    return (b // tb, nl), rows, row_map


def _ret_call(proj, pos, gn, s_all, prev, layer, *, b, l, tb, tl, sl, unroll):
    grid, rows, row_map = _seq_grid(b, l, tb, tl)
    assert rows % CR == 0 and CR % sl == 0 and (sl == CR and tb == 1 or sl == tl)
    dmat, qd, kd, cdec = _ret_consts(sl)
    cos, sin = _rope_tables(pos)
    if sl < CR:
        cos, sin = jnp.tile(cos, (CR // sl, 1)), jnp.tile(sin, (CR // sl, 1))
        tab_spec = _const_spec(cos.shape)
    else:
        tab_spec = pl.BlockSpec((tl, LANE), lambda bi, li: (li, 0))
    cb = COL_RET // MIX_W
    nprev, prev, s_spec, prev_spec, so_spec, so_shape = _state_io(s_all, prev, layer, tb)
    return pl.pallas_call(
        functools.partial(_ret_kernel, tb=tb, tl=tl, sl=sl, cdec=cdec, nprev=nprev, unroll=unroll),
        grid=grid,
        in_specs=[pl.BlockSpec((rows, MIX_W), row_map(cb + i)) for i in range(4)] + [
            tab_spec, tab_spec,
            _const_spec(dmat.shape), _const_spec(qd.shape), _const_spec(kd.shape), _const_spec(gn.shape),
            s_spec, prev_spec,
        ],
        out_specs=[pl.BlockSpec((rows, MIX_W), row_map(0)), so_spec],
        out_shape=[jax.ShapeDtypeStruct((b * l, MIX_W), BF16), so_shape],
        compiler_params=_cparams(("parallel", "arbitrary")),
        name="retention",
    )(proj, proj, proj, proj, cos, sin, dmat, qd, kd, gn, s_all, prev)


def _ssd_consts(sl):
    nseq = CR // sl
    tril = np.kron(np.eye(nseq), np.tril(np.ones((sl, sl))))
    sumall = np.kron(np.eye(nseq), np.ones((sl, sl)))
    ones = np.ones((CR, CR))
    i = np.arange(CR)[:, None]
    j = np.arange(MIX_W)[None, :] % SSD_HEADDIM
    eye = (i == j).astype(np.float64)
    mask = ((i // sl == j // sl) & (i >= j)).astype(np.float64)
    return (jnp.asarray(np.concatenate([tril, sumall], axis=0), BF16), jnp.asarray(ones, BF16),
            jnp.asarray(eye, F32), jnp.asarray(mask, F32))


def _ssd_kernel(z_ref, xs_ref, bc_ref, dtx_ref, cw_ref, cb_ref, alog_ref, dtb_ref, dsk_ref, nw_ref,
                mcum_ref, ones_ref, eye_ref, mask_ref, cs0_ref, s0_ref, prev_ref,
                o_ref, cs_ref, so_ref, xpad, st, *, tb, tl, sl, nprev, unroll):
    li = pl.program_id(1)
    nl = pl.num_programs(1)
    nseq = CR // sl
    kc = SSD_CONV - 1
    off = SUBLANE - kc
    grp = 2 * SUBLANE

    @pl.when(li == 0)
    def _():
        for s in range(tb):
            for p in range(SSD_PAIRS):
                st[s, p] = s0_ref[s, p].T
        if nprev:
            so_ref[0:nprev] = prev_ref[...]
        if nseq == 1:
            xpad[0:SUBLANE, :] = jnp.zeros((SUBLANE, SSD_CONV_DIM), F32)
            xpad[off:SUBLANE, :] = cs0_ref[0]

    if nseq == 1:
        xpad[SUBLANE:SUBLANE + tl, 0:MIX_W] = xs_ref[...].astype(F32)
        xpad[SUBLANE:SUBLANE + tl, MIX_W:2 * MIX_W] = bc_ref[...].astype(F32)
    else:
        xs_new = xs_ref[...].astype(F32)
        bc_new = bc_ref[...].astype(F32)
        for s in range(tb):
            xpad[s * grp:s * grp + SUBLANE, :] = jnp.zeros((SUBLANE, SSD_CONV_DIM), F32)
            xpad[s * grp + off:s * grp + SUBLANE, :] = cs0_ref[s]
            xpad[s * grp + SUBLANE:(s + 1) * grp, 0:MIX_W] = xs_new[s * sl:(s + 1) * sl, :]
            xpad[s * grp + SUBLANE:(s + 1) * grp, MIX_W:2 * MIX_W] = bc_new[s * sl:(s + 1) * sl, :]
            cs_ref[s] = xpad[(s + 1) * grp - kc:(s + 1) * grp, :]

    a_neg = -jnp.exp(alog_ref[...])
    half = lax.broadcasted_iota(jnp.int32, (CR, LANE), 1) < SSD_HEADDIM

    def body(pc, carry):
        r0 = pl.multiple_of(pc * CR, CR)
        if nseq == 1:
            win = xpad[pl.ds(r0, CR + SUBLANE), :]
            take = lambda a: a[SUBLANE:, :]
        else:
            win = xpad[pl.ds(pl.multiple_of(pc * nseq * grp, nseq * grp), nseq * grp), :]
            take = lambda a: a.reshape(nseq, grp, SSD_CONV_DIM)[:, SUBLANE:, :].reshape(CR, SSD_CONV_DIM)
        acc = cb_ref[...] + take(pltpu.roll(win, kc, axis=0)) * cw_ref[0:1, :]
        for j in range(1, kc):
            acc = acc + take(pltpu.roll(win, kc - j, axis=0)) * cw_ref[j:j + 1, :]
        xc = _silu(acc + take(win) * cw_ref[kc:kc + 1, :])
        xs = xc[:, 0:MIX_W]
        bm = xc[:, MIX_W:MIX_W + SSD_GROUPS * SSD_STATE]
        cm = xc[:, MIX_W + SSD_GROUPS * SSD_STATE:2 * MIX_W]

        dtx = _softplus(dtx_ref[pl.ds(r0, CR), :].astype(F32) + dtb_ref[...])
        both = _sel_l(mcum_ref[...], dtx * a_neg)
        cum, clast = both[0:CR], both[CR:2 * CR]
        ecum = jnp.exp(cum)
        dend = jnp.exp(clast - cum)
        elast = jnp.exp(clast)
        rr = _sel_l(ones_ref[...], cum * eye_ref[...])
        msk = mask_ref[...]
        lmat = jnp.exp(jnp.where(msk > 0, cum - rr, 0.0)) * msk
        zz = z_ref[pl.ds(r0, CR), :].astype(F32)
        for g in range(SSD_GROUPS):
            gs = slice(g * SSD_STATE, (g + 1) * SSD_STATE)
            bm_g = bm[:, gs]
            cm_g = cm[:, gs]
            cm_gb = cm_g.astype(BF16)
            cb2 = _dot_nt(cm_gb, jnp.concatenate([bm_g, bm_g], axis=0).astype(BF16))
            ys = []
            for pp in range(SSD_PAIRS // SSD_GROUPS):
                p = g * (SSD_PAIRS // SSD_GROUPS) + pp
                ps = slice(p * LANE, (p + 1) * LANE)
                xs_p = xs[:, ps]
                xdt = xs_p * dtx[:, ps]
                xst = jnp.concatenate([jnp.where(half, xdt, 0.0), jnp.where(half, 0.0, xdt)], axis=0).astype(BF16)
                y = _dot((cb2 * lmat[:, ps]).astype(BF16), xst)
                xd = xdt * dend[:, ps]
                yis = []
                for s in range(nseq):
                    seq = pc * nseq + s if nseq > 1 else 0
                    if nseq == 1:
                        cm_s, bm_s, xd_s = cm_gb, bm_g, xd
                    else:
                        rs = slice(s * sl, (s + 1) * sl)
                        cm_s, bm_s, xd_s = cm_g[rs].astype(BF16), bm_g[rs], xd[rs]
                    stp = st[seq, p]
                    yis.append(_dot(cm_s, stp.astype(BF16)))
                    st[seq, p] = stp * elast[s * sl:s * sl + 1, ps] + _dot(bm_s.T.astype(BF16), xd_s.astype(BF16))
                yi = yis[0] if nseq == 1 else jnp.concatenate(yis, axis=0)
                ys.append(y + yi * ecum[:, ps] + xs_p * dsk_ref[:, ps])
            gw = SSD_HEADDIM * SSD_HEADS // SSD_GROUPS
            zs = slice(g * gw, (g + 1) * gw)
            yg = jnp.concatenate(ys, axis=1) * _silu(zz[:, zs])
            o_ref[pl.ds(r0, CR), zs] = _rms(yg, nw_ref[:, zs]).astype(BF16)
        return carry

    lax.fori_loop(0, tb * tl // CR, body, 0, unroll=unroll)

    if nseq == 1:
        xpad[0:SUBLANE, :] = xpad[tl:tl + SUBLANE, :]

    @pl.when(li == nl - 1)
    def _():
        if nseq == 1:
            cs_ref[0] = xpad[tl + off:tl + SUBLANE, :]
        for s in range(tb):
            for p in range(SSD_PAIRS):
                so_ref[nprev, s, p] = st[s, p].T


def _ssd_call(proj, cw, cb, alog, dtb, dsk, nw, cs0, s_all, prev, layer, *, b, l, tb, tl, sl, unroll):
    grid, rows, row_map = _seq_grid(b, l, tb, tl)
    assert rows % CR == 0 and CR % sl == 0 and (sl == CR and tb == 1 or sl == tl == SUBLANE)
    consts = _ssd_consts(sl)
    pair = lambda a: a.reshape(a.shape[:2] + (SSD_PAIRS, LANE, SSD_STATE))
    s_pairs = pair(s_all)
    nprev, prev, s_spec, prev_spec, so_spec, so_shape = _state_io(s_pairs, None if prev is None else pair(prev),
                                                                  layer, tb)
    cs_spec = pl.BlockSpec((tb, SSD_CONV - 1, SSD_CONV_DIM), lambda bi, li: (bi, 0, 0))
    params = (cw, cb, alog, dtb, dsk, nw)
    xpad_rows = SUBLANE + tl if sl == CR else tb * 2 * SUBLANE
    o, cs, so = pl.pallas_call(
        functools.partial(_ssd_kernel, tb=tb, tl=tl, sl=sl, nprev=nprev, unroll=unroll),
        grid=grid,
        in_specs=[pl.BlockSpec((rows, MIX_W), row_map(cb_)) for cb_ in
                  (COL_SZ // MIX_W, COL_XS // MIX_W, COL_BC // MIX_W, COL_DT // MIX_W)]
        + [_const_spec(a.shape) for a in params + consts] + [cs_spec, s_spec, prev_spec],
        out_specs=[pl.BlockSpec((rows, MIX_W), row_map(0)), cs_spec, so_spec],
        out_shape=[jax.ShapeDtypeStruct((b * l, MIX_W), BF16), jax.ShapeDtypeStruct(cs0.shape, F32), so_shape],
        scratch_shapes=[pltpu.VMEM((xpad_rows, SSD_CONV_DIM), F32),
                        pltpu.VMEM((tb, SSD_PAIRS, SSD_STATE, LANE), F32)],
        compiler_params=_cparams(("parallel", "arbitrary")),
        name="ssd",
    )(proj, proj, proj, proj, *params, *consts, cs0, s_pairs, prev)
    return o, cs, so.reshape((nprev + 1,) + s_all.shape[1:])


def _hg_levels(sl):
    ms = []
    m = sl // 2
    while m >= 1:
        ms.append(m)
        m //= 2
    return ms


def _hg_consts(sl):
    nseq = CR // sl
    tril = np.kron(np.eye(nseq), np.tril(np.ones((sl, sl))))
    sumall = np.kron(np.eye(nseq), np.ones((sl, sl)))
    mats = []
    masks = []
    i = np.arange(CR)
    for m in _hg_levels(sl):
        ref = (i // (2 * m)) * (2 * m) + m - 1
        if m > 1:
            mats.append(tril - tril[ref])
        same = (i[:, None] // (2 * m)) == (i[None, :] // (2 * m))
        qside = ((i // m) % 2 == 1)[:, None]
        kside = ((i // m) % 2 == 0)[None, :]
        masks.append((same & qside & kside).astype(np.float64))
    return (jnp.asarray(np.concatenate([tril, sumall], axis=0), BF16), jnp.asarray(np.concatenate(mats, axis=0), BF16),
            jnp.asarray(np.stack(masks), F32))


def _hg_kernel(q_ref, f_ref, i_ref, g_ref, lb_ref, nw_ref, mcum_ref, mlev_ref, mask_ref, s0_ref, prev_ref,
               o_ref, so_ref, st, *, tb, tl, sl, nprev, unroll):
    li = pl.program_id(1)
    nl = pl.num_programs(1)

    @pl.when(li == 0)
    def _():
        for s in range(tb):
            for h in range(HG_HEADS):
                st[s, h] = s0_ref[s, h].T
        if nprev:
            so_ref[0:nprev] = prev_ref[...]

    nseq = CR // sl
    levels = _hg_levels(sl)

    odd_row = (lax.broadcasted_iota(jnp.int32, (CR, MIX_W), 0) & 1) == 1

    def body(pc, carry):
        r0 = pl.multiple_of(pc * CR, CR)
        lb = lb_ref[...]
        q_all = _silu(q_ref[pl.ds(r0, CR), :].astype(F32))
        sig = _sigmoid(f_ref[pl.ds(r0, CR), :].astype(F32))
        fdec = lb + (1.0 - lb) * sig
        logf = jnp.log(fdec)
        k_all = (1.0 - lb) * (1.0 - sig)
        v_all = i_ref[pl.ds(r0, CR), :].astype(F32)
        cums = _sel_l(mcum_ref[...], logf)
        gc_all = cums[0:CR]
        glast_all = cums[CR:2 * CR]
        es = []
        if len(levels) > 1:
            dl = _sel2_l(mlev_ref[...], logf)
            es = [jnp.exp(-jnp.abs(dl[n * CR:(n + 1) * CR])) for n in range(len(levels) - 1)]
        es.append(jnp.where(odd_row, fdec, 1.0))
        qg_all = q_all * jnp.exp(gc_all)
        kd_all = k_all * jnp.exp(glast_all - gc_all)
        elast_all = jnp.exp(glast_all)
        for h in range(HG_HEADS):
            hs = slice(h * LANE, (h + 1) * LANE)
            q, k, v = q_all[:, hs], k_all[:, hs], v_all[:, hs]
            att = None
            for n, e_all in enumerate(es):
                e = e_all[:, hs]
                a = _dot_nt((q * e).astype(BF16), (k * e).astype(BF16)) * mask_ref[n]
                att = a if att is None else att + a
            o = _dot(att.astype(BF16), v.astype(BF16)) + jnp.sum(q * k, axis=-1, keepdims=True) * v
            qg, kd, elast = qg_all[:, hs], kd_all[:, hs], elast_all[:, hs]
            ois = []
            for s in range(nseq):
                seq = pc * nseq + s if nseq > 1 else 0
                rs = slice(s * sl, (s + 1) * sl)
                stt = st[seq, h]
                ois.append(_dot_nt(qg[rs].astype(BF16), stt.astype(BF16)))
                st[seq, h] = stt * elast[s * sl:s * sl + 1, :] + _dot(v[rs].T.astype(BF16), kd[rs].astype(BF16))
            o = o + (ois[0] if nseq == 1 else jnp.concatenate(ois, axis=0))
            gate = _sigmoid(g_ref[pl.ds(r0, CR), hs].astype(F32))
            o_ref[pl.ds(r0, CR), hs] = (_rms(o, nw_ref[...]) * gate).astype(BF16)
        return carry

    lax.fori_loop(0, tb * tl // CR, body, 0, unroll=unroll)

    @pl.when(li == nl - 1)
    def _():
        for s in range(tb):
            for h in range(HG_HEADS):
                so_ref[nprev, s, h] = st[s, h].T


def _hg_call(proj, lb, nw, s_all, prev, layer, *, b, l, tb, tl, sl, unroll):
    grid, rows, row_map = _seq_grid(b, l, tb, tl)
    assert rows % CR == 0 and CR % sl == 0 and (sl == CR and tb == 1 or sl == tl)
    consts = _hg_consts(sl)
    cb = COL_HG // MIX_W
    nprev, prev, s_spec, prev_spec, so_spec, so_shape = _state_io(s_all, prev, layer, tb)
    return pl.pallas_call(
        functools.partial(_hg_kernel, tb=tb, tl=tl, sl=sl, nprev=nprev, unroll=unroll),
        grid=grid,
        in_specs=[pl.BlockSpec((rows, MIX_W), row_map(cb + i)) for i in range(4)]
        + [_const_spec(a.shape) for a in (lb, nw) + consts] + [s_spec, prev_spec],
        out_specs=[pl.BlockSpec((rows, MIX_W), row_map(0)), so_spec],
        out_shape=[jax.ShapeDtypeStruct((b * l, MIX_W), BF16), so_shape],
        scratch_shapes=[pltpu.VMEM((tb, HG_HEADS, HG_DK, HG_DK), F32)],
        compiler_params=_cparams(("parallel", "arbitrary")),
        name="hgrn2",
    )(proj, proj, proj, proj, lb, nw, *consts, s_all, prev)


S5_TILES = MIX_W // LANE
S5_TLANES = S5_LANES // S5_TILES


def _s5_perm(tb, tl):
    rows = tb * tl
    p = np.zeros((rows, rows))
    t, s = np.meshgrid(np.arange(tl), np.arange(tb), indexing='ij')
    p[(s * tl + t).ravel(), (t * tb + s).ravel()] = 1.0
    return jnp.asarray(p, BF16), jnp.asarray(p.T, BF16)


def _s5_kernel(su_ref, perm_ref, permt_ref, wbu_ref, are_ref, aim_ref, wc_ref, d_ref, wglu_ref, h0r_ref, h0i_ref,
               o_ref, hr_ref, hi_ref, bur, bui, *, tb, tl):
    @pl.when(pl.program_id(1) == 0)
    def _():
        hr_ref[...] = h0r_ref[...]
        hi_ref[...] = h0i_ref[...]

    rows = tl * tb
    u = _dot(permt_ref[...], su_ref[...].reshape(rows, MIX_W))
    for t in range(S5_TILES):
        bu = _dot(u[:, t * LANE:(t + 1) * LANE].astype(BF16), wbu_ref[t])
        ls = slice(t * S5_TLANES, (t + 1) * S5_TLANES)
        bur[:, ls] = bu[:, :S5_TLANES]
        bui[:, ls] = bu[:, S5_TLANES:]

    for t in range(S5_TILES):
        ls = slice(t * S5_TLANES, (t + 1) * S5_TLANES)
        ar = jnp.broadcast_to(are_ref[:, ls], (tb, S5_TLANES))
        ai = jnp.broadcast_to(aim_ref[:, ls], (tb, S5_TLANES))

        def step(i, carry, ls=ls, ar=ar, ai=ai):
            hr, hi = carry
            r = pl.multiple_of(i * tb, tb)
            nr = (ar * hr - ai * hi) + bur[pl.ds(r, tb), ls]
            ni = (ar * hi + ai * hr) + bui[pl.ds(r, tb), ls]
            bur[pl.ds(r, tb), ls] = nr
            bui[pl.ds(r, tb), ls] = ni
            return nr, ni

        hr, hi = lax.fori_loop(0, tl, step, (hr_ref[:, ls], hi_ref[:, ls]))
        hr_ref[:, ls] = hr
        hi_ref[:, ls] = hi

    ys = []
    for t in range(S5_TILES):
        ls = slice(t * S5_TLANES, (t + 1) * S5_TLANES)
        hcat = jnp.concatenate([bur[:, ls], bui[:, ls]], axis=1).astype(BF16)
        cs = slice(t * LANE, (t + 1) * LANE)
        ys.append(_dot(hcat, wc_ref[t]) + d_ref[:, cs] * u[:, cs])
    z = _gelu_tanh(jnp.concatenate(ys, axis=1))
    out = z * _sigmoid(_dot(z.astype(BF16), wglu_ref[...]))
    o_ref[...] = _dot(perm_ref[...], out.astype(BF16)).astype(BF16).reshape(o_ref.shape)


def _s5_call(proj, wbu, are, aim, wc, d, wglu, h0r, h0i, *, b, l, tb, tl):
    assert l % tl == 0 and b % tb == 0 and tb == SUBLANE
    perm, permt = _s5_perm(tb, tl)
    h_spec = pl.BlockSpec((tb, S5_LANES), lambda bi, li: (bi, 0))
    params = (perm, permt, wbu, are, aim, wc, d, wglu)
    cb = COL_SU // MIX_W
    if tl == l:
        su, su_spec = proj, pl.BlockSpec((tb * tl, MIX_W), lambda bi, li: (bi, cb))
        o_shape, o_spec = (b * l, MIX_W), pl.BlockSpec((tb * tl, MIX_W), lambda bi, li: (bi, 0))
    else:
        su, su_spec = proj.reshape(b, l, -1), pl.BlockSpec((tb, tl, MIX_W), lambda bi, li: (bi, li, cb))
        o_shape, o_spec = (b, l, MIX_W), pl.BlockSpec((tb, tl, MIX_W), lambda bi, li: (bi, li, 0))
    o, hr, hi = pl.pallas_call(
        functools.partial(_s5_kernel, tb=tb, tl=tl),
        grid=(b // tb, l // tl),
        in_specs=[su_spec] + [_const_spec(a.shape) for a in params] + [h_spec, h_spec],
        out_specs=[o_spec, h_spec, h_spec],
        out_shape=[jax.ShapeDtypeStruct(o_shape, BF16), jax.ShapeDtypeStruct(h0r.shape, F32),
                   jax.ShapeDtypeStruct(h0i.shape, F32)],
        scratch_shapes=[pltpu.VMEM((tl * tb, S5_LANES), F32), pltpu.VMEM((tl * tb, S5_LANES), F32)],
        compiler_params=_cparams(("parallel", "arbitrary")),
        name="s5",
    )(su, *params, h0r, h0i)
    return o.reshape(b * l, MIX_W), hr, hi


def _s5_params(p):
    a_re, a_im = p['s5_a_re'], p['s5_a_im']
    dt = jnp.exp(p['s5_log_dt'])[:, None]
    mag = jnp.exp(dt * a_re)
    ab_re, ab_im = mag * jnp.cos(dt * a_im), mag * jnp.sin(dt * a_im)
    den = a_re * a_re + a_im * a_im
    n_re, n_im = ab_re - 1.0, ab_im
    f_re = (n_re * a_re + n_im * a_im) / den
    f_im = (n_im * a_re - n_re * a_im) / den
    b_re, b_im = p['s5_b_re'], p['s5_b_im']
    bb_re = f_re[..., None] * b_re - f_im[..., None] * b_im
    bb_im = f_re[..., None] * b_im + f_im[..., None] * b_re
    gpt = S5_GROUPS // S5_TILES
    eye = jnp.eye(gpt, dtype=F32)

    def pack_b(bb):
        bt = bb.reshape(S5_TILES, gpt, S5_STATE, S5_GROUP)
        w = jnp.einsum('tgpm,gh->tgmhp', bt, eye)
        return w.reshape(S5_TILES, gpt * S5_GROUP, gpt * S5_STATE)

    def pack_c(cc):
        ct = cc.reshape(S5_TILES, gpt, S5_GROUP, S5_STATE)
        w = jnp.einsum('tgmp,gh->tgphm', ct, eye)
        return w.reshape(S5_TILES, gpt * S5_STATE, gpt * S5_GROUP)

    wbu = jnp.concatenate([pack_b(bb_re), pack_b(bb_im)], axis=2).astype(BF16)
    wc = jnp.concatenate([pack_c(p['s5_c_re']), -pack_c(p['s5_c_im'])], axis=1).astype(BF16)
    return (wbu, ab_re.reshape(1, S5_LANES), ab_im.reshape(1, S5_LANES), wc, p['s5_d'].reshape(1, MIX_W))


def _merge_kernel(x_ref, gl_ref, o0, o1, o2, o3, wb_ref, wo_ref, g_ref, out_ref):
    m = None
    for n, o in enumerate((o0, o1, o2, o3)):
        t = _dot(o[...], wb_ref[n])
        t = _sigmoid(gl_ref[:, n * D_MODEL:(n + 1) * D_MODEL].astype(F32)) * t
        m = t if m is None else m + t
    mo = _dot(m.astype(BF16), wo_ref[...])
    out_ref[...] = x_ref[...] + _rms(mo, g_ref[...])


def _merge_call(x2, proj, branches, wb, wo, g, *, tm):
    t = x2.shape[0]
    row = lambda i: (i, 0)
    return pl.pallas_call(
        _merge_kernel,
        grid=(t // tm,),
        in_specs=[pl.BlockSpec((tm, D_MODEL), row), pl.BlockSpec((tm, GATE_COLS), row)]
        + [pl.BlockSpec((tm, MIX_W), row)] * N_BRANCH
        + [_const_spec(wb.shape), _const_spec(wo.shape), _const_spec(g.shape)],
        out_specs=pl.BlockSpec((tm, D_MODEL), row),
        out_shape=jax.ShapeDtypeStruct((t, D_MODEL), F32),
        compiler_params=_cparams(("parallel",)),
        name="merge",
    )(x2, proj, *branches, wb, wo, g)


def _ffn_kernel(x_ref, g1_ref, w1_ref, w2_ref, g2_ref, o_ref, h_ref, acc_ref):
    j = pl.program_id(1)

    @pl.when(j == 0)
    def _():
        h_ref[...] = _rms(x_ref[...], g1_ref[...]).astype(BF16)
        acc_ref[...] = jnp.zeros_like(acc_ref)

    u = jnp.maximum(_dot(h_ref[...], w1_ref[...]), 0.0)
    acc_ref[...] += _dot((u * u).astype(BF16), w2_ref[...])

    @pl.when(j == pl.num_programs(1) - 1)
    def _():
        o_ref[...] = x_ref[...] + _rms(acc_ref[...], g2_ref[...])


def _ffn_call(x2, g1, w1, w2, g2, *, tm, tf):
    t = x2.shape[0]
    return pl.pallas_call(
        _ffn_kernel,
        grid=(t // tm, D_FF // tf),
        in_specs=[
            pl.BlockSpec((tm, D_MODEL), lambda i, j: (i, 0)),
            pl.BlockSpec((1, D_MODEL), lambda i, j: (0, 0)),
            pl.BlockSpec((D_MODEL, tf), lambda i, j: (0, j)),
            pl.BlockSpec((tf, D_MODEL), lambda i, j: (j, 0)),
            pl.BlockSpec((1, D_MODEL), lambda i, j: (0, 0)),
        ],
        out_specs=pl.BlockSpec((tm, D_MODEL), lambda i, j: (i, 0)),
        out_shape=jax.ShapeDtypeStruct((t, D_MODEL), F32),
        scratch_shapes=[pltpu.VMEM((tm, D_MODEL), BF16), pltpu.VMEM((tm, D_MODEL), F32)],
        compiler_params=_cparams(("parallel", "arbitrary")),
        name="ffn",
    )(x2, g1, w1, w2, g2)


def _prep_layer(p, lb):
    w_in = p['w_in']
    o_sdt = 4 * MIX_W + MIX_W + SSD_CONV_DIM
    o_hq = o_sdt + SSD_HEADS
    o_su = o_hq + 4 * MIX_W
    o_gl = o_su + MIX_W
    w_main = jnp.concatenate([
        w_in[:, o_gl:o_gl + GATE_COLS], w_in[:, 0:o_sdt], w_in[:, o_hq:o_su],
        jnp.repeat(w_in[:, o_sdt:o_hq], SSD_HEADDIM, axis=1), w_in[:, o_su:o_gl]], axis=1).astype(BF16)
    assert w_main.shape[1] == PROJ_COLS

    def head_lanes(v):
        return jnp.repeat(v.astype(F32), SSD_HEADDIM).reshape(1, MIX_W)

    q = dict(
        w_main=w_main,
        g_pre_mix=p['g_pre_mix'].reshape(1, D_MODEL),
        g_post_mix=p['g_post_mix'].reshape(1, D_MODEL),
        g_pre_ffn=p['g_pre_ffn'].reshape(1, D_MODEL),
        g_post_ffn=p['g_post_ffn'].reshape(1, D_MODEL),
        ret_gn=p['ret_gn'],
        conv_w=p['ssd_conv_w'],
        conv_b=p['ssd_conv_b'].reshape(1, SSD_CONV_DIM),
        a_log=head_lanes(p['ssd_a_log']),
        dt_bias=head_lanes(p['ssd_dt_bias']),
        d_skip=head_lanes(p['ssd_d']),
        ssd_norm=p['ssd_norm'].reshape(1, MIX_W),
        hg_lb=lb.reshape(1, MIX_W),
        hg_norm=p['hg_norm'].reshape(1, HG_DK),
        s5=_s5_params(p),
        w_glu=p['s5_w_glu'].astype(BF16),
        w_branch=p['w_branch'].astype(BF16),
        w_out=p['w_out'].astype(BF16),
        w_ff1=p['w_ff1'].astype(BF16),
        w_ff2=p['w_ff2'].astype(BF16),
    )
    return q


def _layer(x3, states, acc, layer, q, pos, cfg):
    b, l, _ = x3.shape
    x2 = x3.reshape(b * l, D_MODEL)
    tb, tl, c = cfg['tb'], cfg['tl'], cfg['c']
    proj = _proj_call(x2, q['g_pre_mix'], q['w_main'], tm=cfg['tm_proj'], tn=cfg['tn'])
    st = {n: states[n][layer] for n in ('conv', 's5_re', 's5_im')}
    o_ret, s_ret = _ret_call(proj, pos, q['ret_gn'], states['ret'], acc['ret'], layer, b=b, l=l, tb=tb, tl=tl, sl=c,
                             unroll=cfg['u_ret'])
    o_ssd, conv_new, s_ssd = _ssd_call(proj, q['conv_w'], q['conv_b'], q['a_log'], q['dt_bias'], q['d_skip'],
                                       q['ssd_norm'], st['conv'], states['ssd'], acc['ssd'], layer,
                                       b=b, l=l, tb=tb, tl=tl, sl=c, unroll=cfg['u_ssd'])
    o_hg, s_hg = _hg_call(proj, q['hg_lb'], q['hg_norm'], states['hgrn'], acc['hgrn'], layer, b=b, l=l, tb=tb, tl=tl,
                          sl=c, unroll=cfg['u_hg'])
    wbu, are, aim, wc, d5 = q['s5']
    o_s5, s5_re, s5_im = _s5_call(proj, wbu, are, aim, wc, d5, q['w_glu'],
                                  st['s5_re'].reshape(b, S5_LANES), st['s5_im'].reshape(b, S5_LANES),
                                  b=b, l=l, tb=SUBLANE, tl=cfg['s5_tl'])
    x2 = _merge_call(x2, proj, (o_ret, o_ssd, o_hg, o_s5), q['w_branch'], q['w_out'], q['g_post_mix'], tm=cfg['tm_merge'])
    x2 = _ffn_call(x2, q['g_pre_ffn'], q['w_ff1'], q['w_ff2'], q['g_post_ffn'], tm=cfg['tm'], tf=cfg['tf'])
    def push(name, new):
        return new[None] if acc[name] is None else jnp.concatenate([acc[name], new[None]], axis=0)

    new = {'ret': s_ret, 'ssd': s_ssd, 'hgrn': s_hg, 'conv': push('conv', conv_new),
           's5_re': push('s5_re', s5_re.reshape(b, S5_GROUPS, S5_STATE)),
           's5_im': push('s5_im', s5_im.reshape(b, S5_GROUPS, S5_STATE))}
    return x2.reshape(b, l, D_MODEL), new


_NAMES = ('ret', 'ssd', 'conv', 'hgrn', 's5_re', 's5_im')


def _trunk(x, states, layers, pos, cfg):
    acc = {n: None for n in _NAMES}
    for layer, q in enumerate(layers):
        x, acc = _layer(x, states, acc, layer, q, pos, cfg)
    return x, acc


def _group_cfg(b, l):
    if l % CHUNK == 0:
        return dict(tb=1, tl=min(l, 512), c=CHUNK, tm=1024, tm_proj=2048, tn=1536, tm_merge=512, tf=1024,
                    s5_tl=min(l, 64), u_ret=4, u_ssd=4, u_hg=4)
    return dict(tb=SUBLANE, tl=l, c=l, tm=min(b * l, 1024), tm_proj=min(b * l, 1024), tn=1536,
                tm_merge=min(b * l, 512), tf=1024, s5_tl=l, u_ret=1, u_ssd=1, u_hg=1)


def kernel(x_prompt, x_sample, state_ret, state_ssd, state_conv, state_hgrn, state_s5_re, state_s5_im, g_pre_mix, g_post_mix, g_pre_ffn, g_post_ffn, w_in, ret_gn, ssd_conv_w, ssd_conv_b, ssd_a_log, ssd_dt_bias, ssd_d, ssd_norm, hg_lb_logits, hg_norm, s5_a_re, s5_a_im, s5_b_re, s5_b_im, s5_c_re, s5_c_im, s5_d, s5_log_dt, s5_w_glu, w_branch, w_out, w_ff1, w_ff2):
    params = {
        'g_pre_mix': g_pre_mix, 'g_post_mix': g_post_mix, 'g_pre_ffn': g_pre_ffn, 'g_post_ffn': g_post_ffn,
        'w_in': w_in, 'ret_gn': ret_gn, 'ssd_conv_w': ssd_conv_w, 'ssd_conv_b': ssd_conv_b,
        'ssd_a_log': ssd_a_log, 'ssd_dt_bias': ssd_dt_bias, 'ssd_d': ssd_d, 'ssd_norm': ssd_norm,
        'hg_norm': hg_norm, 's5_a_re': s5_a_re, 's5_a_im': s5_a_im, 's5_b_re': s5_b_re, 's5_b_im': s5_b_im,
        's5_c_re': s5_c_re, 's5_c_im': s5_c_im, 's5_d': s5_d, 's5_log_dt': s5_log_dt, 's5_w_glu': s5_w_glu,
        'w_branch': w_branch, 'w_out': w_out, 'w_ff1': w_ff1, 'w_ff2': w_ff2,
    }
    depth = w_in.shape[0]
    w = jax.nn.softmax(hg_lb_logits.astype(F32), axis=0)
    lbs = jnp.cumsum(w, axis=0) - w[0]
    layers = [_prep_layer({k: v[i] for k, v in params.items()}, lbs[i]) for i in range(depth)]

    bp, lp, _ = x_prompt.shape
    bs, ls, _ = x_sample.shape
    zero_states = {
        'ret': jnp.zeros((depth, bp) + state_ret.shape[2:], F32),
        'ssd': jnp.zeros((depth, bp) + state_ssd.shape[2:], F32),
        'conv': jnp.zeros((depth, bp) + state_conv.shape[2:], F32),
        'hgrn': jnp.zeros((depth, bp) + state_hgrn.shape[2:], F32),
        's5_re': jnp.zeros((depth, bp) + state_s5_re.shape[2:], F32),
        's5_im': jnp.zeros((depth, bp) + state_s5_im.shape[2:], F32),
    }
    sample_states = {'ret': state_ret, 'ssd': state_ssd, 'conv': state_conv, 'hgrn': state_hgrn,
                     's5_re': state_s5_re, 's5_im': state_s5_im}
    pos_p = np.arange(lp)
    pos_s = PAST_LEN + np.arange(ls)
    y_p, new_p = _trunk(x_prompt, zero_states, layers, pos_p, _group_cfg(bp, lp))
    y_s, new_s = _trunk(x_sample, sample_states, layers, pos_s, _group_cfg(bs, ls))
    return (y_p, y_s,
            new_p['ret'], new_s['ret'], new_p['ssd'], new_s['ssd'], new_p['conv'], new_s['conv'],
            new_p['hgrn'], new_s['hgrn'], new_p['s5_re'], new_s['s5_re'], new_p['s5_im'], new_s['s5_im'])
```

```python
import functools
import math

import jax
import jax.numpy as jnp
import numpy as np
from jax import lax
from jax.experimental import pallas as pl
from jax.experimental.pallas import tpu as pltpu

F32 = jnp.float32
BF16 = jnp.bfloat16

D_MODEL = 1024
DEPTH = 2
PAST_LEN = 16384
MIX_W = D_MODEL // 2
N_BRANCH = 4
RET_HEADS = 4
RET_DK = MIX_W // RET_HEADS
ROPE_BASE = 10000.0
SSD_HEADDIM = 64
SSD_HEADS = MIX_W // SSD_HEADDIM
SSD_GROUPS = 2
SSD_STATE = 128
SSD_CONV = 4
SSD_CONV_DIM = MIX_W + 2 * SSD_GROUPS * SSD_STATE
SSD_PAIRS = SSD_HEADS // 2
HG_HEADS = 4
HG_DK = MIX_W // HG_HEADS
S5_GROUP = 16
S5_GROUPS = MIX_W // S5_GROUP
S5_STATE = 64
S5_LANES = S5_GROUPS * S5_STATE
D_FF = 4 * D_MODEL
GATE_COLS = N_BRANCH * D_MODEL
CHUNK = 64
CR = 64
EPS = 1e-6

LANE = 128
SUBLANE = 8
VMEM_LIMIT = 48 * 1024 * 1024

COL_GL = 0
COL_RET = GATE_COLS
COL_SZ = COL_RET + 4 * MIX_W
COL_XS = COL_SZ + MIX_W
COL_BC = COL_XS + MIX_W
COL_HG = COL_BC + MIX_W
COL_DT = COL_HG + 4 * MIX_W
COL_SU = COL_DT + MIX_W
PROJ_COLS = COL_SU + MIX_W


def _sigmoid(x):
    return 1.0 / (1.0 + jnp.exp(-x))


def _silu(x):
    return x * _sigmoid(x)


def _softplus(x):
    return jnp.maximum(x, 0.0) + jnp.log1p(jnp.exp(-jnp.abs(x)))


def _gelu_tanh(x):
    c = math.sqrt(2.0 / math.pi)
    return 0.5 * x * (1.0 + jnp.tanh(c * (x + 0.044715 * (x * x * x))))


def _rms(x, g):
    ms = jnp.mean(x * x, axis=-1, keepdims=True)
    return x * lax.rsqrt(ms + EPS) * g


def _dot(a, b):
    return jnp.dot(a, b, preferred_element_type=F32)


def _dot_nt(a, b):
    return lax.dot_general(a, b, (((1,), (1,)), ((), ())), preferred_element_type=F32)


def _split3(a):
    hi = a.astype(BF16)
    r1 = a - hi.astype(F32)
    mid = r1.astype(BF16)
    lo = (r1 - mid.astype(F32)).astype(BF16)
    return hi, mid, lo


def _sel_l(m, a):
    hi, mid, lo = _split3(a)
    return (_dot(m, lo) + _dot(m, mid)) + _dot(m, hi)


def _sel2_l(m, a):
    hi = a.astype(BF16)
    mid = (a - hi.astype(F32)).astype(BF16)
    return _dot(m, mid) + _dot(m, hi)


def _sel_r(a, m):
    hi, mid, lo = _split3(a)
    return (_dot(lo, m) + _dot(mid, m)) + _dot(hi, m)


def _cparams(sem):
    return pltpu.CompilerParams(dimension_semantics=sem, vmem_limit_bytes=VMEM_LIMIT)


def _const_spec(shape):
    nd = len(shape)
    return pl.BlockSpec(shape, lambda *_: (0,) * nd)


def _state_io(s_all, prev, layer, tb):
    dims = tuple(s_all.shape[2:])
    zeros = (0,) * len(dims)
    in_spec = pl.BlockSpec((None, tb) + dims, lambda bi, li: (layer, bi) + zeros)
    if prev is None:
        nprev, prev, prev_spec = 0, s_all, in_spec
    else:
        nprev = prev.shape[0]
        prev_spec = pl.BlockSpec((nprev, tb) + dims, lambda bi, li: (0, bi) + zeros)
    out_spec = pl.BlockSpec((nprev + 1, tb) + dims, lambda bi, li: (0, bi) + zeros)
    out_shape = jax.ShapeDtypeStruct((nprev + 1, s_all.shape[1]) + dims, F32)
    return nprev, prev, in_spec, prev_spec, out_spec, out_shape


def _proj_kernel(x_ref, g_ref, w_ref, o_ref, h_ref):
    @pl.when(pl.program_id(1) == 0)
    def _():
        h_ref[...] = _rms(x_ref[...], g_ref[...]).astype(BF16)

    o_ref[...] = _dot(h_ref[...], w_ref[...]).astype(BF16)


def _proj_call(x2, g, w, *, tm, tn):
    t, d = x2.shape
    n = w.shape[1]
    return pl.pallas_call(
        _proj_kernel,
        grid=(t // tm, n // tn),
        in_specs=[
            pl.BlockSpec((tm, d), lambda i, j: (i, 0)),
            pl.BlockSpec((1, d), lambda i, j: (0, 0)),
            pl.BlockSpec((d, tn), lambda i, j: (0, j)),
        ],
        out_specs=pl.BlockSpec((tm, tn), lambda i, j: (i, j)),
        out_shape=jax.ShapeDtypeStruct((t, n), BF16),
        scratch_shapes=[pltpu.VMEM((tm, d), BF16)],
        compiler_params=_cparams(("parallel", "arbitrary")),
        name="in_proj",
    )(x2, g, w)


def _ret_consts(sl):
    nseq = CR // sl
    h = np.arange(RET_HEADS, dtype=np.float64)
    log_g = np.log(1.0 - 2.0 ** (-5.0 - h))
    idx = np.arange(sl, dtype=np.float64)
    diff = idx[:, None] - idx[None, :]
    dmat = np.where(diff >= 0, np.exp(log_g[:, None, None] * np.maximum(diff, 0.0)), 0.0)
    dmat = np.stack([np.kron(np.eye(nseq), d) for d in dmat])
    q_dec = np.tile(np.exp(log_g[:, None] * (idx[None, :] + 1.0)), (1, nseq))
    k_dec = np.tile(np.exp(log_g[:, None] * (sl - 1.0 - idx[None, :])), (1, nseq))
    chunk_dec = np.exp(log_g * sl)
    qd = np.broadcast_to(q_dec[:, :, None], (RET_HEADS, CR, LANE))
    kd = np.broadcast_to(k_dec[:, :, None], (RET_HEADS, CR, LANE))
    return (jnp.asarray(dmat, F32), jnp.asarray(qd, F32), jnp.asarray(kd, F32),
            tuple(float(v) for v in chunk_dec))


def _rope_tables(pos):
    half = RET_DK // 2
    inv = ROPE_BASE ** (-np.arange(half, dtype=np.float64) / half)
    ang = np.asarray(pos, np.float64)[:, None] * inv[None, :]
    cos, sin = np.cos(ang), np.sin(ang)
    return (jnp.asarray(np.concatenate([cos, cos], axis=1), F32),
            jnp.asarray(np.concatenate([-sin, sin], axis=1), F32))


def _ret_kernel(q_ref, k_ref, v_ref, g_ref, cos_ref, sin_ref, dmat_ref, qd_ref, kd_ref, gn_ref, s0_ref, prev_ref,
                o_ref, so_ref, *, tb, tl, sl, cdec, nprev, unroll):
    st_ref = so_ref.at[nprev]

    @pl.when(pl.program_id(1) == 0)
    def _():
        st_ref[...] = s0_ref[...]
        if nprev:
            so_ref[0:nprev] = prev_ref[...]

    nseq = CR // sl
    scale = RET_DK ** -0.5

    def body(pc, carry):
        r0 = pl.multiple_of(pc * CR, CR)
        t0 = r0 if nseq == 1 else 0
        cos = cos_ref[pl.ds(t0, CR), :]
        sin = sin_ref[pl.ds(t0, CR), :]
        heads = [slice(h * LANE, (h + 1) * LANE) for h in range(RET_HEADS)]
        qrs, krs = [], []
        for hs in heads:
            q = q_ref[pl.ds(r0, CR), hs].astype(F32)
            k = k_ref[pl.ds(r0, CR), hs].astype(F32)
            qrs.append(q * cos + pltpu.roll(q, LANE // 2, axis=1) * sin)
            krs.append((k * cos + pltpu.roll(k, LANE // 2, axis=1) * sin) * scale)
        vs = [v_ref[pl.ds(r0, CR), hs] for hs in heads]
        atts = [(_dot_nt(qrs[h].astype(BF16), krs[h].astype(BF16)) * dmat_ref[h]).astype(BF16)
                for h in range(RET_HEADS)]
        outs = []
        for h in range(RET_HEADS):
            o = _dot(atts[h], vs[h])
            qd = qrs[h] * qd_ref[h]
            kd = krs[h] * kd_ref[h]
            ois = []
            for s in range(nseq):
                seq = pc * nseq + s if nseq > 1 else 0
                rs = slice(s * sl, (s + 1) * sl)
                st = st_ref[seq, h]
                ois.append(_dot(qd[rs].astype(BF16), st.astype(BF16)))
                v_s = vs[h] if nseq == 1 else vs[h].astype(F32)[rs].astype(BF16)
                st_ref[seq, h] = st * cdec[h] + _dot(kd[rs].T.astype(BF16), v_s)
            outs.append(o + (ois[0] if nseq == 1 else jnp.concatenate(ois, axis=0)))
        for h, hs in enumerate(heads):
            o = outs[h]
            mu = jnp.mean(o, axis=-1, keepdims=True)
            oc = o - mu
            var = jnp.mean(oc * oc, axis=-1, keepdims=True)
            on = oc * lax.rsqrt(var + EPS) * gn_ref[h:h + 1, :]
            o_ref[pl.ds(r0, CR), hs] = (on * _silu(g_ref[pl.ds(r0, CR), hs].astype(F32))).astype(BF16)
        return carry

    lax.fori_loop(0, tb * tl // CR, body, 0, unroll=unroll)


def _seq_grid(b, l, tb, tl):
    assert b % tb == 0 and l % tl == 0 and (tb == 1 or tl == l)
    nl = l // tl
    rows = tb * tl

    def row_map(col_blk):
        return lambda bi, li: (bi * nl + li, col_blk)

    return (b // tb, nl), rows, row_map


def _ret_call(proj, pos, gn, s_all, prev, layer, *, b, l, tb, tl, sl, unroll):
    grid, rows, row_map = _seq_grid(b, l, tb, tl)
    assert rows % CR == 0 and CR % sl == 0 and (sl == CR and tb == 1 or sl == tl)
    dmat, qd, kd, cdec = _ret_consts(sl)
    cos, sin = _rope_tables(pos)
    if sl < CR:
        cos, sin = jnp.tile(cos, (CR // sl, 1)), jnp.tile(sin, (CR // sl, 1))
        tab_spec = _const_spec(cos.shape)
    else:
        tab_spec = pl.BlockSpec((tl, LANE), lambda bi, li: (li, 0))
    cb = COL_RET // MIX_W
    nprev, prev, s_spec, prev_spec, so_spec, so_shape = _state_io(s_all, prev, layer, tb)
    return pl.pallas_call(
        functools.partial(_ret_kernel, tb=tb, tl=tl, sl=sl, cdec=cdec, nprev=nprev, unroll=unroll),
        grid=grid,
        in_specs=[pl.BlockSpec((rows, MIX_W), row_map(cb + i)) for i in range(4)] + [
            tab_spec, tab_spec,
            _const_spec(dmat.shape), _const_spec(qd.shape), _const_spec(kd.shape), _const_spec(gn.shape),
            s_spec, prev_spec,
        ],
        out_specs=[pl.BlockSpec((rows, MIX_W), row_map(0)), so_spec],
        out_shape=[jax.ShapeDtypeStruct((b * l, MIX_W), BF16), so_shape],
        compiler_params=_cparams(("parallel", "arbitrary")),
        name="retention",
    )(proj, proj, proj, proj, cos, sin, dmat, qd, kd, gn, s_all, prev)


def _ssd_consts(sl):
    nseq = CR // sl
    tril = np.kron(np.eye(nseq), np.tril(np.ones((sl, sl))))
    sumall = np.kron(np.eye(nseq), np.ones((sl, sl)))
    ones = np.ones((CR, CR))
    i = np.arange(CR)[:, None]
    j = np.arange(MIX_W)[None, :] % SSD_HEADDIM
    eye = (i == j).astype(np.float64)
    mask = ((i // sl == j // sl) & (i >= j)).astype(np.float64)
    return (jnp.asarray(np.concatenate([tril, sumall], axis=0), BF16), jnp.asarray(ones, BF16),
            jnp.asarray(eye, F32), jnp.asarray(mask, F32))


def _ssd_kernel(z_ref, xs_ref, bc_ref, dtx_ref, cw_ref, cb_ref, alog_ref, dtb_ref, dsk_ref, nw_ref,
                mcum_ref, ones_ref, eye_ref, mask_ref, cs0_ref, s0_ref, prev_ref,
                o_ref, cs_ref, so_ref, xpad, st, *, tb, tl, sl, nprev, unroll):
    li = pl.program_id(1)
    nl = pl.num_programs(1)
    nseq = CR // sl
    kc = SSD_CONV - 1
    off = SUBLANE - kc
    grp = 2 * SUBLANE

    @pl.when(li == 0)
    def _():
        for s in range(tb):
            for p in range(SSD_PAIRS):
                st[s, p] = s0_ref[s, p].T
        if nprev:
            so_ref[0:nprev] = prev_ref[...]
        if nseq == 1:
            xpad[0:SUBLANE, :] = jnp.zeros((SUBLANE, SSD_CONV_DIM), F32)
            xpad[off:SUBLANE, :] = cs0_ref[0]

    if nseq == 1:
        xpad[SUBLANE:SUBLANE + tl, 0:MIX_W] = xs_ref[...].astype(F32)
        xpad[SUBLANE:SUBLANE + tl, MIX_W:2 * MIX_W] = bc_ref[...].astype(F32)
    else:
        xs_new = xs_ref[...].astype(F32)
        bc_new = bc_ref[...].astype(F32)
        for s in range(tb):
            xpad[s * grp:s * grp + SUBLANE, :] = jnp.zeros((SUBLANE, SSD_CONV_DIM), F32)
            xpad[s * grp + off:s * grp + SUBLANE, :] = cs0_ref[s]
            xpad[s * grp + SUBLANE:(s + 1) * grp, 0:MIX_W] = xs_new[s * sl:(s + 1) * sl, :]
            xpad[s * grp + SUBLANE:(s + 1) * grp, MIX_W:2 * MIX_W] = bc_new[s * sl:(s + 1) * sl, :]
            cs_ref[s] = xpad[(s + 1) * grp - kc:(s + 1) * grp, :]

    a_neg = -jnp.exp(alog_ref[...])
    half = lax.broadcasted_iota(jnp.int32, (CR, LANE), 1) < SSD_HEADDIM

    def body(pc, carry):
        r0 = pl.multiple_of(pc * CR, CR)
        if nseq == 1:
            win = xpad[pl.ds(r0, CR + SUBLANE), :]
            take = lambda a: a[SUBLANE:, :]
        else:
            win = xpad[pl.ds(pl.multiple_of(pc * nseq * grp, nseq * grp), nseq * grp), :]
            take = lambda a: a.reshape(nseq, grp, SSD_CONV_DIM)[:, SUBLANE:, :].reshape(CR, SSD_CONV_DIM)
        acc = cb_ref[...] + take(pltpu.roll(win, kc, axis=0)) * cw_ref[0:1, :]
        for j in range(1, kc):
            acc = acc + take(pltpu.roll(win, kc - j, axis=0)) * cw_ref[j:j + 1, :]
        xc = _silu(acc + take(win) * cw_ref[kc:kc + 1, :])
        xs = xc[:, 0:MIX_W]
        bm = xc[:, MIX_W:MIX_W + SSD_GROUPS * SSD_STATE]
        cm = xc[:, MIX_W + SSD_GROUPS * SSD_STATE:2 * MIX_W]

        dtx = _softplus(dtx_ref[pl.ds(r0, CR), :].astype(F32) + dtb_ref[...])
        both = _sel_l(mcum_ref[...], dtx * a_neg)
        cum, clast = both[0:CR], both[CR:2 * CR]
        ecum = jnp.exp(cum)
        dend = jnp.exp(clast - cum)
        elast = jnp.exp(clast)
        rr = _sel_l(ones_ref[...], cum * eye_ref[...])
        msk = mask_ref[...]
        lmat = jnp.exp(jnp.where(msk > 0, cum - rr, 0.0)) * msk
        zz = z_ref[pl.ds(r0, CR), :].astype(F32)
        npp = SSD_PAIRS // SSD_GROUPS
        bm_gs = [bm[:, g * SSD_STATE:(g + 1) * SSD_STATE] for g in range(SSD_GROUPS)]
        cm_gs = [cm[:, g * SSD_STATE:(g + 1) * SSD_STATE] for g in range(SSD_GROUPS)]
        cm_gbs = [c_.astype(BF16) for c_ in cm_gs]
        cb2s = [_dot_nt(cm_gbs[g], jnp.concatenate([bm_gs[g], bm_gs[g]], axis=0).astype(BF16))
                for g in range(SSD_GROUPS)]
        pss = [slice(p * LANE, (p + 1) * LANE) for p in range(SSD_PAIRS)]
        xdts = [xs[:, ps] * dtx[:, ps] for ps in pss]
        xsts = [jnp.concatenate([jnp.where(half, x_, 0.0), jnp.where(half, 0.0, x_)], axis=0).astype(BF16)
                for x_ in xdts]
        ys = [_dot((cb2s[p // npp] * lmat[:, pss[p]]).astype(BF16), xsts[p]) for p in range(SSD_PAIRS)]
        xds = [xdts[p] * dend[:, pss[p]] for p in range(SSD_PAIRS)]
        yis = []
        for p in range(SSD_PAIRS):
            g = p // npp
            parts = []
            for s in range(nseq):
                seq = pc * nseq + s if nseq > 1 else 0
                if nseq == 1:
                    cm_s, bm_s, xd_s = cm_gbs[g], bm_gs[g], xds[p]
                else:
                    rs = slice(s * sl, (s + 1) * sl)
                    cm_s, bm_s, xd_s = cm_gs[g][rs].astype(BF16), bm_gs[g][rs], xds[p][rs]
                stp = st[seq, p]
                parts.append(_dot(cm_s, stp.astype(BF16)))
                st[seq, p] = stp * elast[s * sl:s * sl + 1, pss[p]] + _dot(bm_s.T.astype(BF16), xd_s.astype(BF16))
            yis.append(parts[0] if nseq == 1 else jnp.concatenate(parts, axis=0))
        gw = SSD_HEADDIM * SSD_HEADS // SSD_GROUPS
        for g in range(SSD_GROUPS):
            yg = jnp.concatenate([ys[p] + yis[p] * ecum[:, pss[p]] + xs[:, pss[p]] * dsk_ref[:, pss[p]]
                                  for p in range(g * npp, (g + 1) * npp)], axis=1)
            zs = slice(g * gw, (g + 1) * gw)
            o_ref[pl.ds(r0, CR), zs] = _rms(yg * _silu(zz[:, zs]), nw_ref[:, zs]).astype(BF16)
        return carry

    lax.fori_loop(0, tb * tl // CR, body, 0, unroll=unroll)

    if nseq == 1:
        xpad[0:SUBLANE, :] = xpad[tl:tl + SUBLANE, :]

    @pl.when(li == nl - 1)
    def _():
        if nseq == 1:
            cs_ref[0] = xpad[tl + off:tl + SUBLANE, :]
        for s in range(tb):
            for p in range(SSD_PAIRS):
                so_ref[nprev, s, p] = st[s, p].T


def _ssd_call(proj, cw, cb, alog, dtb, dsk, nw, cs0, s_all, prev, layer, *, b, l, tb, tl, sl, unroll):
    grid, rows, row_map = _seq_grid(b, l, tb, tl)
    assert rows % CR == 0 and CR % sl == 0 and (sl == CR and tb == 1 or sl == tl == SUBLANE)
    consts = _ssd_consts(sl)
    pair = lambda a: a.reshape(a.shape[:2] + (SSD_PAIRS, LANE, SSD_STATE))
    s_pairs = pair(s_all)
    nprev, prev, s_spec, prev_spec, so_spec, so_shape = _state_io(s_pairs, None if prev is None else pair(prev),
                                                                  layer, tb)
    cs_spec = pl.BlockSpec((tb, SSD_CONV - 1, SSD_CONV_DIM), lambda bi, li: (bi, 0, 0))
    params = (cw, cb, alog, dtb, dsk, nw)
    xpad_rows = SUBLANE + tl if sl == CR else tb * 2 * SUBLANE
    o, cs, so = pl.pallas_call(
        functools.partial(_ssd_kernel, tb=tb, tl=tl, sl=sl, nprev=nprev, unroll=unroll),
        grid=grid,
        in_specs=[pl.BlockSpec((rows, MIX_W), row_map(cb_)) for cb_ in
                  (COL_SZ // MIX_W, COL_XS // MIX_W, COL_BC // MIX_W, COL_DT // MIX_W)]
        + [_const_spec(a.shape) for a in params + consts] + [cs_spec, s_spec, prev_spec],
        out_specs=[pl.BlockSpec((rows, MIX_W), row_map(0)), cs_spec, so_spec],
        out_shape=[jax.ShapeDtypeStruct((b * l, MIX_W), BF16), jax.ShapeDtypeStruct(cs0.shape, F32), so_shape],
        scratch_shapes=[pltpu.VMEM((xpad_rows, SSD_CONV_DIM), F32),
                        pltpu.VMEM((tb, SSD_PAIRS, SSD_STATE, LANE), F32)],
        compiler_params=_cparams(("parallel", "arbitrary")),
        name="ssd",
    )(proj, proj, proj, proj, *params, *consts, cs0, s_pairs, prev)
    return o, cs, so.reshape((nprev + 1,) + s_all.shape[1:])


def _hg_levels(sl):
    ms = []
    m = sl // 2
    while m >= 1:
        ms.append(m)
        m //= 2
    return ms


def _hg_consts(sl):
    nseq = CR // sl
    tril = np.kron(np.eye(nseq), np.tril(np.ones((sl, sl))))
    sumall = np.kron(np.eye(nseq), np.ones((sl, sl)))
    mats = []
    masks = []
    i = np.arange(CR)
    for m in _hg_levels(sl):
        ref = (i // (2 * m)) * (2 * m) + m - 1
        if m > 1:
            mats.append(tril - tril[ref])
        same = (i[:, None] // (2 * m)) == (i[None, :] // (2 * m))
        qside = ((i // m) % 2 == 1)[:, None]
        kside = ((i // m) % 2 == 0)[None, :]
        masks.append((same & qside & kside).astype(np.float64))
    return (jnp.asarray(np.concatenate([tril, sumall], axis=0), BF16), jnp.asarray(np.concatenate(mats, axis=0), BF16),
            jnp.asarray(np.stack(masks), F32))


def _hg_kernel(q_ref, f_ref, i_ref, g_ref, lb_ref, nw_ref, mcum_ref, mlev_ref, mask_ref, s0_ref, prev_ref,
               o_ref, so_ref, st, qe_s, ke_s, qg_s, kd_s, el_s, dv_s, *, tb, tl, sl, nprev, unroll):
    li = pl.program_id(1)
    nl = pl.num_programs(1)

    @pl.when(li == 0)
    def _():
        for s in range(tb):
            for h in range(HG_HEADS):
                st[s, h] = s0_ref[s, h].T
        if nprev:
            so_ref[0:nprev] = prev_ref[...]

    nseq = CR // sl
    levels = _hg_levels(sl)

    nlev = len(levels)
    odd_row = (lax.broadcasted_iota(jnp.int32, (CR, MIX_W), 0) & 1) == 1

    def stage_a(pc, carry):
        r0 = pl.multiple_of(pc * CR, CR)
        lb = lb_ref[...]
        q = _silu(q_ref[pl.ds(r0, CR), :].astype(F32))
        sig = _sigmoid(f_ref[pl.ds(r0, CR), :].astype(F32))
        fdec = lb + (1.0 - lb) * sig
        logf = jnp.log(fdec)
        k = (1.0 - lb) * (1.0 - sig)
        v = i_ref[pl.ds(r0, CR), :].astype(F32)
        cums = _sel_l(mcum_ref[...], logf)
        gc = cums[0:CR]
        glast = cums[CR:2 * CR]
        if nlev > 1:
            dl = _sel2_l(mlev_ref[...], logf)
        for n in range(nlev):
            e = jnp.exp(-jnp.abs(dl[n * CR:(n + 1) * CR])) if n < nlev - 1 else jnp.where(odd_row, fdec, 1.0)
            qe_s[pc, n] = (q * e).astype(BF16)
            ke_s[pc, n] = (k * e).astype(BF16)
        qg_s[pc] = (q * jnp.exp(gc)).astype(BF16)
        kd_s[pc] = (k * jnp.exp(glast - gc)).astype(BF16)
        el_s[pc] = jnp.exp(glast)
        qk = q * k
        dv_s[pc] = jnp.concatenate(
            [jnp.sum(qk[:, h * LANE:(h + 1) * LANE], axis=-1, keepdims=True) * v[:, h * LANE:(h + 1) * LANE]
             for h in range(HG_HEADS)], axis=1)
        return carry

    def stage_b(pc, carry):
        r0 = pl.multiple_of(pc * CR, CR)
        heads = [slice(h * LANE, (h + 1) * LANE) for h in range(HG_HEADS)]
        vs = [i_ref[pl.ds(r0, CR), hs] for hs in heads]
        atts = []
        for hs in heads:
            att = None
            for n in range(nlev):
                a = _dot_nt(qe_s[pc, n, :, hs], ke_s[pc, n, :, hs]) * mask_ref[n]
                att = a if att is None else att + a
            atts.append(att.astype(BF16))
        outs = []
        for h, hs in enumerate(heads):
            o = _dot(atts[h], vs[h]) + dv_s[pc, :, hs]
            ois = []
            for s in range(nseq):
                seq = pc * nseq + s if nseq > 1 else 0
                rs = slice(s * sl, (s + 1) * sl)
                stt = st[seq, h]
                if nseq == 1:
                    qg_r, kd_r, v_r = qg_s[pc, :, hs], kd_s[pc, :, hs], vs[h].astype(F32)
                else:
                    qg_r = qg_s[pc, :, hs].astype(F32)[rs].astype(BF16)
                    kd_r = kd_s[pc, :, hs].astype(F32)[rs].astype(BF16)
                    v_r = vs[h].astype(F32)[rs]
                ois.append(_dot_nt(qg_r, stt.astype(BF16)))
                st[seq, h] = stt * el_s[pc, s * sl:s * sl + 1, hs] + _dot(v_r.T.astype(BF16), kd_r)
            outs.append(o + (ois[0] if nseq == 1 else jnp.concatenate(ois, axis=0)))
        for h, hs in enumerate(heads):
            gate = _sigmoid(g_ref[pl.ds(r0, CR), hs].astype(F32))
            o_ref[pl.ds(r0, CR), hs] = (_rms(outs[h], nw_ref[...]) * gate).astype(BF16)
        return carry

    ngrp = tb * tl // CR
    lax.fori_loop(0, ngrp, stage_a, 0, unroll=unroll)
    lax.fori_loop(0, ngrp, stage_b, 0, unroll=unroll)

    @pl.when(li == nl - 1)
    def _():
        for s in range(tb):
            for h in range(HG_HEADS):
                so_ref[nprev, s, h] = st[s, h].T


def _hg_call(proj, lb, nw, s_all, prev, layer, *, b, l, tb, tl, sl, unroll):
    grid, rows, row_map = _seq_grid(b, l, tb, tl)
    assert rows % CR == 0 and CR % sl == 0 and (sl == CR and tb == 1 or sl == tl)
    consts = _hg_consts(sl)
    nlev = len(_hg_levels(sl))
    cb = COL_HG // MIX_W
    nprev, prev, s_spec, prev_spec, so_spec, so_shape = _state_io(s_all, prev, layer, tb)
    stage = lambda dt, *lead: pltpu.VMEM((rows // CR,) + lead + (CR, MIX_W), dt)
    return pl.pallas_call(
        functools.partial(_hg_kernel, tb=tb, tl=tl, sl=sl, nprev=nprev, unroll=unroll),
        grid=grid,
        in_specs=[pl.BlockSpec((rows, MIX_W), row_map(cb + i)) for i in range(4)]
        + [_const_spec(a.shape) for a in (lb, nw) + consts] + [s_spec, prev_spec],
        out_specs=[pl.BlockSpec((rows, MIX_W), row_map(0)), so_spec],
        out_shape=[jax.ShapeDtypeStruct((b * l, MIX_W), BF16), so_shape],
        scratch_shapes=[pltpu.VMEM((tb, HG_HEADS, HG_DK, HG_DK), F32), stage(BF16, nlev), stage(BF16, nlev),
                        stage(BF16), stage(BF16), stage(F32), stage(F32)],
        compiler_params=_cparams(("parallel", "arbitrary")),
        name="hgrn2",
    )(proj, proj, proj, proj, lb, nw, *consts, s_all, prev)


S5_TILES = MIX_W // LANE
S5_TLANES = S5_LANES // S5_TILES


def _s5_perm(tb, tl):
    rows = tb * tl
    p = np.zeros((rows, rows))
    t, s = np.meshgrid(np.arange(tl), np.arange(tb), indexing='ij')
    p[(s * tl + t).ravel(), (t * tb + s).ravel()] = 1.0
    return jnp.asarray(p, BF16), jnp.asarray(p.T, BF16)


def _s5_kernel(su_ref, perm_ref, permt_ref, wbu_ref, are_ref, aim_ref, wc_ref, d_ref, wglu_ref, h0r_ref, h0i_ref,
               o_ref, hr_ref, hi_ref, bur, bui, *, tb, tl):
    @pl.when(pl.program_id(1) == 0)
    def _():
        hr_ref[...] = h0r_ref[...]
        hi_ref[...] = h0i_ref[...]

    rows = tl * tb
    u = _dot(permt_ref[...], su_ref[...].reshape(rows, MIX_W))
    for t in range(S5_TILES):
        bu = _dot(u[:, t * LANE:(t + 1) * LANE].astype(BF16), wbu_ref[t])
        ls = slice(t * S5_TLANES, (t + 1) * S5_TLANES)
        bur[:, ls] = bu[:, :S5_TLANES]
        bui[:, ls] = bu[:, S5_TLANES:]

    for t in range(S5_TILES):
        ls = slice(t * S5_TLANES, (t + 1) * S5_TLANES)
        ar = jnp.broadcast_to(are_ref[:, ls], (tb, S5_TLANES))
        ai = jnp.broadcast_to(aim_ref[:, ls], (tb, S5_TLANES))

        def step(i, carry, ls=ls, ar=ar, ai=ai):
            hr, hi = carry
            r = pl.multiple_of(i * tb, tb)
            nr = (ar * hr - ai * hi) + bur[pl.ds(r, tb), ls]
            ni = (ar * hi + ai * hr) + bui[pl.ds(r, tb), ls]
            bur[pl.ds(r, tb), ls] = nr
            bui[pl.ds(r, tb), ls] = ni
            return nr, ni

        hr, hi = lax.fori_loop(0, tl, step, (hr_ref[:, ls], hi_ref[:, ls]))
        hr_ref[:, ls] = hr
        hi_ref[:, ls] = hi

    ys = []
    for t in range(S5_TILES):
        ls = slice(t * S5_TLANES, (t + 1) * S5_TLANES)
        hcat = jnp.concatenate([bur[:, ls], bui[:, ls]], axis=1).astype(BF16)
        cs = slice(t * LANE, (t + 1) * LANE)
        ys.append(_dot(hcat, wc_ref[t]) + d_ref[:, cs] * u[:, cs])
    z = _gelu_tanh(jnp.concatenate(ys, axis=1))
    out = z * _sigmoid(_dot(z.astype(BF16), wglu_ref[...]))
    o_ref[...] = _dot(perm_ref[...], out.astype(BF16)).astype(BF16).reshape(o_ref.shape)


def _s5_call(proj, wbu, are, aim, wc, d, wglu, h0r, h0i, *, b, l, tb, tl):
    assert l % tl == 0 and b % tb == 0 and tb == SUBLANE
    perm, permt = _s5_perm(tb, tl)
    h_spec = pl.BlockSpec((tb, S5_LANES), lambda bi, li: (bi, 0))
    params = (perm, permt, wbu, are, aim, wc, d, wglu)
    cb = COL_SU // MIX_W
    if tl == l:
        su, su_spec = proj, pl.BlockSpec((tb * tl, MIX_W), lambda bi, li: (bi, cb))
        o_shape, o_spec = (b * l, MIX_W), pl.BlockSpec((tb * tl, MIX_W), lambda bi, li: (bi, 0))
    else:
        su, su_spec = proj.reshape(b, l, -1), pl.BlockSpec((tb, tl, MIX_W), lambda bi, li: (bi, li, cb))
        o_shape, o_spec = (b, l, MIX_W), pl.BlockSpec((tb, tl, MIX_W), lambda bi, li: (bi, li, 0))
    o, hr, hi = pl.pallas_call(
        functools.partial(_s5_kernel, tb=tb, tl=tl),
        grid=(b // tb, l // tl),
        in_specs=[su_spec] + [_const_spec(a.shape) for a in params] + [h_spec, h_spec],
        out_specs=[o_spec, h_spec, h_spec],
        out_shape=[jax.ShapeDtypeStruct(o_shape, BF16), jax.ShapeDtypeStruct(h0r.shape, F32),
                   jax.ShapeDtypeStruct(h0i.shape, F32)],
        scratch_shapes=[pltpu.VMEM((tl * tb, S5_LANES), F32), pltpu.VMEM((tl * tb, S5_LANES), F32)],
        compiler_params=_cparams(("parallel", "arbitrary")),
        name="s5",
    )(su, *params, h0r, h0i)
    return o.reshape(b * l, MIX_W), hr, hi


def _s5_params(p):
    a_re, a_im = p['s5_a_re'], p['s5_a_im']
    dt = jnp.exp(p['s5_log_dt'])[:, None]
    mag = jnp.exp(dt * a_re)
    ab_re, ab_im = mag * jnp.cos(dt * a_im), mag * jnp.sin(dt * a_im)
    den = a_re * a_re + a_im * a_im
    n_re, n_im = ab_re - 1.0, ab_im
    f_re = (n_re * a_re + n_im * a_im) / den
    f_im = (n_im * a_re - n_re * a_im) / den
    b_re, b_im = p['s5_b_re'], p['s5_b_im']
    bb_re = f_re[..., None] * b_re - f_im[..., None] * b_im
    bb_im = f_re[..., None] * b_im + f_im[..., None] * b_re
    gpt = S5_GROUPS // S5_TILES
    eye = jnp.eye(gpt, dtype=F32)

    def pack_b(bb):
        bt = bb.reshape(S5_TILES, gpt, S5_STATE, S5_GROUP)
        w = jnp.einsum('tgpm,gh->tgmhp', bt, eye)
        return w.reshape(S5_TILES, gpt * S5_GROUP, gpt * S5_STATE)

    def pack_c(cc):
        ct = cc.reshape(S5_TILES, gpt, S5_GROUP, S5_STATE)
        w = jnp.einsum('tgmp,gh->tgphm', ct, eye)
        return w.reshape(S5_TILES, gpt * S5_STATE, gpt * S5_GROUP)

    wbu = jnp.concatenate([pack_b(bb_re), pack_b(bb_im)], axis=2).astype(BF16)
    wc = jnp.concatenate([pack_c(p['s5_c_re']), -pack_c(p['s5_c_im'])], axis=1).astype(BF16)
    return (wbu, ab_re.reshape(1, S5_LANES), ab_im.reshape(1, S5_LANES), wc, p['s5_d'].reshape(1, MIX_W))


def _merge_kernel(x_ref, gl_ref, o0, o1, o2, o3, wb_ref, wo_ref, g_ref, out_ref):
    m = None
    for n, o in enumerate((o0, o1, o2, o3)):
        t = _dot(o[...], wb_ref[n])
        t = _sigmoid(gl_ref[:, n * D_MODEL:(n + 1) * D_MODEL].astype(F32)) * t
        m = t if m is None else m + t
    mo = _dot(m.astype(BF16), wo_ref[...])
    out_ref[...] = x_ref[...] + _rms(mo, g_ref[...])


def _merge_call(x2, proj, branches, wb, wo, g, *, tm):
    t = x2.shape[0]
    row = lambda i: (i, 0)
    return pl.pallas_call(
        _merge_kernel,
        grid=(t // tm,),
        in_specs=[pl.BlockSpec((tm, D_MODEL), row), pl.BlockSpec((tm, GATE_COLS), row)]
        + [pl.BlockSpec((tm, MIX_W), row)] * N_BRANCH
        + [_const_spec(wb.shape), _const_spec(wo.shape), _const_spec(g.shape)],
        out_specs=pl.BlockSpec((tm, D_MODEL), row),
        out_shape=jax.ShapeDtypeStruct((t, D_MODEL), F32),
        compiler_params=_cparams(("parallel",)),
        name="merge",
    )(x2, proj, *branches, wb, wo, g)


def _ffn_kernel(x_ref, g1_ref, w1_ref, w2_ref, g2_ref, o_ref, h_ref, acc_ref):
    j = pl.program_id(1)

    @pl.when(j == 0)
    def _():
        h_ref[...] = _rms(x_ref[...], g1_ref[...]).astype(BF16)
        acc_ref[...] = jnp.zeros_like(acc_ref)

    u = jnp.maximum(_dot(h_ref[...], w1_ref[...]), 0.0)
    acc_ref[...] += _dot((u * u).astype(BF16), w2_ref[...])

    @pl.when(j == pl.num_programs(1) - 1)
    def _():
        o_ref[...] = x_ref[...] + _rms(acc_ref[...], g2_ref[...])


def _ffn_call(x2, g1, w1, w2, g2, *, tm, tf):
    t = x2.shape[0]
    return pl.pallas_call(
        _ffn_kernel,
        grid=(t // tm, D_FF // tf),
        in_specs=[
            pl.BlockSpec((tm, D_MODEL), lambda i, j: (i, 0)),
            pl.BlockSpec((1, D_MODEL), lambda i, j: (0, 0)),
            pl.BlockSpec((D_MODEL, tf), lambda i, j: (0, j)),
            pl.BlockSpec((tf, D_MODEL), lambda i, j: (j, 0)),
            pl.BlockSpec((1, D_MODEL), lambda i, j: (0, 0)),
        ],
        out_specs=pl.BlockSpec((tm, D_MODEL), lambda i, j: (i, 0)),
        out_shape=jax.ShapeDtypeStruct((t, D_MODEL), F32),
        scratch_shapes=[pltpu.VMEM((tm, D_MODEL), BF16), pltpu.VMEM((tm, D_MODEL), F32)],
        compiler_params=_cparams(("parallel", "arbitrary")),
        name="ffn",
    )(x2, g1, w1, w2, g2)


def _prep_layer(p, lb):
    w_in = p['w_in']
    o_sdt = 4 * MIX_W + MIX_W + SSD_CONV_DIM
    o_hq = o_sdt + SSD_HEADS
    o_su = o_hq + 4 * MIX_W
    o_gl = o_su + MIX_W
    w_main = jnp.concatenate([
        w_in[:, o_gl:o_gl + GATE_COLS], w_in[:, 0:o_sdt], w_in[:, o_hq:o_su],
        jnp.repeat(w_in[:, o_sdt:o_hq], SSD_HEADDIM, axis=1), w_in[:, o_su:o_gl]], axis=1).astype(BF16)
    assert w_main.shape[1] == PROJ_COLS

    def head_lanes(v):
        return jnp.repeat(v.astype(F32), SSD_HEADDIM).reshape(1, MIX_W)

    q = dict(
        w_main=w_main,
        g_pre_mix=p['g_pre_mix'].reshape(1, D_MODEL),
        g_post_mix=p['g_post_mix'].reshape(1, D_MODEL),
        g_pre_ffn=p['g_pre_ffn'].reshape(1, D_MODEL),
        g_post_ffn=p['g_post_ffn'].reshape(1, D_MODEL),
        ret_gn=p['ret_gn'],
        conv_w=p['ssd_conv_w'],
        conv_b=p['ssd_conv_b'].reshape(1, SSD_CONV_DIM),
        a_log=head_lanes(p['ssd_a_log']),
        dt_bias=head_lanes(p['ssd_dt_bias']),
        d_skip=head_lanes(p['ssd_d']),
        ssd_norm=p['ssd_norm'].reshape(1, MIX_W),
        hg_lb=lb.reshape(1, MIX_W),
        hg_norm=p['hg_norm'].reshape(1, HG_DK),
        s5=_s5_params(p),
        w_glu=p['s5_w_glu'].astype(BF16),
        w_branch=p['w_branch'].astype(BF16),
        w_out=p['w_out'].astype(BF16),
        w_ff1=p['w_ff1'].astype(BF16),
        w_ff2=p['w_ff2'].astype(BF16),
    )
    return q


def _layer(x3, states, acc, layer, q, pos, cfg):
    b, l, _ = x3.shape
    x2 = x3.reshape(b * l, D_MODEL)
    tb, tl, c = cfg['tb'], cfg['tl'], cfg['c']
    proj = _proj_call(x2, q['g_pre_mix'], q['w_main'], tm=cfg['tm_proj'], tn=cfg['tn'])
    st = {n: states[n][layer] for n in ('conv', 's5_re', 's5_im')}
    o_ret, s_ret = _ret_call(proj, pos, q['ret_gn'], states['ret'], acc['ret'], layer, b=b, l=l, tb=tb, tl=tl, sl=c,
                             unroll=cfg['u_ret'])
    o_ssd, conv_new, s_ssd = _ssd_call(proj, q['conv_w'], q['conv_b'], q['a_log'], q['dt_bias'], q['d_skip'],
                                       q['ssd_norm'], st['conv'], states['ssd'], acc['ssd'], layer,
                                       b=b, l=l, tb=tb, tl=tl, sl=c, unroll=cfg['u_ssd'])
    o_hg, s_hg = _hg_call(proj, q['hg_lb'], q['hg_norm'], states['hgrn'], acc['hgrn'], layer, b=b, l=l, tb=tb, tl=tl,
                          sl=c, unroll=cfg['u_hg'])
    wbu, are, aim, wc, d5 = q['s5']
    o_s5, s5_re, s5_im = _s5_call(proj, wbu, are, aim, wc, d5, q['w_glu'],
                                  st['s5_re'].reshape(b, S5_LANES), st['s5_im'].reshape(b, S5_LANES),
                                  b=b, l=l, tb=SUBLANE, tl=cfg['s5_tl'])
    x2 = _merge_call(x2, proj, (o_ret, o_ssd, o_hg, o_s5), q['w_branch'], q['w_out'], q['g_post_mix'], tm=cfg['tm_merge'])
    x2 = _ffn_call(x2, q['g_pre_ffn'], q['w_ff1'], q['w_ff2'], q['g_post_ffn'], tm=cfg['tm'], tf=cfg['tf'])
    def push(name, new):
        return new[None] if acc[name] is None else jnp.concatenate([acc[name], new[None]], axis=0)

    new = {'ret': s_ret, 'ssd': s_ssd, 'hgrn': s_hg, 'conv': push('conv', conv_new),
           's5_re': push('s5_re', s5_re.reshape(b, S5_GROUPS, S5_STATE)),
           's5_im': push('s5_im', s5_im.reshape(b, S5_GROUPS, S5_STATE))}
    return x2.reshape(b, l, D_MODEL), new


_NAMES = ('ret', 'ssd', 'conv', 'hgrn', 's5_re', 's5_im')


def _trunk(x, states, layers, pos, cfg):
    acc = {n: None for n in _NAMES}
    for layer, q in enumerate(layers):
        x, acc = _layer(x, states, acc, layer, q, pos, cfg)
    return x, acc


def _group_cfg(b, l):
    if l % CHUNK == 0:
        return dict(tb=1, tl=min(l, 512), c=CHUNK, tm=1024, tm_proj=2048, tn=1536, tm_merge=512, tf=1024,
                    s5_tl=min(l, 64), u_ret=4, u_ssd=8, u_hg=8)
    return dict(tb=SUBLANE, tl=l, c=l, tm=min(b * l, 1024), tm_proj=min(b * l, 1024), tn=1536,
                tm_merge=min(b * l, 512), tf=1024, s5_tl=l, u_ret=1, u_ssd=1, u_hg=1)


def kernel(x_prompt, x_sample, state_ret, state_ssd, state_conv, state_hgrn, state_s5_re, state_s5_im, g_pre_mix, g_post_mix, g_pre_ffn, g_post_ffn, w_in, ret_gn, ssd_conv_w, ssd_conv_b, ssd_a_log, ssd_dt_bias, ssd_d, ssd_norm, hg_lb_logits, hg_norm, s5_a_re, s5_a_im, s5_b_re, s5_b_im, s5_c_re, s5_c_im, s5_d, s5_log_dt, s5_w_glu, w_branch, w_out, w_ff1, w_ff2):
    params = {
        'g_pre_mix': g_pre_mix, 'g_post_mix': g_post_mix, 'g_pre_ffn': g_pre_ffn, 'g_post_ffn': g_post_ffn,
        'w_in': w_in, 'ret_gn': ret_gn, 'ssd_conv_w': ssd_conv_w, 'ssd_conv_b': ssd_conv_b,
        'ssd_a_log': ssd_a_log, 'ssd_dt_bias': ssd_dt_bias, 'ssd_d': ssd_d, 'ssd_norm': ssd_norm,
        'hg_norm': hg_norm, 's5_a_re': s5_a_re, 's5_a_im': s5_a_im, 's5_b_re': s5_b_re, 's5_b_im': s5_b_im,
        's5_c_re': s5_c_re, 's5_c_im': s5_c_im, 's5_d': s5_d, 's5_log_dt': s5_log_dt, 's5_w_glu': s5_w_glu,
        'w_branch': w_branch, 'w_out': w_out, 'w_ff1': w_ff1, 'w_ff2': w_ff2,
    }
    depth = w_in.shape[0]
    w = jax.nn.softmax(hg_lb_logits.astype(F32), axis=0)
    lbs = jnp.cumsum(w, axis=0) - w[0]
    layers = [_prep_layer({k: v[i] for k, v in params.items()}, lbs[i]) for i in range(depth)]

    bp, lp, _ = x_prompt.shape
    bs, ls, _ = x_sample.shape
    zero_states = {
        'ret': jnp.zeros((depth, bp) + state_ret.shape[2:], F32),
        'ssd': jnp.zeros((depth, bp) + state_ssd.shape[2:], F32),
        'conv': jnp.zeros((depth, bp) + state_conv.shape[2:], F32),
        'hgrn': jnp.zeros((depth, bp) + state_hgrn.shape[2:], F32),
        's5_re': jnp.zeros((depth, bp) + state_s5_re.shape[2:], F32),
        's5_im': jnp.zeros((depth, bp) + state_s5_im.shape[2:], F32),
    }
    sample_states = {'ret': state_ret, 'ssd': state_ssd, 'conv': state_conv, 'hgrn': state_hgrn,
                     's5_re': state_s5_re, 's5_im': state_s5_im}
    pos_p = np.arange(lp)
    pos_s = PAST_LEN + np.arange(ls)
    y_p, new_p = _trunk(x_prompt, zero_states, layers, pos_p, _group_cfg(bp, lp))
    y_s, new_s = _trunk(x_sample, sample_states, layers, pos_s, _group_cfg(bs, ls))
    return (y_p, y_s,
            new_p['ret'], new_s['ret'], new_p['ssd'], new_s['ssd'], new_p['conv'], new_s['conv'],
            new_p['hgrn'], new_s['hgrn'], new_p['s5_re'], new_s['s5_re'], new_p['s5_im'], new_s['s5_im'])
```

```python
import functools
import math

import jax
import jax.numpy as jnp
import numpy as np
from jax import lax
from jax.experimental import pallas as pl
from jax.experimental.pallas import tpu as pltpu

F32 = jnp.float32
BF16 = jnp.bfloat16

D_MODEL = 1024
DEPTH = 2
PAST_LEN = 16384
MIX_W = D_MODEL // 2
N_BRANCH = 4
RET_HEADS = 4
RET_DK = MIX_W // RET_HEADS
ROPE_BASE = 10000.0
SSD_HEADDIM = 64
SSD_HEADS = MIX_W // SSD_HEADDIM
SSD_GROUPS = 2
SSD_STATE = 128
SSD_CONV = 4
SSD_CONV_DIM = MIX_W + 2 * SSD_GROUPS * SSD_STATE
SSD_PAIRS = SSD_HEADS // 2
HG_HEADS = 4
HG_DK = MIX_W // HG_HEADS
S5_GROUP = 16
S5_GROUPS = MIX_W // S5_GROUP
S5_STATE = 64
S5_LANES = S5_GROUPS * S5_STATE
D_FF = 4 * D_MODEL
GATE_COLS = N_BRANCH * D_MODEL
CHUNK = 64
CR = 64
EPS = 1e-6

LANE = 128
SUBLANE = 8
VMEM_LIMIT = 48 * 1024 * 1024

COL_RET = 0
COL_SZ = COL_RET + 4 * MIX_W
COL_XS = COL_SZ + MIX_W
COL_BC = COL_XS + MIX_W
COL_HG = COL_BC + MIX_W
COL_SU = COL_HG + 4 * MIX_W
COL_GL = COL_SU + MIX_W
COL_DT = COL_GL + GATE_COLS
PROJ_COLS = COL_DT + MIX_W
GL_BLK = GATE_COLS // 2
assert COL_GL % GL_BLK == 0


def _sigmoid(x):
    return 1.0 / (1.0 + jnp.exp(-x))


def _silu(x):
    return x * _sigmoid(x)


def _softplus(x):
    return jnp.maximum(x, 0.0) + jnp.log1p(jnp.exp(-jnp.abs(x)))


def _gelu_tanh(x):
    c = math.sqrt(2.0 / math.pi)
    return 0.5 * x * (1.0 + jnp.tanh(c * (x + 0.044715 * (x * x * x))))


def _rms(x, g):
    ms = jnp.mean(x * x, axis=-1, keepdims=True)
    return x * lax.rsqrt(ms + EPS) * g


def _dot(a, b):
    return jnp.dot(a, b, preferred_element_type=F32)


def _dot_nt(a, b):
    return lax.dot_general(a, b, (((1,), (1,)), ((), ())), preferred_element_type=F32)


def _split3(a):
    hi = a.astype(BF16)
    r1 = a - hi.astype(F32)
    mid = r1.astype(BF16)
    lo = (r1 - mid.astype(F32)).astype(BF16)
    return hi, mid, lo


def _sel_l(m, a):
    hi, mid, lo = _split3(a)
    return (_dot(m, lo) + _dot(m, mid)) + _dot(m, hi)


def _sel2_l(m, a):
    hi = a.astype(BF16)
    mid = (a - hi.astype(F32)).astype(BF16)
    return _dot(m, mid) + _dot(m, hi)


def _sel_r(a, m):
    hi, mid, lo = _split3(a)
    return (_dot(lo, m) + _dot(mid, m)) + _dot(hi, m)


def _cparams(sem):
    return pltpu.CompilerParams(dimension_semantics=sem, vmem_limit_bytes=VMEM_LIMIT)


def _const_spec(shape):
    nd = len(shape)
    return pl.BlockSpec(shape, lambda *_: (0,) * nd)


def _state_io(s_all, prev, layer, tb):
    dims = tuple(s_all.shape[2:])
    zeros = (0,) * len(dims)
    in_spec = pl.BlockSpec((None, tb) + dims, lambda bi, li: (layer, bi) + zeros)
    if prev is None:
        nprev, prev, prev_spec = 0, s_all, in_spec
    else:
        nprev = prev.shape[0]
        prev_spec = pl.BlockSpec((nprev, tb) + dims, lambda bi, li: (0, bi) + zeros)
    out_spec = pl.BlockSpec((nprev + 1, tb) + dims, lambda bi, li: (0, bi) + zeros)
    out_shape = jax.ShapeDtypeStruct((nprev + 1, s_all.shape[1]) + dims, F32)
    return nprev, prev, in_spec, prev_spec, out_spec, out_shape


def _proj_kernel(x_ref, g_ref, w_ref, o_ref, h_ref):
    @pl.when(pl.program_id(1) == 0)
    def _():
        h_ref[...] = _rms(x_ref[...], g_ref[...]).astype(BF16)

    o_ref[...] = _dot(h_ref[...], w_ref[...]).astype(BF16)


def _proj_call(x2, g, w, *, tm, tn):
    t, d = x2.shape
    n = w.shape[1]
    return pl.pallas_call(
        _proj_kernel,
        grid=(t // tm, n // tn),
        in_specs=[
            pl.BlockSpec((tm, d), lambda i, j: (i, 0)),
            pl.BlockSpec((1, d), lambda i, j: (0, 0)),
            pl.BlockSpec((d, tn), lambda i, j: (0, j)),
        ],
        out_specs=pl.BlockSpec((tm, tn), lambda i, j: (i, j)),
        out_shape=jax.ShapeDtypeStruct((t, n), BF16),
        scratch_shapes=[pltpu.VMEM((tm, d), BF16)],
        compiler_params=_cparams(("parallel", "arbitrary")),
        name="in_proj",
    )(x2, g, w)


def _ret_consts(sl):
    nseq = CR // sl
    h = np.arange(RET_HEADS, dtype=np.float64)
    log_g = np.log(1.0 - 2.0 ** (-5.0 - h))
    idx = np.arange(sl, dtype=np.float64)
    diff = idx[:, None] - idx[None, :]
    dmat = np.where(diff >= 0, np.exp(log_g[:, None, None] * np.maximum(diff, 0.0)), 0.0)
    dmat = np.stack([np.kron(np.eye(nseq), d) for d in dmat])
    q_dec = np.tile(np.exp(log_g[:, None] * (idx[None, :] + 1.0)), (1, nseq))
    k_dec = np.tile(np.exp(log_g[:, None] * (sl - 1.0 - idx[None, :])), (1, nseq))
    chunk_dec = np.exp(log_g * sl)
    qd = np.broadcast_to(q_dec[:, :, None], (RET_HEADS, CR, LANE))
    kd = np.broadcast_to(k_dec[:, :, None], (RET_HEADS, CR, LANE))
    return (jnp.asarray(dmat, F32), jnp.asarray(qd, F32), jnp.asarray(kd, F32),
            tuple(float(v) for v in chunk_dec))


def _rope_tables(pos):
    half = RET_DK // 2
    inv = ROPE_BASE ** (-np.arange(half, dtype=np.float64) / half)
    ang = np.asarray(pos, np.float64)[:, None] * inv[None, :]
    cos, sin = np.cos(ang), np.sin(ang)
    return (jnp.asarray(np.concatenate([cos, cos], axis=1), F32),
            jnp.asarray(np.concatenate([-sin, sin], axis=1), F32))


def _ret_kernel(q_ref, k_ref, v_ref, g_ref, cos_ref, sin_ref, dmat_ref, qd_ref, kd_ref, gn_ref, s0_ref, prev_ref,
                o_ref, so_ref, *, tb, tl, sl, cdec, nprev, unroll):
    st_ref = so_ref.at[nprev]

    @pl.when(pl.program_id(1) == 0)
    def _():
        st_ref[...] = s0_ref[...]
        if nprev:
            so_ref[0:nprev] = prev_ref[...]

    nseq = CR // sl
    scale = RET_DK ** -0.5

    def body(pc, carry):
        r0 = pl.multiple_of(pc * CR, CR)
        t0 = r0 if nseq == 1 else 0
        cos = cos_ref[pl.ds(t0, CR), :]
        sin = sin_ref[pl.ds(t0, CR), :]
        heads = [slice(h * LANE, (h + 1) * LANE) for h in range(RET_HEADS)]
        qrs, krs = [], []
        for hs in heads:
            q = q_ref[pl.ds(r0, CR), hs].astype(F32)
            k = k_ref[pl.ds(r0, CR), hs].astype(F32)
            qrs.append(q * cos + pltpu.roll(q, LANE // 2, axis=1) * sin)
            krs.append((k * cos + pltpu.roll(k, LANE // 2, axis=1) * sin) * scale)
        vs = [v_ref[pl.ds(r0, CR), hs] for hs in heads]
        atts = [(_dot_nt(qrs[h].astype(BF16), krs[h].astype(BF16)) * dmat_ref[h]).astype(BF16)
                for h in range(RET_HEADS)]
        outs = []
        for h in range(RET_HEADS):
            o = _dot(atts[h], vs[h])
            qd = qrs[h] * qd_ref[h]
            kd = krs[h] * kd_ref[h]
            ois = []
            for s in range(nseq):
                seq = pc * nseq + s if nseq > 1 else 0
                rs = slice(s * sl, (s + 1) * sl)
                st = st_ref[seq, h]
                ois.append(_dot(qd[rs].astype(BF16), st.astype(BF16)))
                v_s = vs[h] if nseq == 1 else vs[h].astype(F32)[rs].astype(BF16)
                st_ref[seq, h] = st * cdec[h] + _dot(kd[rs].T.astype(BF16), v_s)
            outs.append(o + (ois[0] if nseq == 1 else jnp.concatenate(ois, axis=0)))
        for h, hs in enumerate(heads):
            o = outs[h]
            mu = jnp.mean(o, axis=-1, keepdims=True)
            oc = o - mu
            var = jnp.mean(oc * oc, axis=-1, keepdims=True)
            on = oc * lax.rsqrt(var + EPS) * gn_ref[h:h + 1, :]
            o_ref[pl.ds(r0, CR), hs] = (on * _silu(g_ref[pl.ds(r0, CR), hs].astype(F32))).astype(BF16)
        return carry

    lax.fori_loop(0, tb * tl // CR, body, 0, unroll=unroll)


def _seq_grid(b, l, tb, tl):
    assert b % tb == 0 and l % tl == 0 and (tb == 1 or tl == l)
    nl = l // tl
    rows = tb * tl

    def row_map(col_blk):
        return lambda bi, li: (bi * nl + li, col_blk)

    return (b // tb, nl), rows, row_map


def _ret_call(proj, pos, gn, s_all, prev, layer, *, b, l, tb, tl, sl, unroll):
    grid, rows, row_map = _seq_grid(b, l, tb, tl)
    assert rows % CR == 0 and CR % sl == 0 and (sl == CR and tb == 1 or sl == tl)
    dmat, qd, kd, cdec = _ret_consts(sl)
    cos, sin = _rope_tables(pos)
    if sl < CR:
        cos, sin = jnp.tile(cos, (CR // sl, 1)), jnp.tile(sin, (CR // sl, 1))
        tab_spec = _const_spec(cos.shape)
    else:
        tab_spec = pl.BlockSpec((tl, LANE), lambda bi, li: (li, 0))
    cb = COL_RET // MIX_W
    nprev, prev, s_spec, prev_spec, so_spec, so_shape = _state_io(s_all, prev, layer, tb)
    return pl.pallas_call(
        functools.partial(_ret_kernel, tb=tb, tl=tl, sl=sl, cdec=cdec, nprev=nprev, unroll=unroll),
        grid=grid,
        in_specs=[pl.BlockSpec((rows, MIX_W), row_map(cb + i)) for i in range(4)] + [
            tab_spec, tab_spec,
            _const_spec(dmat.shape), _const_spec(qd.shape), _const_spec(kd.shape), _const_spec(gn.shape),
            s_spec, prev_spec,
        ],
        out_specs=[pl.BlockSpec((rows, MIX_W), row_map(0)), so_spec],
        out_shape=[jax.ShapeDtypeStruct((b * l, MIX_W), BF16), so_shape],
        compiler_params=_cparams(("parallel", "arbitrary")),
        name="retention",
    )(proj, proj, proj, proj, cos, sin, dmat, qd, kd, gn, s_all, prev)


def _ssd_consts(sl):
    nseq = CR // sl
    tril = np.kron(np.eye(nseq), np.tril(np.ones((sl, sl))))
    sumall = np.kron(np.eye(nseq), np.ones((sl, sl)))
    ones = np.ones((CR, CR))
    i = np.arange(CR)[:, None]
    j = np.arange(MIX_W)[None, :] % SSD_HEADDIM
    eye = (i == j).astype(np.float64)
    mask = ((i // sl == j // sl) & (i >= j)).astype(np.float64)
    return (jnp.asarray(np.concatenate([tril, sumall], axis=0), BF16), jnp.asarray(ones, BF16),
            jnp.asarray(eye, F32), jnp.asarray(mask, F32))


def _ssd_kernel(z_ref, xs_ref, bc_ref, dtx_ref, cw_ref, cb_ref, alog_ref, dtb_ref, dsk_ref, nw_ref,
                mcum_ref, ones_ref, eye_ref, mask_ref, cs0_ref, s0_ref, prev_ref,
                o_ref, cs_ref, so_ref, xpad, st, *, tb, tl, sl, nprev, unroll):
    li = pl.program_id(1)
    nl = pl.num_programs(1)
    nseq = CR // sl
    kc = SSD_CONV - 1
    off = SUBLANE - kc
    grp = 2 * SUBLANE

    @pl.when(li == 0)
    def _():
        for s in range(tb):
            for p in range(SSD_PAIRS):
                st[s, p] = s0_ref[s, p].T
        if nprev:
            so_ref[0:nprev] = prev_ref[...]
        if nseq == 1:
            xpad[0:SUBLANE, :] = jnp.zeros((SUBLANE, SSD_CONV_DIM), F32)
            xpad[off:SUBLANE, :] = cs0_ref[0]

    if nseq == 1:
        xpad[SUBLANE:SUBLANE + tl, 0:MIX_W] = xs_ref[...].astype(F32)
        xpad[SUBLANE:SUBLANE + tl, MIX_W:2 * MIX_W] = bc_ref[...].astype(F32)
    else:
        xs_new = xs_ref[...].astype(F32)
        bc_new = bc_ref[...].astype(F32)
        for s in range(tb):
            xpad[s * grp:s * grp + SUBLANE, :] = jnp.zeros((SUBLANE, SSD_CONV_DIM), F32)
            xpad[s * grp + off:s * grp + SUBLANE, :] = cs0_ref[s]
            xpad[s * grp + SUBLANE:(s + 1) * grp, 0:MIX_W] = xs_new[s * sl:(s + 1) * sl, :]
            xpad[s * grp + SUBLANE:(s + 1) * grp, MIX_W:2 * MIX_W] = bc_new[s * sl:(s + 1) * sl, :]
            cs_ref[s] = xpad[(s + 1) * grp - kc:(s + 1) * grp, :]

    a_neg = -jnp.exp(alog_ref[...])
    half = lax.broadcasted_iota(jnp.int32, (CR, LANE), 1) < SSD_HEADDIM

    def body(pc, carry):
        r0 = pl.multiple_of(pc * CR, CR)
        if nseq == 1:
            win = xpad[pl.ds(r0, CR + SUBLANE), :]
            take = lambda a: a[SUBLANE:, :]
        else:
            win = xpad[pl.ds(pl.multiple_of(pc * nseq * grp, nseq * grp), nseq * grp), :]
            take = lambda a: a.reshape(nseq, grp, SSD_CONV_DIM)[:, SUBLANE:, :].reshape(CR, SSD_CONV_DIM)
        acc = cb_ref[...] + take(pltpu.roll(win, kc, axis=0)) * cw_ref[0:1, :]
        for j in range(1, kc):
            acc = acc + take(pltpu.roll(win, kc - j, axis=0)) * cw_ref[j:j + 1, :]
        xc = _silu(acc + take(win) * cw_ref[kc:kc + 1, :])
        xs = xc[:, 0:MIX_W]
        bm = xc[:, MIX_W:MIX_W + SSD_GROUPS * SSD_STATE]
        cm = xc[:, MIX_W + SSD_GROUPS * SSD_STATE:2 * MIX_W]

        dtx = _softplus(dtx_ref[pl.ds(r0, CR), :].astype(F32) + dtb_ref[...])
        both = _sel_l(mcum_ref[...], dtx * a_neg)
        cum, clast = both[0:CR], both[CR:2 * CR]
        ecum = jnp.exp(cum)
        dend = jnp.exp(clast - cum)
        elast = jnp.exp(clast)
        rr = _sel_l(ones_ref[...], cum * eye_ref[...])
        msk = mask_ref[...]
        lmat = jnp.exp(jnp.where(msk > 0, cum - rr, 0.0)) * msk
        zz = z_ref[pl.ds(r0, CR), :].astype(F32)
        npp = SSD_PAIRS // SSD_GROUPS
        bm_gs = [bm[:, g * SSD_STATE:(g + 1) * SSD_STATE] for g in range(SSD_GROUPS)]
        cm_gs = [cm[:, g * SSD_STATE:(g + 1) * SSD_STATE] for g in range(SSD_GROUPS)]
        cm_gbs = [c_.astype(BF16) for c_ in cm_gs]
        cb2s = [_dot_nt(cm_gbs[g], jnp.concatenate([bm_gs[g], bm_gs[g]], axis=0).astype(BF16))
                for g in range(SSD_GROUPS)]
        pss = [slice(p * LANE, (p + 1) * LANE) for p in range(SSD_PAIRS)]
        xdts = [xs[:, ps] * dtx[:, ps] for ps in pss]
        xsts = [jnp.concatenate([jnp.where(half, x_, 0.0), jnp.where(half, 0.0, x_)], axis=0).astype(BF16)
                for x_ in xdts]
        ys = [_dot((cb2s[p // npp] * lmat[:, pss[p]]).astype(BF16), xsts[p]) for p in range(SSD_PAIRS)]
        xds = [xdts[p] * dend[:, pss[p]] for p in range(SSD_PAIRS)]
        yis = []
        for p in range(SSD_PAIRS):
            g = p // npp
            parts = []
            for s in range(nseq):
                seq = pc * nseq + s if nseq > 1 else 0
                if nseq == 1:
                    cm_s, bm_s, xd_s = cm_gbs[g], bm_gs[g], xds[p]
                else:
                    rs = slice(s * sl, (s + 1) * sl)
                    cm_s, bm_s, xd_s = cm_gs[g][rs].astype(BF16), bm_gs[g][rs], xds[p][rs]
                stp = st[seq, p]
                parts.append(_dot(cm_s, stp.astype(BF16)))
                st[seq, p] = stp * elast[s * sl:s * sl + 1, pss[p]] + _dot(bm_s.T.astype(BF16), xd_s.astype(BF16))
            yis.append(parts[0] if nseq == 1 else jnp.concatenate(parts, axis=0))
        gw = SSD_HEADDIM * SSD_HEADS // SSD_GROUPS
        for g in range(SSD_GROUPS):
            yg = jnp.concatenate([ys[p] + yis[p] * ecum[:, pss[p]] + xs[:, pss[p]] * dsk_ref[:, pss[p]]
                                  for p in range(g * npp, (g + 1) * npp)], axis=1)
            zs = slice(g * gw, (g + 1) * gw)
            o_ref[pl.ds(r0, CR), zs] = _rms(yg * _silu(zz[:, zs]), nw_ref[:, zs]).astype(BF16)
        return carry

    lax.fori_loop(0, tb * tl // CR, body, 0, unroll=unroll)

    if nseq == 1:
        xpad[0:SUBLANE, :] = xpad[tl:tl + SUBLANE, :]

    @pl.when(li == nl - 1)
    def _():
        if nseq == 1:
            cs_ref[0] = xpad[tl + off:tl + SUBLANE, :]
        for s in range(tb):
            for p in range(SSD_PAIRS):
                so_ref[nprev, s, p] = st[s, p].T


def _ssd_call(proj, cw, cb, alog, dtb, dsk, nw, cs0, s_all, prev, layer, *, b, l, tb, tl, sl, unroll):
    grid, rows, row_map = _seq_grid(b, l, tb, tl)
    assert rows % CR == 0 and CR % sl == 0 and (sl == CR and tb == 1 or sl == tl == SUBLANE)
    consts = _ssd_consts(sl)
    pair = lambda a: a.reshape(a.shape[:2] + (SSD_PAIRS, LANE, SSD_STATE))
    s_pairs = pair(s_all)
    nprev, prev, s_spec, prev_spec, so_spec, so_shape = _state_io(s_pairs, None if prev is None else pair(prev),
                                                                  layer, tb)
    cs_spec = pl.BlockSpec((tb, SSD_CONV - 1, SSD_CONV_DIM), lambda bi, li: (bi, 0, 0))
    params = (cw, cb, alog, dtb, dsk, nw)
    xpad_rows = SUBLANE + tl if sl == CR else tb * 2 * SUBLANE
    o, cs, so = pl.pallas_call(
        functools.partial(_ssd_kernel, tb=tb, tl=tl, sl=sl, nprev=nprev, unroll=unroll),
        grid=grid,
        in_specs=[pl.BlockSpec((rows, MIX_W), row_map(cb_)) for cb_ in
                  (COL_SZ // MIX_W, COL_XS // MIX_W, COL_BC // MIX_W, COL_DT // MIX_W)]
        + [_const_spec(a.shape) for a in params + consts] + [cs_spec, s_spec, prev_spec],
        out_specs=[pl.BlockSpec((rows, MIX_W), row_map(0)), cs_spec, so_spec],
        out_shape=[jax.ShapeDtypeStruct((b * l, MIX_W), BF16), jax.ShapeDtypeStruct(cs0.shape, F32), so_shape],
        scratch_shapes=[pltpu.VMEM((xpad_rows, SSD_CONV_DIM), F32),
                        pltpu.VMEM((tb, SSD_PAIRS, SSD_STATE, LANE), F32)],
        compiler_params=_cparams(("parallel", "arbitrary")),
        name="ssd",
    )(proj, proj, proj, proj, *params, *consts, cs0, s_pairs, prev)
    return o, cs, so.reshape((nprev + 1,) + s_all.shape[1:])


def _hg_levels(sl):
    ms = []
    m = sl // 2
    while m >= 1:
        ms.append(m)
        m //= 2
    return ms


def _hg_consts(sl):
    nseq = CR // sl
    tril = np.kron(np.eye(nseq), np.tril(np.ones((sl, sl))))
    sumall = np.kron(np.eye(nseq), np.ones((sl, sl)))
    mats = []
    masks = []
    i = np.arange(CR)
    for m in _hg_levels(sl):
        ref = (i // (2 * m)) * (2 * m) + m - 1
        if m > 1:
            mats.append(tril - tril[ref])
        same = (i[:, None] // (2 * m)) == (i[None, :] // (2 * m))
        qside = ((i // m) % 2 == 1)[:, None]
        kside = ((i // m) % 2 == 0)[None, :]
        masks.append((same & qside & kside).astype(np.float64))
    return (jnp.asarray(np.concatenate([tril, sumall], axis=0), BF16), jnp.asarray(np.concatenate(mats, axis=0), BF16),
            jnp.asarray(np.stack(masks), F32))


def _hg_kernel(q_ref, f_ref, i_ref, g_ref, lb_ref, nw_ref, mcum_ref, mlev_ref, mask_ref, s0_ref, prev_ref,
               o_ref, so_ref, st, qe_s, ke_s, qg_s, kd_s, el_s, dv_s, *, tb, tl, sl, nprev, unroll):
    li = pl.program_id(1)
    nl = pl.num_programs(1)

    @pl.when(li == 0)
    def _():
        for s in range(tb):
            for h in range(HG_HEADS):
                st[s, h] = s0_ref[s, h].T
        if nprev:
            so_ref[0:nprev] = prev_ref[...]

    nseq = CR // sl
    levels = _hg_levels(sl)

    nlev = len(levels)
    odd_row = (lax.broadcasted_iota(jnp.int32, (CR, MIX_W), 0) & 1) == 1

    def stage_a(pc, carry):
        r0 = pl.multiple_of(pc * CR, CR)
        lb = lb_ref[...]
        q = _silu(q_ref[pl.ds(r0, CR), :].astype(F32))
        sig = _sigmoid(f_ref[pl.ds(r0, CR), :].astype(F32))
        fdec = lb + (1.0 - lb) * sig
        logf = jnp.log(fdec)
        k = (1.0 - lb) * (1.0 - sig)
        v = i_ref[pl.ds(r0, CR), :].astype(F32)
        cums = _sel_l(mcum_ref[...], logf)
        gc = cums[0:CR]
        glast = cums[CR:2 * CR]
        if nlev > 1:
            dl = _sel2_l(mlev_ref[...], logf)
        for n in range(nlev):
            e = jnp.exp(-jnp.abs(dl[n * CR:(n + 1) * CR])) if n < nlev - 1 else jnp.where(odd_row, fdec, 1.0)
            qe_s[pc, n] = (q * e).astype(BF16)
            ke_s[pc, n] = (k * e).astype(BF16)
        qg_s[pc] = (q * jnp.exp(gc)).astype(BF16)
        kd_s[pc] = (k * jnp.exp(glast - gc)).astype(BF16)
        el_s[pc] = jnp.exp(glast)
        qk = q * k
        dv_s[pc] = jnp.concatenate(
            [jnp.sum(qk[:, h * LANE:(h + 1) * LANE], axis=-1, keepdims=True) * v[:, h * LANE:(h + 1) * LANE]
             for h in range(HG_HEADS)], axis=1)
        return carry

    def stage_b(pc, carry):
        r0 = pl.multiple_of(pc * CR, CR)
        heads = [slice(h * LANE, (h + 1) * LANE) for h in range(HG_HEADS)]
        vs = [i_ref[pl.ds(r0, CR), hs] for hs in heads]
        atts = []
        for hs in heads:
            att = None
            for n in range(nlev):
                a = _dot_nt(qe_s[pc, n, :, hs], ke_s[pc, n, :, hs]) * mask_ref[n]
                att = a if att is None else att + a
            atts.append(att.astype(BF16))
        outs = []
        for h, hs in enumerate(heads):
            o = _dot(atts[h], vs[h]) + dv_s[pc, :, hs]
            ois = []
            for s in range(nseq):
                seq = pc * nseq + s if nseq > 1 else 0
                rs = slice(s * sl, (s + 1) * sl)
                stt = st[seq, h]
                if nseq == 1:
                    qg_r, kd_r, v_r = qg_s[pc, :, hs], kd_s[pc, :, hs], vs[h].astype(F32)
                else:
                    qg_r = qg_s[pc, :, hs].astype(F32)[rs].astype(BF16)
                    kd_r = kd_s[pc, :, hs].astype(F32)[rs].astype(BF16)
                    v_r = vs[h].astype(F32)[rs]
                ois.append(_dot_nt(qg_r, stt.astype(BF16)))
                st[seq, h] = stt * el_s[pc, s * sl:s * sl + 1, hs] + _dot(v_r.T.astype(BF16), kd_r)
            outs.append(o + (ois[0] if nseq == 1 else jnp.concatenate(ois, axis=0)))
        for h, hs in enumerate(heads):
            gate = _sigmoid(g_ref[pl.ds(r0, CR), hs].astype(F32))
            o_ref[pl.ds(r0, CR), hs] = (_rms(outs[h], nw_ref[...]) * gate).astype(BF16)
        return carry

    ngrp = tb * tl // CR
    lax.fori_loop(0, ngrp, stage_a, 0, unroll=unroll)
    lax.fori_loop(0, ngrp, stage_b, 0, unroll=unroll)

    @pl.when(li == nl - 1)
    def _():
        for s in range(tb):
            for h in range(HG_HEADS):
                so_ref[nprev, s, h] = st[s, h].T


def _hg_call(proj, lb, nw, s_all, prev, layer, *, b, l, tb, tl, sl, unroll):
    grid, rows, row_map = _seq_grid(b, l, tb, tl)
    assert rows % CR == 0 and CR % sl == 0 and (sl == CR and tb == 1 or sl == tl)
    consts = _hg_consts(sl)
    nlev = len(_hg_levels(sl))
    cb = COL_HG // MIX_W
    nprev, prev, s_spec, prev_spec, so_spec, so_shape = _state_io(s_all, prev, layer, tb)
    stage = lambda dt, *lead: pltpu.VMEM((rows // CR,) + lead + (CR, MIX_W), dt)
    return pl.pallas_call(
        functools.partial(_hg_kernel, tb=tb, tl=tl, sl=sl, nprev=nprev, unroll=unroll),
        grid=grid,
        in_specs=[pl.BlockSpec((rows, MIX_W), row_map(cb + i)) for i in range(4)]
        + [_const_spec(a.shape) for a in (lb, nw) + consts] + [s_spec, prev_spec],
        out_specs=[pl.BlockSpec((rows, MIX_W), row_map(0)), so_spec],
        out_shape=[jax.ShapeDtypeStruct((b * l, MIX_W), BF16), so_shape],
        scratch_shapes=[pltpu.VMEM((tb, HG_HEADS, HG_DK, HG_DK), F32), stage(BF16, nlev), stage(BF16, nlev),
                        stage(BF16), stage(BF16), stage(F32), stage(F32)],
        compiler_params=_cparams(("parallel", "arbitrary")),
        name="hgrn2",
    )(proj, proj, proj, proj, lb, nw, *consts, s_all, prev)


S5_TILES = MIX_W // LANE
S5_TLANES = S5_LANES // S5_TILES


def _s5_perm(tb, tl):
    rows = tb * tl
    p = np.zeros((rows, rows))
    t, s = np.meshgrid(np.arange(tl), np.arange(tb), indexing='ij')
    p[(s * tl + t).ravel(), (t * tb + s).ravel()] = 1.0
    return jnp.asarray(p, BF16), jnp.asarray(p.T, BF16)


def _s5_kernel(su_ref, perm_ref, permt_ref, wbu_ref, are_ref, aim_ref, wc_ref, d_ref, wglu_ref, h0r_ref, h0i_ref,
               o_ref, hr_ref, hi_ref, bur, bui, *, tb, tl):
    @pl.when(pl.program_id(1) == 0)
    def _():
        hr_ref[...] = h0r_ref[...]
        hi_ref[...] = h0i_ref[...]

    rows = tl * tb
    u = _dot(permt_ref[...], su_ref[...].reshape(rows, MIX_W))
    tiles = [slice(t * S5_TLANES, (t + 1) * S5_TLANES) for t in range(S5_TILES)]

    def bu_tile(t):
        bu = _dot(u[:, t * LANE:(t + 1) * LANE].astype(BF16), wbu_ref[t])
        bur[:, tiles[t]] = bu[:, :S5_TLANES]
        bui[:, tiles[t]] = bu[:, S5_TLANES:]

    def scan_tile(t):
        ls = tiles[t]
        ar = jnp.broadcast_to(are_ref[:, ls], (tb, S5_TLANES))
        ai = jnp.broadcast_to(aim_ref[:, ls], (tb, S5_TLANES))
        hr, hi = hr_ref[:, ls], hi_ref[:, ls]
        for i in range(tl):
            rs = slice(i * tb, (i + 1) * tb)
            hr, hi = (ar * hr - ai * hi) + bur[rs, ls], (ar * hi + ai * hr) + bui[rs, ls]
            bur[rs, ls] = hr
            bui[rs, ls] = hi
        hr_ref[:, ls] = hr
        hi_ref[:, ls] = hi

    def y_tile(t):
        hcat = jnp.concatenate([bur[:, tiles[t]], bui[:, tiles[t]]], axis=1).astype(BF16)
        cs = slice(t * LANE, (t + 1) * LANE)
        return _dot(hcat, wc_ref[t]) + d_ref[:, cs] * u[:, cs]

    ys = [None] * S5_TILES
    bu_tile(0)
    for t in range(S5_TILES):
        if t + 1 < S5_TILES:
            bu_tile(t + 1)
        scan_tile(t)
        if t >= 1:
            ys[t - 1] = y_tile(t - 1)
    ys[S5_TILES - 1] = y_tile(S5_TILES - 1)
    z = _gelu_tanh(jnp.concatenate(ys, axis=1))
    out = z * _sigmoid(_dot(z.astype(BF16), wglu_ref[...]))
    o_ref[...] = _dot(perm_ref[...], out.astype(BF16)).astype(BF16).reshape(o_ref.shape)


def _s5_call(proj, wbu, are, aim, wc, d, wglu, h0r, h0i, *, b, l, tb, tl):
    assert l % tl == 0 and b % tb == 0 and tb == SUBLANE
    perm, permt = _s5_perm(tb, tl)
    h_spec = pl.BlockSpec((tb, S5_LANES), lambda bi, li: (bi, 0))
    params = (perm, permt, wbu, are, aim, wc, d, wglu)
    cb = COL_SU // MIX_W
    if tl == l:
        su, su_spec = proj, pl.BlockSpec((tb * tl, MIX_W), lambda bi, li: (bi, cb))
        o_shape, o_spec = (b * l, MIX_W), pl.BlockSpec((tb * tl, MIX_W), lambda bi, li: (bi, 0))
    else:
        su, su_spec = proj.reshape(b, l, -1), pl.BlockSpec((tb, tl, MIX_W), lambda bi, li: (bi, li, cb))
        o_shape, o_spec = (b, l, MIX_W), pl.BlockSpec((tb, tl, MIX_W), lambda bi, li: (bi, li, 0))
    o, hr, hi = pl.pallas_call(
        functools.partial(_s5_kernel, tb=tb, tl=tl),
        grid=(b // tb, l // tl),
        in_specs=[su_spec] + [_const_spec(a.shape) for a in params] + [h_spec, h_spec],
        out_specs=[o_spec, h_spec, h_spec],
        out_shape=[jax.ShapeDtypeStruct(o_shape, BF16), jax.ShapeDtypeStruct(h0r.shape, F32),
                   jax.ShapeDtypeStruct(h0i.shape, F32)],
        scratch_shapes=[pltpu.VMEM((tl * tb, S5_LANES), F32), pltpu.VMEM((tl * tb, S5_LANES), F32)],
        compiler_params=_cparams(("parallel", "arbitrary")),
        name="s5",
    )(su, *params, h0r, h0i)
    return o.reshape(b * l, MIX_W), hr, hi


def _s5_params(p):
    a_re, a_im = p['s5_a_re'], p['s5_a_im']
    dt = jnp.exp(p['s5_log_dt'])[:, None]
    mag = jnp.exp(dt * a_re)
    ab_re, ab_im = mag * jnp.cos(dt * a_im), mag * jnp.sin(dt * a_im)
    den = a_re * a_re + a_im * a_im
    n_re, n_im = ab_re - 1.0, ab_im
    f_re = (n_re * a_re + n_im * a_im) / den
    f_im = (n_im * a_re - n_re * a_im) / den
    b_re, b_im = p['s5_b_re'], p['s5_b_im']
    bb_re = f_re[..., None] * b_re - f_im[..., None] * b_im
    bb_im = f_re[..., None] * b_im + f_im[..., None] * b_re
    gpt = S5_GROUPS // S5_TILES
    eye = jnp.eye(gpt, dtype=F32)

    def pack_b(bb):
        bt = bb.reshape(S5_TILES, gpt, S5_STATE, S5_GROUP)
        w = jnp.einsum('tgpm,gh->tgmhp', bt, eye)
        return w.reshape(S5_TILES, gpt * S5_GROUP, gpt * S5_STATE)

    def pack_c(cc):
        ct = cc.reshape(S5_TILES, gpt, S5_GROUP, S5_STATE)
        w = jnp.einsum('tgmp,gh->tgphm', ct, eye)
        return w.reshape(S5_TILES, gpt * S5_STATE, gpt * S5_GROUP)

    wbu = jnp.concatenate([pack_b(bb_re), pack_b(bb_im)], axis=2).astype(BF16)
    wc = jnp.concatenate([pack_c(p['s5_c_re']), -pack_c(p['s5_c_im'])], axis=1).astype(BF16)
    return (wbu, ab_re.reshape(1, S5_LANES), ab_im.reshape(1, S5_LANES), wc, p['s5_d'].reshape(1, MIX_W))


def _merge_kernel(x_ref, gla_ref, glb_ref, o0, o1, o2, o3, wb_ref, wo_ref, g_ref, out_ref):
    m = None
    per = GL_BLK // D_MODEL
    for n, o in enumerate((o0, o1, o2, o3)):
        t = _dot(o[...], wb_ref[n])
        gl_ref = (gla_ref, glb_ref)[n // per]
        t = _sigmoid(gl_ref[:, (n % per) * D_MODEL:(n % per + 1) * D_MODEL].astype(F32)) * t
        m = t if m is None else m + t
    mo = _dot(m.astype(BF16), wo_ref[...])
    out_ref[...] = x_ref[...] + _rms(mo, g_ref[...])


def _merge_call(x2, proj, branches, wb, wo, g, *, tm):
    t = x2.shape[0]
    row = lambda i: (i, 0)
    return pl.pallas_call(
        _merge_kernel,
        grid=(t // tm,),
        in_specs=[pl.BlockSpec((tm, D_MODEL), row)]
        + [pl.BlockSpec((tm, GL_BLK), lambda i, c=COL_GL // GL_BLK + k: (i, c)) for k in range(2)]
        + [pl.BlockSpec((tm, MIX_W), row)] * N_BRANCH
        + [_const_spec(wb.shape), _const_spec(wo.shape), _const_spec(g.shape)],
        out_specs=pl.BlockSpec((tm, D_MODEL), row),
        out_shape=jax.ShapeDtypeStruct((t, D_MODEL), F32),
        compiler_params=_cparams(("parallel",)),
        name="merge",
    )(x2, proj, proj, *branches, wb, wo, g)


def _ffn_kernel(x_ref, g1_ref, w1_ref, w2_ref, g2_ref, o_ref, h_ref, acc_ref):
    j = pl.program_id(1)

    @pl.when(j == 0)
    def _():
        h_ref[...] = _rms(x_ref[...], g1_ref[...]).astype(BF16)
        acc_ref[...] = jnp.zeros_like(acc_ref)

    u = jnp.maximum(_dot(h_ref[...], w1_ref[...]), 0.0)
    acc_ref[...] += _dot((u * u).astype(BF16), w2_ref[...])

    @pl.when(j == pl.num_programs(1) - 1)
    def _():
        o_ref[...] = x_ref[...] + _rms(acc_ref[...], g2_ref[...])


def _ffn_call(x2, g1, w1, w2, g2, *, tm, tf):
    t = x2.shape[0]
    return pl.pallas_call(
        _ffn_kernel,
        grid=(t // tm, D_FF // tf),
        in_specs=[
            pl.BlockSpec((tm, D_MODEL), lambda i, j: (i, 0)),
            pl.BlockSpec((1, D_MODEL), lambda i, j: (0, 0)),
            pl.BlockSpec((D_MODEL, tf), lambda i, j: (0, j)),
            pl.BlockSpec((tf, D_MODEL), lambda i, j: (j, 0)),
            pl.BlockSpec((1, D_MODEL), lambda i, j: (0, 0)),
        ],
        out_specs=pl.BlockSpec((tm, D_MODEL), lambda i, j: (i, 0)),
        out_shape=jax.ShapeDtypeStruct((t, D_MODEL), F32),
        scratch_shapes=[pltpu.VMEM((tm, D_MODEL), BF16), pltpu.VMEM((tm, D_MODEL), F32)],
        compiler_params=_cparams(("parallel", "arbitrary")),
        name="ffn",
    )(x2, g1, w1, w2, g2)


def _prep_layer(p, lb):
    w_in = p['w_in']
    o_sdt = 4 * MIX_W + MIX_W + SSD_CONV_DIM
    o_hq = o_sdt + SSD_HEADS
    w_main = jnp.concatenate([
        w_in[:, 0:o_sdt], w_in[:, o_hq:], jnp.repeat(w_in[:, o_sdt:o_hq], SSD_HEADDIM, axis=1)], axis=1).astype(BF16)
    assert w_main.shape[1] == PROJ_COLS and o_sdt == COL_HG

    def head_lanes(v):
        return jnp.repeat(v.astype(F32), SSD_HEADDIM).reshape(1, MIX_W)

    q = dict(
        w_main=w_main,
        g_pre_mix=p['g_pre_mix'].reshape(1, D_MODEL),
        g_post_mix=p['g_post_mix'].reshape(1, D_MODEL),
        g_pre_ffn=p['g_pre_ffn'].reshape(1, D_MODEL),
        g_post_ffn=p['g_post_ffn'].reshape(1, D_MODEL),
        ret_gn=p['ret_gn'],
        conv_w=p['ssd_conv_w'],
        conv_b=p['ssd_conv_b'].reshape(1, SSD_CONV_DIM),
        a_log=head_lanes(p['ssd_a_log']),
        dt_bias=head_lanes(p['ssd_dt_bias']),
        d_skip=head_lanes(p['ssd_d']),
        ssd_norm=p['ssd_norm'].reshape(1, MIX_W),
        hg_lb=lb.reshape(1, MIX_W),
        hg_norm=p['hg_norm'].reshape(1, HG_DK),
        s5=_s5_params(p),
        w_glu=p['s5_w_glu'].astype(BF16),
        w_branch=p['w_branch'].astype(BF16),
        w_out=p['w_out'].astype(BF16),
        w_ff1=p['w_ff1'].astype(BF16),
        w_ff2=p['w_ff2'].astype(BF16),
    )
    return q


def _layer(x3, states, acc, layer, q, pos, cfg):
    b, l, _ = x3.shape
    x2 = x3.reshape(b * l, D_MODEL)
    tb, tl, c = cfg['tb'], cfg['tl'], cfg['c']
    proj = _proj_call(x2, q['g_pre_mix'], q['w_main'], tm=cfg['tm_proj'], tn=cfg['tn'])
    st = {n: states[n][layer] for n in ('conv', 's5_re', 's5_im')}
    o_ret, s_ret = _ret_call(proj, pos, q['ret_gn'], states['ret'], acc['ret'], layer, b=b, l=l, tb=tb, tl=tl, sl=c,
                             unroll=cfg['u_ret'])
    o_ssd, conv_new, s_ssd = _ssd_call(proj, q['conv_w'], q['conv_b'], q['a_log'], q['dt_bias'], q['d_skip'],
                                       q['ssd_norm'], st['conv'], states['ssd'], acc['ssd'], layer,
                                       b=b, l=l, tb=tb, tl=tl, sl=c, unroll=cfg['u_ssd'])
    o_hg, s_hg = _hg_call(proj, q['hg_lb'], q['hg_norm'], states['hgrn'], acc['hgrn'], layer, b=b, l=l, tb=tb, tl=tl,
                          sl=c, unroll=cfg['u_hg'])
    wbu, are, aim, wc, d5 = q['s5']
    o_s5, s5_re, s5_im = _s5_call(proj, wbu, are, aim, wc, d5, q['w_glu'],
                                  st['s5_re'].reshape(b, S5_LANES), st['s5_im'].reshape(b, S5_LANES),
                                  b=b, l=l, tb=SUBLANE, tl=cfg['s5_tl'])
    x2 = _merge_call(x2, proj, (o_ret, o_ssd, o_hg, o_s5), q['w_branch'], q['w_out'], q['g_post_mix'], tm=cfg['tm_merge'])
    x2 = _ffn_call(x2, q['g_pre_ffn'], q['w_ff1'], q['w_ff2'], q['g_post_ffn'], tm=cfg['tm'], tf=cfg['tf'])
    def push(name, new):
        return new[None] if acc[name] is None else jnp.concatenate([acc[name], new[None]], axis=0)

    new = {'ret': s_ret, 'ssd': s_ssd, 'hgrn': s_hg, 'conv': push('conv', conv_new),
           's5_re': push('s5_re', s5_re.reshape(b, S5_GROUPS, S5_STATE)),
           's5_im': push('s5_im', s5_im.reshape(b, S5_GROUPS, S5_STATE))}
    return x2.reshape(b, l, D_MODEL), new


_NAMES = ('ret', 'ssd', 'conv', 'hgrn', 's5_re', 's5_im')


def _trunk(x, states, layers, pos, cfg):
    acc = {n: None for n in _NAMES}
    for layer, q in enumerate(layers):
        x, acc = _layer(x, states, acc, layer, q, pos, cfg)
    return x, acc


def _group_cfg(b, l):
    if l % CHUNK == 0:
        return dict(tb=1, tl=min(l, 512), c=CHUNK, tm=1024, tm_proj=2048, tn=1536, tm_merge=512, tf=1024,
                    s5_tl=min(l, 64), u_ret=4, u_ssd=8, u_hg=8)
    return dict(tb=SUBLANE, tl=l, c=l, tm=min(b * l, 1024), tm_proj=min(b * l, 1024), tn=1536,
                tm_merge=min(b * l, 512), tf=1024, s5_tl=l, u_ret=1, u_ssd=1, u_hg=1)


def kernel(x_prompt, x_sample, state_ret, state_ssd, state_conv, state_hgrn, state_s5_re, state_s5_im, g_pre_mix, g_post_mix, g_pre_ffn, g_post_ffn, w_in, ret_gn, ssd_conv_w, ssd_conv_b, ssd_a_log, ssd_dt_bias, ssd_d, ssd_norm, hg_lb_logits, hg_norm, s5_a_re, s5_a_im, s5_b_re, s5_b_im, s5_c_re, s5_c_im, s5_d, s5_log_dt, s5_w_glu, w_branch, w_out, w_ff1, w_ff2):
    params = {
        'g_pre_mix': g_pre_mix, 'g_post_mix': g_post_mix, 'g_pre_ffn': g_pre_ffn, 'g_post_ffn': g_post_ffn,
        'w_in': w_in, 'ret_gn': ret_gn, 'ssd_conv_w': ssd_conv_w, 'ssd_conv_b': ssd_conv_b,
        'ssd_a_log': ssd_a_log, 'ssd_dt_bias': ssd_dt_bias, 'ssd_d': ssd_d, 'ssd_norm': ssd_norm,
        'hg_norm': hg_norm, 's5_a_re': s5_a_re, 's5_a_im': s5_a_im, 's5_b_re': s5_b_re, 's5_b_im': s5_b_im,
        's5_c_re': s5_c_re, 's5_c_im': s5_c_im, 's5_d': s5_d, 's5_log_dt': s5_log_dt, 's5_w_glu': s5_w_glu,
        'w_branch': w_branch, 'w_out': w_out, 'w_ff1': w_ff1, 'w_ff2': w_ff2,
    }
    depth = w_in.shape[0]
    w = jax.nn.softmax(hg_lb_logits.astype(F32), axis=0)
    lbs = jnp.cumsum(w, axis=0) - w[0]
    layers = [_prep_layer({k: v[i] for k, v in params.items()}, lbs[i]) for i in range(depth)]

    bp, lp, _ = x_prompt.shape
    bs, ls, _ = x_sample.shape
    zero_states = {
        'ret': jnp.zeros((depth, bp) + state_ret.shape[2:], F32),
        'ssd': jnp.zeros((depth, bp) + state_ssd.shape[2:], F32),
        'conv': jnp.zeros((depth, bp) + state_conv.shape[2:], F32),
        'hgrn': jnp.zeros((depth, bp) + state_hgrn.shape[2:], F32),
        's5_re': jnp.zeros((depth, bp) + state_s5_re.shape[2:], F32),
        's5_im': jnp.zeros((depth, bp) + state_s5_im.shape[2:], F32),
    }
    sample_states = {'ret': state_ret, 'ssd': state_ssd, 'conv': state_conv, 'hgrn': state_hgrn,
                     's5_re': state_s5_re, 's5_im': state_s5_im}
    pos_p = np.arange(lp)
    pos_s = PAST_LEN + np.arange(ls)
    y_p, new_p = _trunk(x_prompt, zero_states, layers, pos_p, _group_cfg(bp, lp))
    y_s, new_s = _trunk(x_sample, sample_states, layers, pos_s, _group_cfg(bs, ls))
    return (y_p, y_s,
            new_p['ret'], new_s['ret'], new_p['ssd'], new_s['ssd'], new_p['conv'], new_s['conv'],
            new_p['hgrn'], new_s['hgrn'], new_p['s5_re'], new_s['s5_re'], new_p['s5_im'], new_s['s5_im'])
```

```python
import functools
import math

import jax
import jax.numpy as jnp
import numpy as np
from jax import lax
from jax.experimental import pallas as pl
from jax.experimental.pallas import tpu as pltpu

F32 = jnp.float32
BF16 = jnp.bfloat16

D_MODEL = 1024
DEPTH = 2
PAST_LEN = 16384
MIX_W = D_MODEL // 2
N_BRANCH = 4
RET_HEADS = 4
RET_DK = MIX_W // RET_HEADS
ROPE_BASE = 10000.0
SSD_HEADDIM = 64
SSD_HEADS = MIX_W // SSD_HEADDIM
SSD_GROUPS = 2
SSD_STATE = 128
SSD_CONV = 4
SSD_CONV_DIM = MIX_W + 2 * SSD_GROUPS * SSD_STATE
SSD_PAIRS = SSD_HEADS // 2
HG_HEADS = 4
HG_DK = MIX_W // HG_HEADS
S5_GROUP = 16
S5_GROUPS = MIX_W // S5_GROUP
S5_STATE = 64
S5_LANES = S5_GROUPS * S5_STATE
D_FF = 4 * D_MODEL
GATE_COLS = N_BRANCH * D_MODEL
CHUNK = 64
CR = 64
EPS = 1e-6

LANE = 128
SUBLANE = 8
VMEM_LIMIT = 48 * 1024 * 1024

COL_RET = 0
COL_SZ = COL_RET + 4 * MIX_W
COL_XS = COL_SZ + MIX_W
COL_BC = COL_XS + MIX_W
COL_HG = COL_BC + MIX_W
COL_SU = COL_HG + 4 * MIX_W
COL_GL = COL_SU + MIX_W
COL_DT = COL_GL + GATE_COLS
PROJ_COLS = COL_DT + MIX_W
GL_BLK = GATE_COLS // 2
assert COL_GL % GL_BLK == 0


def _sigmoid(x):
    return 1.0 / (1.0 + jnp.exp(-x))


def _silu(x):
    return x * _sigmoid(x)


def _softplus(x):
    return jnp.maximum(x, 0.0) + jnp.log1p(jnp.exp(-jnp.abs(x)))


def _gelu_tanh(x):
    c = math.sqrt(2.0 / math.pi)
    return 0.5 * x * (1.0 + jnp.tanh(c * (x + 0.044715 * (x * x * x))))


def _rms(x, g):
    ms = jnp.mean(x * x, axis=-1, keepdims=True)
    return x * lax.rsqrt(ms + EPS) * g


def _dot(a, b):
    return jnp.dot(a, b, preferred_element_type=F32)


def _dot_nt(a, b):
    return lax.dot_general(a, b, (((1,), (1,)), ((), ())), preferred_element_type=F32)


def _split3(a):
    hi = a.astype(BF16)
    r1 = a - hi.astype(F32)
    mid = r1.astype(BF16)
    lo = (r1 - mid.astype(F32)).astype(BF16)
    return hi, mid, lo


def _sel_l(m, a):
    hi, mid, lo = _split3(a)
    return (_dot(m, lo) + _dot(m, mid)) + _dot(m, hi)


def _sel2_l(m, a):
    hi = a.astype(BF16)
    mid = (a - hi.astype(F32)).astype(BF16)
    return _dot(m, mid) + _dot(m, hi)


def _sel_r(a, m):
    hi, mid, lo = _split3(a)
    return (_dot(lo, m) + _dot(mid, m)) + _dot(hi, m)


def _cparams(sem):
    return pltpu.CompilerParams(dimension_semantics=sem, vmem_limit_bytes=VMEM_LIMIT)


def _const_spec(shape):
    nd = len(shape)
    return pl.BlockSpec(shape, lambda *_: (0,) * nd)


def _state_io(s_all, prev, layer, tb):
    dims = tuple(s_all.shape[2:])
    zeros = (0,) * len(dims)
    in_spec = pl.BlockSpec((None, tb) + dims, lambda bi, li: (layer, bi) + zeros)
    if prev is None:
        nprev, prev, prev_spec = 0, s_all, in_spec
    else:
        nprev = prev.shape[0]
        prev_spec = pl.BlockSpec((nprev, tb) + dims, lambda bi, li: (0, bi) + zeros)
    out_spec = pl.BlockSpec((nprev + 1, tb) + dims, lambda bi, li: (0, bi) + zeros)
    out_shape = jax.ShapeDtypeStruct((nprev + 1, s_all.shape[1]) + dims, F32)
    return nprev, prev, in_spec, prev_spec, out_spec, out_shape


def _proj_kernel(x_ref, g_ref, w_ref, o_ref, h_ref):
    @pl.when(pl.program_id(1) == 0)
    def _():
        h_ref[...] = _rms(x_ref[...], g_ref[...]).astype(BF16)

    o_ref[...] = _dot(h_ref[...], w_ref[...]).astype(BF16)


def _proj_call(x2, g, w, *, tm, tn):
    t, d = x2.shape
    n = w.shape[1]
    return pl.pallas_call(
        _proj_kernel,
        grid=(t // tm, n // tn),
        in_specs=[
            pl.BlockSpec((tm, d), lambda i, j: (i, 0)),
            pl.BlockSpec((1, d), lambda i, j: (0, 0)),
            pl.BlockSpec((d, tn), lambda i, j: (0, j)),
        ],
        out_specs=pl.BlockSpec((tm, tn), lambda i, j: (i, j)),
        out_shape=jax.ShapeDtypeStruct((t, n), BF16),
        scratch_shapes=[pltpu.VMEM((tm, d), BF16)],
        compiler_params=_cparams(("parallel", "arbitrary")),
        name="in_proj",
    )(x2, g, w)


def _ret_consts(sl):
    nseq = CR // sl
    h = np.arange(RET_HEADS, dtype=np.float64)
    log_g = np.log(1.0 - 2.0 ** (-5.0 - h))
    idx = np.arange(sl, dtype=np.float64)
    diff = idx[:, None] - idx[None, :]
    dmat = np.where(diff >= 0, np.exp(log_g[:, None, None] * np.maximum(diff, 0.0)), 0.0)
    dmat = np.stack([np.kron(np.eye(nseq), d) for d in dmat])
    q_dec = np.tile(np.exp(log_g[:, None] * (idx[None, :] + 1.0)), (1, nseq))
    k_dec = np.tile(np.exp(log_g[:, None] * (sl - 1.0 - idx[None, :])), (1, nseq))
    chunk_dec = np.exp(log_g * sl)
    qd = np.broadcast_to(q_dec[:, :, None], (RET_HEADS, CR, LANE))
    kd = np.broadcast_to(k_dec[:, :, None], (RET_HEADS, CR, LANE))
    return (jnp.asarray(dmat, F32), jnp.asarray(qd, F32), jnp.asarray(kd, F32),
            tuple(float(v) for v in chunk_dec))


def _rope_tables(pos):
    half = RET_DK // 2
    inv = ROPE_BASE ** (-np.arange(half, dtype=np.float64) / half)
    ang = np.asarray(pos, np.float64)[:, None] * inv[None, :]
    cos, sin = np.cos(ang), np.sin(ang)
    return (jnp.asarray(np.concatenate([cos, cos], axis=1), F32),
            jnp.asarray(np.concatenate([-sin, sin], axis=1), F32))


def _ret_kernel(q_ref, k_ref, v_ref, g_ref, cos_ref, sin_ref, dmat_ref, qd_ref, kd_ref, gn_ref, s0_ref, prev_ref,
                o_ref, so_ref, *, tb, tl, sl, cdec, nprev, unroll):
    st_ref = so_ref.at[nprev]

    @pl.when(pl.program_id(1) == 0)
    def _():
        st_ref[...] = s0_ref[...]
        if nprev:
            so_ref[0:nprev] = prev_ref[...]

    nseq = CR // sl
    scale = RET_DK ** -0.5

    def body(pc, carry):
        r0 = pl.multiple_of(pc * CR, CR)
        t0 = r0 if nseq == 1 else 0
        cos = cos_ref[pl.ds(t0, CR), :]
        sin = sin_ref[pl.ds(t0, CR), :]
        heads = [slice(h * LANE, (h + 1) * LANE) for h in range(RET_HEADS)]
        qrs, krs = [], []
        for hs in heads:
            q = q_ref[pl.ds(r0, CR), hs].astype(F32)
            k = k_ref[pl.ds(r0, CR), hs].astype(F32)
            qrs.append(q * cos + pltpu.roll(q, LANE // 2, axis=1) * sin)
            krs.append((k * cos + pltpu.roll(k, LANE // 2, axis=1) * sin) * scale)
        vs = [v_ref[pl.ds(r0, CR), hs] for hs in heads]
        atts = [(_dot_nt(qrs[h].astype(BF16), krs[h].astype(BF16)) * dmat_ref[h]).astype(BF16)
                for h in range(RET_HEADS)]
        outs = []
        for h in range(RET_HEADS):
            o = _dot(atts[h], vs[h])
            qd = qrs[h] * qd_ref[h]
            kd = krs[h] * kd_ref[h]
            ois = []
            for s in range(nseq):
                seq = pc * nseq + s if nseq > 1 else 0
                rs = slice(s * sl, (s + 1) * sl)
                st = st_ref[seq, h]
                ois.append(_dot(qd[rs].astype(BF16), st.astype(BF16)))
                v_s = vs[h] if nseq == 1 else vs[h].astype(F32)[rs].astype(BF16)
                st_ref[seq, h] = st * cdec[h] + _dot(kd[rs].T.astype(BF16), v_s)
            outs.append(o + (ois[0] if nseq == 1 else jnp.concatenate(ois, axis=0)))
        for h, hs in enumerate(heads):
            o = outs[h]
            mu = jnp.mean(o, axis=-1, keepdims=True)
            oc = o - mu
            var = jnp.mean(oc * oc, axis=-1, keepdims=True)
            on = oc * lax.rsqrt(var + EPS) * gn_ref[h:h + 1, :]
            o_ref[pl.ds(r0, CR), hs] = (on * _silu(g_ref[pl.ds(r0, CR), hs].astype(F32))).astype(BF16)
        return carry

    lax.fori_loop(0, tb * tl // CR, body, 0, unroll=unroll)


def _seq_grid(b, l, tb, tl):
    assert b % tb == 0 and l % tl == 0 and (tb == 1 or tl == l)
    nl = l // tl
    rows = tb * tl

    def row_map(col_blk):
        return lambda bi, li: (bi * nl + li, col_blk)

    return (b // tb, nl), rows, row_map


def _ret_call(proj, pos, gn, s_all, prev, layer, *, b, l, tb, tl, sl, unroll):
    grid, rows, row_map = _seq_grid(b, l, tb, tl)
    assert rows % CR == 0 and CR % sl == 0 and (sl == CR and tb == 1 or sl == tl)
    dmat, qd, kd, cdec = _ret_consts(sl)
    cos, sin = _rope_tables(pos)
    if sl < CR:
        cos, sin = jnp.tile(cos, (CR // sl, 1)), jnp.tile(sin, (CR // sl, 1))
        tab_spec = _const_spec(cos.shape)
    else:
        tab_spec = pl.BlockSpec((tl, LANE), lambda bi, li: (li, 0))
    cb = COL_RET // MIX_W
    nprev, prev, s_spec, prev_spec, so_spec, so_shape = _state_io(s_all, prev, layer, tb)
    return pl.pallas_call(
        functools.partial(_ret_kernel, tb=tb, tl=tl, sl=sl, cdec=cdec, nprev=nprev, unroll=unroll),
        grid=grid,
        in_specs=[pl.BlockSpec((rows, MIX_W), row_map(cb + i)) for i in range(4)] + [
            tab_spec, tab_spec,
            _const_spec(dmat.shape), _const_spec(qd.shape), _const_spec(kd.shape), _const_spec(gn.shape),
            s_spec, prev_spec,
        ],
        out_specs=[pl.BlockSpec((rows, MIX_W), row_map(0)), so_spec],
        out_shape=[jax.ShapeDtypeStruct((b * l, MIX_W), BF16), so_shape],
        compiler_params=_cparams(("parallel", "arbitrary")),
        name="retention",
    )(proj, proj, proj, proj, cos, sin, dmat, qd, kd, gn, s_all, prev)


def _ssd_consts(sl):
    nseq = CR // sl
    tril = np.kron(np.eye(nseq), np.tril(np.ones((sl, sl))))
    sumall = np.kron(np.eye(nseq), np.ones((sl, sl)))
    ones = np.ones((CR, CR))
    i = np.arange(CR)[:, None]
    j = np.arange(MIX_W)[None, :] % SSD_HEADDIM
    eye = (i == j).astype(np.float64)
    mask = ((i // sl == j // sl) & (i >= j)).astype(np.float64)
    return (jnp.asarray(np.concatenate([tril, sumall], axis=0), BF16), jnp.asarray(ones, BF16),
            jnp.asarray(eye, F32), jnp.asarray(mask, F32))


def _ssd_kernel(z_ref, xs_ref, bc_ref, dtx_ref, cw_ref, cb_ref, alog_ref, dtb_ref, dsk_ref, nw_ref,
                mcum_ref, ones_ref, eye_ref, mask_ref, cs0_ref, s0_ref, prev_ref,
                o_ref, cs_ref, so_ref, xpad, st, *, tb, tl, sl, nprev, unroll):
    li = pl.program_id(1)
    nl = pl.num_programs(1)
    nseq = CR // sl
    kc = SSD_CONV - 1
    off = SUBLANE - kc
    grp = 2 * SUBLANE

    @pl.when(li == 0)
    def _():
        for s in range(tb):
            for p in range(SSD_PAIRS):
                st[s, p] = s0_ref[s, p].T
        if nprev:
            so_ref[0:nprev] = prev_ref[...]
        if nseq == 1:
            xpad[0:SUBLANE, :] = jnp.zeros((SUBLANE, SSD_CONV_DIM), F32)
            xpad[off:SUBLANE, :] = cs0_ref[0]

    if nseq == 1:
        xpad[SUBLANE:SUBLANE + tl, 0:MIX_W] = xs_ref[...].astype(F32)
        xpad[SUBLANE:SUBLANE + tl, MIX_W:2 * MIX_W] = bc_ref[...].astype(F32)
    else:
        xs_new = xs_ref[...].astype(F32)
        bc_new = bc_ref[...].astype(F32)
        for s in range(tb):
            xpad[s * grp:s * grp + SUBLANE, :] = jnp.zeros((SUBLANE, SSD_CONV_DIM), F32)
            xpad[s * grp + off:s * grp + SUBLANE, :] = cs0_ref[s]
            xpad[s * grp + SUBLANE:(s + 1) * grp, 0:MIX_W] = xs_new[s * sl:(s + 1) * sl, :]
            xpad[s * grp + SUBLANE:(s + 1) * grp, MIX_W:2 * MIX_W] = bc_new[s * sl:(s + 1) * sl, :]
            cs_ref[s] = xpad[(s + 1) * grp - kc:(s + 1) * grp, :]

    a_neg = -jnp.exp(alog_ref[...])
    half = lax.broadcasted_iota(jnp.int32, (CR, LANE), 1) < SSD_HEADDIM

    def body(pc, carry):
        r0 = pl.multiple_of(pc * CR, CR)
        if nseq == 1:
            win = xpad[pl.ds(r0, CR + SUBLANE), :]
            take = lambda a: a[SUBLANE:, :]
        else:
            win = xpad[pl.ds(pl.multiple_of(pc * nseq * grp, nseq * grp), nseq * grp), :]
            take = lambda a: a.reshape(nseq, grp, SSD_CONV_DIM)[:, SUBLANE:, :].reshape(CR, SSD_CONV_DIM)
        acc = cb_ref[...] + take(pltpu.roll(win, kc, axis=0)) * cw_ref[0:1, :]
        for j in range(1, kc):
            acc = acc + take(pltpu.roll(win, kc - j, axis=0)) * cw_ref[j:j + 1, :]
        xc = _silu(acc + take(win) * cw_ref[kc:kc + 1, :])
        xs = xc[:, 0:MIX_W]
        bm = xc[:, MIX_W:MIX_W + SSD_GROUPS * SSD_STATE]
        cm = xc[:, MIX_W + SSD_GROUPS * SSD_STATE:2 * MIX_W]

        dtx = _softplus(dtx_ref[pl.ds(r0, CR), :].astype(F32) + dtb_ref[...])
        both = _sel_l(mcum_ref[...], dtx * a_neg)
        cum, clast = both[0:CR], both[CR:2 * CR]
        ecum = jnp.exp(cum)
        dend = jnp.exp(clast - cum)
        elast = jnp.exp(clast)
        rr = _sel_l(ones_ref[...], cum * eye_ref[...])
        msk = mask_ref[...]
        lmat = jnp.exp(jnp.where(msk > 0, cum - rr, 0.0)) * msk
        zz = z_ref[pl.ds(r0, CR), :].astype(F32)
        npp = SSD_PAIRS // SSD_GROUPS
        bm_gs = [bm[:, g * SSD_STATE:(g + 1) * SSD_STATE] for g in range(SSD_GROUPS)]
        cm_gs = [cm[:, g * SSD_STATE:(g + 1) * SSD_STATE] for g in range(SSD_GROUPS)]
        cm_gbs = [c_.astype(BF16) for c_ in cm_gs]
        cb2s = [_dot_nt(cm_gbs[g], jnp.concatenate([bm_gs[g], bm_gs[g]], axis=0).astype(BF16))
                for g in range(SSD_GROUPS)]
        pss = [slice(p * LANE, (p + 1) * LANE) for p in range(SSD_PAIRS)]
        xdts = [xs[:, ps] * dtx[:, ps] for ps in pss]
        xsts = [jnp.concatenate([jnp.where(half, x_, 0.0), jnp.where(half, 0.0, x_)], axis=0).astype(BF16)
                for x_ in xdts]
        ys = [_dot((cb2s[p // npp] * lmat[:, pss[p]]).astype(BF16), xsts[p]) for p in range(SSD_PAIRS)]
        xds = [xdts[p] * dend[:, pss[p]] for p in range(SSD_PAIRS)]
        yis = []
        for p in range(SSD_PAIRS):
            g = p // npp
            parts = []
            for s in range(nseq):
                seq = pc * nseq + s if nseq > 1 else 0
                if nseq == 1:
                    cm_s, bm_s, xd_s = cm_gbs[g], bm_gs[g], xds[p]
                else:
                    rs = slice(s * sl, (s + 1) * sl)
                    cm_s, bm_s, xd_s = cm_gs[g][rs].astype(BF16), bm_gs[g][rs], xds[p][rs]
                stp = st[seq, p]
                parts.append(_dot(cm_s, stp.astype(BF16)))
                st[seq, p] = stp * elast[s * sl:s * sl + 1, pss[p]] + _dot(bm_s.T.astype(BF16), xd_s.astype(BF16))
            yis.append(parts[0] if nseq == 1 else jnp.concatenate(parts, axis=0))
        gw = SSD_HEADDIM * SSD_HEADS // SSD_GROUPS
        for g in range(SSD_GROUPS):
            yg = jnp.concatenate([ys[p] + yis[p] * ecum[:, pss[p]] + xs[:, pss[p]] * dsk_ref[:, pss[p]]
                                  for p in range(g * npp, (g + 1) * npp)], axis=1)
            zs = slice(g * gw, (g + 1) * gw)
            o_ref[pl.ds(r0, CR), zs] = _rms(yg * _silu(zz[:, zs]), nw_ref[:, zs]).astype(BF16)
        return carry

    lax.fori_loop(0, tb * tl // CR, body, 0, unroll=unroll)

    if nseq == 1:
        xpad[0:SUBLANE, :] = xpad[tl:tl + SUBLANE, :]

    @pl.when(li == nl - 1)
    def _():
        if nseq == 1:
            cs_ref[0] = xpad[tl + off:tl + SUBLANE, :]
        for s in range(tb):
            for p in range(SSD_PAIRS):
                so_ref[nprev, s, p] = st[s, p].T


def _ssd_call(proj, cw, cb, alog, dtb, dsk, nw, cs0, s_all, prev, layer, *, b, l, tb, tl, sl, unroll):
    grid, rows, row_map = _seq_grid(b, l, tb, tl)
    assert rows % CR == 0 and CR % sl == 0 and (sl == CR and tb == 1 or sl == tl == SUBLANE)
    consts = _ssd_consts(sl)
    pair = lambda a: a.reshape(a.shape[:2] + (SSD_PAIRS, LANE, SSD_STATE))
    s_pairs = pair(s_all)
    nprev, prev, s_spec, prev_spec, so_spec, so_shape = _state_io(s_pairs, None if prev is None else pair(prev),
                                                                  layer, tb)
    cs_spec = pl.BlockSpec((tb, SSD_CONV - 1, SSD_CONV_DIM), lambda bi, li: (bi, 0, 0))
    params = (cw, cb, alog, dtb, dsk, nw)
    xpad_rows = SUBLANE + tl if sl == CR else tb * 2 * SUBLANE
    o, cs, so = pl.pallas_call(
        functools.partial(_ssd_kernel, tb=tb, tl=tl, sl=sl, nprev=nprev, unroll=unroll),
        grid=grid,
        in_specs=[pl.BlockSpec((rows, MIX_W), row_map(cb_)) for cb_ in
                  (COL_SZ // MIX_W, COL_XS // MIX_W, COL_BC // MIX_W, COL_DT // MIX_W)]
        + [_const_spec(a.shape) for a in params + consts] + [cs_spec, s_spec, prev_spec],
        out_specs=[pl.BlockSpec((rows, MIX_W), row_map(0)), cs_spec, so_spec],
        out_shape=[jax.ShapeDtypeStruct((b * l, MIX_W), BF16), jax.ShapeDtypeStruct(cs0.shape, F32), so_shape],
        scratch_shapes=[pltpu.VMEM((xpad_rows, SSD_CONV_DIM), F32),
                        pltpu.VMEM((tb, SSD_PAIRS, SSD_STATE, LANE), F32)],
        compiler_params=_cparams(("parallel", "arbitrary")),
        name="ssd",
    )(proj, proj, proj, proj, *params, *consts, cs0, s_pairs, prev)
    return o, cs, so.reshape((nprev + 1,) + s_all.shape[1:])


def _hg_levels(sl):
    ms = []
    m = sl // 2
    while m >= 1:
        ms.append(m)
        m //= 2
    return ms


def _hg_consts(sl):
    nseq = CR // sl
    tril = np.kron(np.eye(nseq), np.tril(np.ones((sl, sl))))
    sumall = np.kron(np.eye(nseq), np.ones((sl, sl)))
    mats = []
    masks = []
    i = np.arange(CR)
    for m in _hg_levels(sl):
        ref = (i // (2 * m)) * (2 * m) + m - 1
        if m > 1:
            mats.append(np.abs(tril - tril[ref]))
        same = (i[:, None] // (2 * m)) == (i[None, :] // (2 * m))
        qside = ((i // m) % 2 == 1)[:, None]
        kside = ((i // m) % 2 == 0)[None, :]
        masks.append((same & qside & kside).astype(np.float64))
    return (jnp.asarray(np.concatenate([tril, sumall], axis=0), BF16), jnp.asarray(np.concatenate(mats, axis=0), BF16),
            jnp.asarray(np.stack(masks), F32))


def _hg_kernel(q_ref, f_ref, i_ref, g_ref, lb_ref, nw_ref, mcum_ref, mlev_ref, mask_ref, s0_ref, prev_ref,
               o_ref, so_ref, st, qe_s, ke_s, qg_s, kd_s, el_s, dv_s, *, tb, tl, sl, nprev, unroll):
    li = pl.program_id(1)
    nl = pl.num_programs(1)

    @pl.when(li == 0)
    def _():
        for s in range(tb):
            for h in range(HG_HEADS):
                st[s, h] = s0_ref[s, h].T
        if nprev:
            so_ref[0:nprev] = prev_ref[...]

    nseq = CR // sl
    levels = _hg_levels(sl)

    nlev = len(levels)
    odd_row = (lax.broadcasted_iota(jnp.int32, (CR, MIX_W), 0) & 1) == 1

    def stage_a(pc, carry):
        r0 = pl.multiple_of(pc * CR, CR)
        lb = lb_ref[...]
        q = _silu(q_ref[pl.ds(r0, CR), :].astype(F32))
        sig = _sigmoid(f_ref[pl.ds(r0, CR), :].astype(F32))
        fdec = lb + (1.0 - lb) * sig
        logf = jnp.log(fdec)
        k = (1.0 - lb) * (1.0 - sig)
        v = i_ref[pl.ds(r0, CR), :].astype(F32)
        cums = _sel_l(mcum_ref[...], logf)
        gc = cums[0:CR]
        glast = cums[CR:2 * CR]
        if nlev > 1:
            dl = _sel2_l(mlev_ref[...], logf)
        qb, kb = q.astype(BF16), k.astype(BF16)
        for n in range(nlev):
            e = jnp.exp(dl[n * CR:(n + 1) * CR]) if n < nlev - 1 else jnp.where(odd_row, fdec, 1.0)
            e = e.astype(BF16)
            qe_s[pc, n] = qb * e
            ke_s[pc, n] = kb * e
        qg_s[pc] = (q * jnp.exp(gc)).astype(BF16)
        kd_s[pc] = (k * jnp.exp(glast - gc)).astype(BF16)
        el_s[pc] = jnp.exp(glast)
        qk = q * k
        dv_s[pc] = jnp.concatenate(
            [jnp.sum(qk[:, h * LANE:(h + 1) * LANE], axis=-1, keepdims=True) * v[:, h * LANE:(h + 1) * LANE]
             for h in range(HG_HEADS)], axis=1)
        return carry

    def stage_b(pc, carry):
        r0 = pl.multiple_of(pc * CR, CR)
        heads = [slice(h * LANE, (h + 1) * LANE) for h in range(HG_HEADS)]
        vs = [i_ref[pl.ds(r0, CR), hs] for hs in heads]
        atts = []
        for hs in heads:
            att = None
            for n in range(nlev):
                a = _dot_nt(qe_s[pc, n, :, hs], ke_s[pc, n, :, hs]) * mask_ref[n]
                att = a if att is None else att + a
            atts.append(att.astype(BF16))
        outs = []
        for h, hs in enumerate(heads):
            o = _dot(atts[h], vs[h]) + dv_s[pc, :, hs]
            ois = []
            for s in range(nseq):
                seq = pc * nseq + s if nseq > 1 else 0
                rs = slice(s * sl, (s + 1) * sl)
                stt = st[seq, h]
                if nseq == 1:
                    qg_r, kd_r, v_r = qg_s[pc, :, hs], kd_s[pc, :, hs], vs[h].astype(F32)
                else:
                    qg_r = qg_s[pc, :, hs].astype(F32)[rs].astype(BF16)
                    kd_r = kd_s[pc, :, hs].astype(F32)[rs].astype(BF16)
                    v_r = vs[h].astype(F32)[rs]
                ois.append(_dot_nt(qg_r, stt.astype(BF16)))
                st[seq, h] = stt * el_s[pc, s * sl:s * sl + 1, hs] + _dot(v_r.T.astype(BF16), kd_r)
            outs.append(o + (ois[0] if nseq == 1 else jnp.concatenate(ois, axis=0)))
        for h, hs in enumerate(heads):
            gate = _sigmoid(g_ref[pl.ds(r0, CR), hs].astype(F32))
            o_ref[pl.ds(r0, CR), hs] = (_rms(outs[h], nw_ref[...]) * gate).astype(BF16)
        return carry

    ngrp = tb * tl // CR
    lax.fori_loop(0, ngrp, stage_a, 0, unroll=unroll)
    lax.fori_loop(0, ngrp, stage_b, 0, unroll=unroll)

    @pl.when(li == nl - 1)
    def _():
        for s in range(tb):
            for h in range(HG_HEADS):
                so_ref[nprev, s, h] = st[s, h].T


def _hg_call(proj, lb, nw, s_all, prev, layer, *, b, l, tb, tl, sl, unroll):
    grid, rows, row_map = _seq_grid(b, l, tb, tl)
    assert rows % CR == 0 and CR % sl == 0 and (sl == CR and tb == 1 or sl == tl)
    consts = _hg_consts(sl)
    nlev = len(_hg_levels(sl))
    cb = COL_HG // MIX_W
    nprev, prev, s_spec, prev_spec, so_spec, so_shape = _state_io(s_all, prev, layer, tb)
    stage = lambda dt, *lead: pltpu.VMEM((rows // CR,) + lead + (CR, MIX_W), dt)
    return pl.pallas_call(
        functools.partial(_hg_kernel, tb=tb, tl=tl, sl=sl, nprev=nprev, unroll=unroll),
        grid=grid,
        in_specs=[pl.BlockSpec((rows, MIX_W), row_map(cb + i)) for i in range(4)]
        + [_const_spec(a.shape) for a in (lb, nw) + consts] + [s_spec, prev_spec],
        out_specs=[pl.BlockSpec((rows, MIX_W), row_map(0)), so_spec],
        out_shape=[jax.ShapeDtypeStruct((b * l, MIX_W), BF16), so_shape],
        scratch_shapes=[pltpu.VMEM((tb, HG_HEADS, HG_DK, HG_DK), F32), stage(BF16, nlev), stage(BF16, nlev),
                        stage(BF16), stage(BF16), stage(F32), stage(F32)],
        compiler_params=_cparams(("parallel", "arbitrary")),
        name="hgrn2",
    )(proj, proj, proj, proj, lb, nw, *consts, s_all, prev)


S5_TILES = MIX_W // LANE
S5_TLANES = S5_LANES // S5_TILES


def _s5_perm(tb, tl):
    rows = tb * tl
    p = np.zeros((rows, rows))
    t, s = np.meshgrid(np.arange(tl), np.arange(tb), indexing='ij')
    p[(s * tl + t).ravel(), (t * tb + s).ravel()] = 1.0
    return jnp.asarray(p, BF16), jnp.asarray(p.T, BF16)


def _s5_kernel(su_ref, perm_ref, permt_ref, wbu_ref, are_ref, aim_ref, wc_ref, d_ref, wglu_ref, h0r_ref, h0i_ref,
               o_ref, hr_ref, hi_ref, bur, bui, *, tb, tl):
    @pl.when(pl.program_id(1) == 0)
    def _():
        hr_ref[...] = h0r_ref[...]
        hi_ref[...] = h0i_ref[...]

    rows = tl * tb
    u = _dot(permt_ref[...], su_ref[...].reshape(rows, MIX_W))
    tiles = [slice(t * S5_TLANES, (t + 1) * S5_TLANES) for t in range(S5_TILES)]

    def bu_tile(t):
        bu = _dot(u[:, t * LANE:(t + 1) * LANE].astype(BF16), wbu_ref[t])
        bur[:, tiles[t]] = bu[:, :S5_TLANES]
        bui[:, tiles[t]] = bu[:, S5_TLANES:]

    def scan_tile(t):
        ls = tiles[t]
        ar = jnp.broadcast_to(are_ref[:, ls], (tb, S5_TLANES))
        ai = jnp.broadcast_to(aim_ref[:, ls], (tb, S5_TLANES))
        hr, hi = hr_ref[:, ls], hi_ref[:, ls]
        for i in range(tl):
            rs = slice(i * tb, (i + 1) * tb)
            hr, hi = (ar * hr - ai * hi) + bur[rs, ls], (ar * hi + ai * hr) + bui[rs, ls]
            bur[rs, ls] = hr
            bui[rs, ls] = hi
        hr_ref[:, ls] = hr
        hi_ref[:, ls] = hi

    def y_tile(t):
        hcat = jnp.concatenate([bur[:, tiles[t]], bui[:, tiles[t]]], axis=1).astype(BF16)
        cs = slice(t * LANE, (t + 1) * LANE)
        return _dot(hcat, wc_ref[t]) + d_ref[:, cs] * u[:, cs]

    ys = [None] * S5_TILES
    bu_tile(0)
    for t in range(S5_TILES):
        if t + 1 < S5_TILES:
            bu_tile(t + 1)
        scan_tile(t)
        if t >= 1:
            ys[t - 1] = y_tile(t - 1)
    ys[S5_TILES - 1] = y_tile(S5_TILES - 1)
    z = _gelu_tanh(jnp.concatenate(ys, axis=1))
    out = z * _sigmoid(_dot(z.astype(BF16), wglu_ref[...]))
    o_ref[...] = _dot(perm_ref[...], out.astype(BF16)).astype(BF16).reshape(o_ref.shape)


def _s5_call(proj, wbu, are, aim, wc, d, wglu, h0r, h0i, *, b, l, tb, tl):
    assert l % tl == 0 and b % tb == 0 and tb == SUBLANE
    perm, permt = _s5_perm(tb, tl)
    h_spec = pl.BlockSpec((tb, S5_LANES), lambda bi, li: (bi, 0))
    params = (perm, permt, wbu, are, aim, wc, d, wglu)
    cb = COL_SU // MIX_W
    if tl == l:
        su, su_spec = proj, pl.BlockSpec((tb * tl, MIX_W), lambda bi, li: (bi, cb))
        o_shape, o_spec = (b * l, MIX_W), pl.BlockSpec((tb * tl, MIX_W), lambda bi, li: (bi, 0))
    else:
        su, su_spec = proj.reshape(b, l, -1), pl.BlockSpec((tb, tl, MIX_W), lambda bi, li: (bi, li, cb))
        o_shape, o_spec = (b, l, MIX_W), pl.BlockSpec((tb, tl, MIX_W), lambda bi, li: (bi, li, 0))
    o, hr, hi = pl.pallas_call(
        functools.partial(_s5_kernel, tb=tb, tl=tl),
        grid=(b // tb, l // tl),
        in_specs=[su_spec] + [_const_spec(a.shape) for a in params] + [h_spec, h_spec],
        out_specs=[o_spec, h_spec, h_spec],
        out_shape=[jax.ShapeDtypeStruct(o_shape, BF16), jax.ShapeDtypeStruct(h0r.shape, F32),
                   jax.ShapeDtypeStruct(h0i.shape, F32)],
        scratch_shapes=[pltpu.VMEM((tl * tb, S5_LANES), F32), pltpu.VMEM((tl * tb, S5_LANES), F32)],
        compiler_params=_cparams(("parallel", "arbitrary")),
        name="s5",
    )(su, *params, h0r, h0i)
    return o.reshape(b * l, MIX_W), hr, hi


def _s5_params(p):
    a_re, a_im = p['s5_a_re'], p['s5_a_im']
    dt = jnp.exp(p['s5_log_dt'])[:, None]
    mag = jnp.exp(dt * a_re)
    ab_re, ab_im = mag * jnp.cos(dt * a_im), mag * jnp.sin(dt * a_im)
    den = a_re * a_re + a_im * a_im
    n_re, n_im = ab_re - 1.0, ab_im
    f_re = (n_re * a_re + n_im * a_im) / den
    f_im = (n_im * a_re - n_re * a_im) / den
    b_re, b_im = p['s5_b_re'], p['s5_b_im']
    bb_re = f_re[..., None] * b_re - f_im[..., None] * b_im
    bb_im = f_re[..., None] * b_im + f_im[..., None] * b_re
    gpt = S5_GROUPS // S5_TILES
    eye = jnp.eye(gpt, dtype=F32)

    def pack_b(bb):
        bt = bb.reshape(S5_TILES, gpt, S5_STATE, S5_GROUP)
        w = jnp.einsum('tgpm,gh->tgmhp', bt, eye)
        return w.reshape(S5_TILES, gpt * S5_GROUP, gpt * S5_STATE)

    def pack_c(cc):
        ct = cc.reshape(S5_TILES, gpt, S5_GROUP, S5_STATE)
        w = jnp.einsum('tgmp,gh->tgphm', ct, eye)
        return w.reshape(S5_TILES, gpt * S5_STATE, gpt * S5_GROUP)

    wbu = jnp.concatenate([pack_b(bb_re), pack_b(bb_im)], axis=2).astype(BF16)
    wc = jnp.concatenate([pack_c(p['s5_c_re']), -pack_c(p['s5_c_im'])], axis=1).astype(BF16)
    return (wbu, ab_re.reshape(1, S5_LANES), ab_im.reshape(1, S5_LANES), wc, p['s5_d'].reshape(1, MIX_W))


def _merge_kernel(x_ref, gla_ref, glb_ref, o0, o1, o2, o3, wb_ref, wo_ref, g_ref, out_ref):
    m = None
    per = GL_BLK // D_MODEL
    for n, o in enumerate((o0, o1, o2, o3)):
        t = _dot(o[...], wb_ref[n])
        gl_ref = (gla_ref, glb_ref)[n // per]
        t = _sigmoid(gl_ref[:, (n % per) * D_MODEL:(n % per + 1) * D_MODEL].astype(F32)) * t
        m = t if m is None else m + t
    mo = _dot(m.astype(BF16), wo_ref[...])
    out_ref[...] = x_ref[...] + _rms(mo, g_ref[...])


def _merge_call(x2, proj, branches, wb, wo, g, *, tm):
    t = x2.shape[0]
    row = lambda i: (i, 0)
    return pl.pallas_call(
        _merge_kernel,
        grid=(t // tm,),
        in_specs=[pl.BlockSpec((tm, D_MODEL), row)]
        + [pl.BlockSpec((tm, GL_BLK), lambda i, c=COL_GL // GL_BLK + k: (i, c)) for k in range(2)]
        + [pl.BlockSpec((tm, MIX_W), row)] * N_BRANCH
        + [_const_spec(wb.shape), _const_spec(wo.shape), _const_spec(g.shape)],
        out_specs=pl.BlockSpec((tm, D_MODEL), row),
        out_shape=jax.ShapeDtypeStruct((t, D_MODEL), F32),
        compiler_params=_cparams(("parallel",)),
        name="merge",
    )(x2, proj, proj, *branches, wb, wo, g)


def _ffn_kernel(x_ref, g1_ref, w1_ref, w2_ref, g2_ref, o_ref, h_ref, acc_ref):
    j = pl.program_id(1)

    @pl.when(j == 0)
    def _():
        h_ref[...] = _rms(x_ref[...], g1_ref[...]).astype(BF16)
        acc_ref[...] = jnp.zeros_like(acc_ref)

    u = jnp.maximum(_dot(h_ref[...], w1_ref[...]), 0.0)
    acc_ref[...] += _dot((u * u).astype(BF16), w2_ref[...])

    @pl.when(j == pl.num_programs(1) - 1)
    def _():
        o_ref[...] = x_ref[...] + _rms(acc_ref[...], g2_ref[...])


def _ffn_call(x2, g1, w1, w2, g2, *, tm, tf):
    t = x2.shape[0]
    return pl.pallas_call(
        _ffn_kernel,
        grid=(t // tm, D_FF // tf),
        in_specs=[
            pl.BlockSpec((tm, D_MODEL), lambda i, j: (i, 0)),
            pl.BlockSpec((1, D_MODEL), lambda i, j: (0, 0)),
            pl.BlockSpec((D_MODEL, tf), lambda i, j: (0, j)),
            pl.BlockSpec((tf, D_MODEL), lambda i, j: (j, 0)),
            pl.BlockSpec((1, D_MODEL), lambda i, j: (0, 0)),
        ],
        out_specs=pl.BlockSpec((tm, D_MODEL), lambda i, j: (i, 0)),
        out_shape=jax.ShapeDtypeStruct((t, D_MODEL), F32),
        scratch_shapes=[pltpu.VMEM((tm, D_MODEL), BF16), pltpu.VMEM((tm, D_MODEL), F32)],
        compiler_params=_cparams(("parallel", "arbitrary")),
        name="ffn",
    )(x2, g1, w1, w2, g2)


W_TILE = MIX_W


def _wprep_kernel(a_ref, b_ref, tail_ref, e_ref, o_ref, *, first_moved, last_moved, dt_tile):
    j = pl.program_id(0)

    def moved(nxt):
        cat = jnp.concatenate([a_ref[...], nxt], axis=1)
        o_ref[...] = cat[:, SSD_HEADS:SSD_HEADS + W_TILE].astype(BF16)

    @pl.when(j < first_moved)
    def _():
        o_ref[...] = a_ref[...].astype(BF16)

    @pl.when((j >= first_moved) & (j < last_moved))
    def _():
        moved(b_ref[:, :LANE])

    @pl.when(j == last_moved)
    def _():
        moved(tail_ref[...])

    @pl.when(j == dt_tile)
    def _():
        o_ref[...] = _dot(a_ref[:, :LANE].astype(BF16), e_ref[...]).astype(BF16)


def _wprep_call(w_in_all, layer):
    _, d, cols = w_in_all.shape
    o_sdt = 4 * MIX_W + MIX_W + SSD_CONV_DIM
    assert o_sdt == COL_HG and o_sdt % W_TILE == 0 and (cols - SSD_HEADS) % W_TILE == 0
    first_moved = o_sdt // W_TILE
    dt_tile = (cols - SSD_HEADS) // W_TILE
    last_moved = dt_tile - 1
    assert (dt_tile + 1) * W_TILE == PROJ_COLS
    tail = jnp.pad(w_in_all[layer, :, dt_tile * W_TILE:], ((0, 0), (0, LANE - SSD_HEADS)))
    e = np.zeros((LANE, MIX_W))
    for h in range(SSD_HEADS):
        e[h, h * SSD_HEADDIM:(h + 1) * SSD_HEADDIM] = 1.0
    e = jnp.asarray(e, BF16)
    a_map = lambda j: (layer, 0, jnp.where(j == dt_tile, first_moved, j))
    b_map = lambda j: (layer, 0, jnp.minimum(j + 1, last_moved))
    return pl.pallas_call(
        functools.partial(_wprep_kernel, first_moved=first_moved, last_moved=last_moved, dt_tile=dt_tile),
        grid=(dt_tile + 1,),
        in_specs=[pl.BlockSpec((None, d, W_TILE), a_map), pl.BlockSpec((None, d, W_TILE), b_map),
                  _const_spec(tail.shape), _const_spec(e.shape)],
        out_specs=pl.BlockSpec((d, W_TILE), lambda j: (0, j)),
        out_shape=jax.ShapeDtypeStruct((d, PROJ_COLS), BF16),
        compiler_params=_cparams(("arbitrary",)),
        name="w_in_layout",
    )(w_in_all, w_in_all, tail, e)


def _prep_layer(p, lb, w_main):

    def head_lanes(v):
        return jnp.repeat(v.astype(F32), SSD_HEADDIM).reshape(1, MIX_W)

    q = dict(
        w_main=w_main,
        g_pre_mix=p['g_pre_mix'].reshape(1, D_MODEL),
        g_post_mix=p['g_post_mix'].reshape(1, D_MODEL),
        g_pre_ffn=p['g_pre_ffn'].reshape(1, D_MODEL),
        g_post_ffn=p['g_post_ffn'].reshape(1, D_MODEL),
        ret_gn=p['ret_gn'],
        conv_w=p['ssd_conv_w'],
        conv_b=p['ssd_conv_b'].reshape(1, SSD_CONV_DIM),
        a_log=head_lanes(p['ssd_a_log']),
        dt_bias=head_lanes(p['ssd_dt_bias']),
        d_skip=head_lanes(p['ssd_d']),
        ssd_norm=p['ssd_norm'].reshape(1, MIX_W),
        hg_lb=lb.reshape(1, MIX_W),
        hg_norm=p['hg_norm'].reshape(1, HG_DK),
        s5=_s5_params(p),
        w_glu=p['s5_w_glu'].astype(BF16),
        w_branch=p['w_branch'].astype(BF16),
        w_out=p['w_out'].astype(BF16),
        w_ff1=p['w_ff1'].astype(BF16),
        w_ff2=p['w_ff2'].astype(BF16),
    )
    return q


def _layer(x3, states, acc, layer, q, pos, cfg):
    b, l, _ = x3.shape
    x2 = x3.reshape(b * l, D_MODEL)
    tb, tl, c = cfg['tb'], cfg['tl'], cfg['c']
    proj = _proj_call(x2, q['g_pre_mix'], q['w_main'], tm=cfg['tm_proj'], tn=cfg['tn'])
    st = {n: states[n][layer] for n in ('conv', 's5_re', 's5_im')}
    o_ret, s_ret = _ret_call(proj, pos, q['ret_gn'], states['ret'], acc['ret'], layer, b=b, l=l, tb=tb, tl=tl, sl=c,
                             unroll=cfg['u_ret'])
    o_ssd, conv_new, s_ssd = _ssd_call(proj, q['conv_w'], q['conv_b'], q['a_log'], q['dt_bias'], q['d_skip'],
                                       q['ssd_norm'], st['conv'], states['ssd'], acc['ssd'], layer,
                                       b=b, l=l, tb=tb, tl=tl, sl=c, unroll=cfg['u_ssd'])
    o_hg, s_hg = _hg_call(proj, q['hg_lb'], q['hg_norm'], states['hgrn'], acc['hgrn'], layer, b=b, l=l, tb=tb, tl=tl,
                          sl=c, unroll=cfg['u_hg'])
    wbu, are, aim, wc, d5 = q['s5']
    o_s5, s5_re, s5_im = _s5_call(proj, wbu, are, aim, wc, d5, q['w_glu'],
                                  st['s5_re'].reshape(b, S5_LANES), st['s5_im'].reshape(b, S5_LANES),
                                  b=b, l=l, tb=SUBLANE, tl=cfg['s5_tl'])
    x2 = _merge_call(x2, proj, (o_ret, o_ssd, o_hg, o_s5), q['w_branch'], q['w_out'], q['g_post_mix'], tm=cfg['tm_merge'])
    x2 = _ffn_call(x2, q['g_pre_ffn'], q['w_ff1'], q['w_ff2'], q['g_post_ffn'], tm=cfg['tm'], tf=cfg['tf'])
    def push(name, new):
        return new[None] if acc[name] is None else jnp.concatenate([acc[name], new[None]], axis=0)

    new = {'ret': s_ret, 'ssd': s_ssd, 'hgrn': s_hg, 'conv': push('conv', conv_new),
           's5_re': push('s5_re', s5_re.reshape(b, S5_GROUPS, S5_STATE)),
           's5_im': push('s5_im', s5_im.reshape(b, S5_GROUPS, S5_STATE))}
    return x2.reshape(b, l, D_MODEL), new


_NAMES = ('ret', 'ssd', 'conv', 'hgrn', 's5_re', 's5_im')


def _trunk(x, states, layers, pos, cfg):
    acc = {n: None for n in _NAMES}
    for layer, q in enumerate(layers):
        x, acc = _layer(x, states, acc, layer, q, pos, cfg)
    return x, acc


def _group_cfg(b, l):
    if l % CHUNK == 0:
        return dict(tb=1, tl=min(l, 512), c=CHUNK, tm=1024, tm_proj=2048, tn=1536, tm_merge=512, tf=1024,
                    s5_tl=min(l, 64), u_ret=4, u_ssd=8, u_hg=8)
    return dict(tb=SUBLANE, tl=l, c=l, tm=min(b * l, 1024), tm_proj=min(b * l, 1024), tn=1536,
                tm_merge=min(b * l, 512), tf=1024, s5_tl=l, u_ret=1, u_ssd=1, u_hg=1)


def kernel(x_prompt, x_sample, state_ret, state_ssd, state_conv, state_hgrn, state_s5_re, state_s5_im, g_pre_mix, g_post_mix, g_pre_ffn, g_post_ffn, w_in, ret_gn, ssd_conv_w, ssd_conv_b, ssd_a_log, ssd_dt_bias, ssd_d, ssd_norm, hg_lb_logits, hg_norm, s5_a_re, s5_a_im, s5_b_re, s5_b_im, s5_c_re, s5_c_im, s5_d, s5_log_dt, s5_w_glu, w_branch, w_out, w_ff1, w_ff2):
    params = {
        'g_pre_mix': g_pre_mix, 'g_post_mix': g_post_mix, 'g_pre_ffn': g_pre_ffn, 'g_post_ffn': g_post_ffn,
        'ret_gn': ret_gn, 'ssd_conv_w': ssd_conv_w, 'ssd_conv_b': ssd_conv_b,
        'ssd_a_log': ssd_a_log, 'ssd_dt_bias': ssd_dt_bias, 'ssd_d': ssd_d, 'ssd_norm': ssd_norm,
        'hg_norm': hg_norm, 's5_a_re': s5_a_re, 's5_a_im': s5_a_im, 's5_b_re': s5_b_re, 's5_b_im': s5_b_im,
        's5_c_re': s5_c_re, 's5_c_im': s5_c_im, 's5_d': s5_d, 's5_log_dt': s5_log_dt, 's5_w_glu': s5_w_glu,
        'w_branch': w_branch, 'w_out': w_out, 'w_ff1': w_ff1, 'w_ff2': w_ff2,
    }
    depth = w_in.shape[0]
    w = jax.nn.softmax(hg_lb_logits.astype(F32), axis=0)
    lbs = jnp.cumsum(w, axis=0) - w[0]
    layers = [_prep_layer({k: v[i] for k, v in params.items()}, lbs[i], _wprep_call(w_in, i)) for i in range(depth)]

    bp, lp, _ = x_prompt.shape
    bs, ls, _ = x_sample.shape
    zero_states = {
        'ret': jnp.zeros((depth, bp) + state_ret.shape[2:], F32),
        'ssd': jnp.zeros((depth, bp) + state_ssd.shape[2:], F32),
        'conv': jnp.zeros((depth, bp) + state_conv.shape[2:], F32),
        'hgrn': jnp.zeros((depth, bp) + state_hgrn.shape[2:], F32),
        's5_re': jnp.zeros((depth, bp) + state_s5_re.shape[2:], F32),
        's5_im': jnp.zeros((depth, bp) + state_s5_im.shape[2:], F32),
    }
    sample_states = {'ret': state_ret, 'ssd': state_ssd, 'conv': state_conv, 'hgrn': state_hgrn,
                     's5_re': state_s5_re, 's5_im': state_s5_im}
    pos_p = np.arange(lp)
    pos_s = PAST_LEN + np.arange(ls)
    y_p, new_p = _trunk(x_prompt, zero_states, layers, pos_p, _group_cfg(bp, lp))
    y_s, new_s = _trunk(x_sample, sample_states, layers, pos_s, _group_cfg(bs, ls))
    return (y_p, y_s,
            new_p['ret'], new_s['ret'], new_p['ssd'], new_s['ssd'], new_p['conv'], new_s['conv'],
            new_p['hgrn'], new_s['hgrn'], new_p['s5_re'], new_s['s5_re'], new_p['s5_im'], new_s['s5_im'])
```

```python
import functools
import math

import jax
import jax.numpy as jnp
import numpy as np
from jax import lax
from jax.experimental import pallas as pl
from jax.experimental.pallas import tpu as pltpu

F32 = jnp.float32
BF16 = jnp.bfloat16

D_MODEL = 1024
DEPTH = 2
PAST_LEN = 16384
MIX_W = D_MODEL // 2
N_BRANCH = 4
RET_HEADS = 4
RET_DK = MIX_W // RET_HEADS
ROPE_BASE = 10000.0
SSD_HEADDIM = 64
SSD_HEADS = MIX_W // SSD_HEADDIM
SSD_GROUPS = 2
SSD_STATE = 128
SSD_CONV = 4
SSD_CONV_DIM = MIX_W + 2 * SSD_GROUPS * SSD_STATE
SSD_PAIRS = SSD_HEADS // 2
HG_HEADS = 4
HG_DK = MIX_W // HG_HEADS
S5_GROUP = 16
S5_GROUPS = MIX_W // S5_GROUP
S5_STATE = 64
S5_LANES = S5_GROUPS * S5_STATE
D_FF = 4 * D_MODEL
GATE_COLS = N_BRANCH * D_MODEL
CHUNK = 64
CR = 64
EPS = 1e-6

LANE = 128
SUBLANE = 8
VMEM_LIMIT = 48 * 1024 * 1024

COL_RET = 0
COL_SZ = COL_RET + 4 * MIX_W
COL_XS = COL_SZ + MIX_W
COL_BC = COL_XS + MIX_W
COL_HG = COL_BC + MIX_W
COL_SU = COL_HG + 4 * MIX_W
COL_GL = COL_SU + MIX_W
COL_DT = COL_GL + GATE_COLS
PROJ_COLS = COL_DT + MIX_W
GL_BLK = GATE_COLS // 2
assert COL_GL % GL_BLK == 0


def _sigmoid(x):
    return 1.0 / (1.0 + jnp.exp(-x))


def _silu(x):
    return x * _sigmoid(x)


def _softplus(x):
    return jnp.maximum(x, 0.0) + jnp.log1p(jnp.exp(-jnp.abs(x)))


def _gelu_tanh(x):
    c = math.sqrt(2.0 / math.pi)
    return 0.5 * x * (1.0 + jnp.tanh(c * (x + 0.044715 * (x * x * x))))


def _rms(x, g):
    ms = jnp.mean(x * x, axis=-1, keepdims=True)
    return x * lax.rsqrt(ms + EPS) * g


def _dot(a, b):
    return jnp.dot(a, b, preferred_element_type=F32)


def _dot_nt(a, b):
    return lax.dot_general(a, b, (((1,), (1,)), ((), ())), preferred_element_type=F32)


def _split3(a):
    hi = a.astype(BF16)
    r1 = a - hi.astype(F32)
    mid = r1.astype(BF16)
    lo = (r1 - mid.astype(F32)).astype(BF16)
    return hi, mid, lo


def _sel_l(m, a):
    hi, mid, lo = _split3(a)
    return (_dot(m, lo) + _dot(m, mid)) + _dot(m, hi)


def _sel2_l(m, a):
    hi = a.astype(BF16)
    mid = (a - hi.astype(F32)).astype(BF16)
    return _dot(m, mid) + _dot(m, hi)


def _sel_r(a, m):
    hi, mid, lo = _split3(a)
    return (_dot(lo, m) + _dot(mid, m)) + _dot(hi, m)


def _cparams(sem):
    return pltpu.CompilerParams(dimension_semantics=sem, vmem_limit_bytes=VMEM_LIMIT)


def _const_spec(shape):
    nd = len(shape)
    return pl.BlockSpec(shape, lambda *_: (0,) * nd)


def _state_io(s_all, prev, layer, tb):
    dims = tuple(s_all.shape[2:])
    zeros = (0,) * len(dims)
    in_spec = pl.BlockSpec((None, tb) + dims, lambda bi, li: (layer, bi) + zeros)
    if prev is None:
        nprev, prev, prev_spec = 0, s_all, in_spec
    else:
        nprev = prev.shape[0]
        prev_spec = pl.BlockSpec((nprev, tb) + dims, lambda bi, li: (0, bi) + zeros)
    out_spec = pl.BlockSpec((nprev + 1, tb) + dims, lambda bi, li: (0, bi) + zeros)
    out_shape = jax.ShapeDtypeStruct((nprev + 1, s_all.shape[1]) + dims, F32)
    return nprev, prev, in_spec, prev_spec, out_spec, out_shape


def _proj_kernel(x_ref, g_ref, w_ref, o_ref, h_ref):
    @pl.when(pl.program_id(1) == 0)
    def _():
        h_ref[...] = _rms(x_ref[...], g_ref[...]).astype(BF16)

    o_ref[...] = _dot(h_ref[...], w_ref[...]).astype(BF16)


def _proj_call(x2, g, w, *, tm, tn):
    t, d = x2.shape
    n = w.shape[1]
    return pl.pallas_call(
        _proj_kernel,
        grid=(t // tm, n // tn),
        in_specs=[
            pl.BlockSpec((tm, d), lambda i, j: (i, 0)),
            pl.BlockSpec((1, d), lambda i, j: (0, 0)),
            pl.BlockSpec((d, tn), lambda i, j: (0, j)),
        ],
        out_specs=pl.BlockSpec((tm, tn), lambda i, j: (i, j)),
        out_shape=jax.ShapeDtypeStruct((t, n), BF16),
        scratch_shapes=[pltpu.VMEM((tm, d), BF16)],
        compiler_params=_cparams(("parallel", "arbitrary")),
        name="in_proj",
    )(x2, g, w)


def _ret_consts(sl):
    nseq = CR // sl
    h = np.arange(RET_HEADS, dtype=np.float64)
    log_g = np.log(1.0 - 2.0 ** (-5.0 - h))
    idx = np.arange(sl, dtype=np.float64)
    diff = idx[:, None] - idx[None, :]
    dmat = np.where(diff >= 0, np.exp(log_g[:, None, None] * np.maximum(diff, 0.0)), 0.0)
    dmat = np.stack([np.kron(np.eye(nseq), d) for d in dmat])
    q_dec = np.tile(np.exp(log_g[:, None] * (idx[None, :] + 1.0)), (1, nseq))
    k_dec = np.tile(np.exp(log_g[:, None] * (sl - 1.0 - idx[None, :])), (1, nseq))
    chunk_dec = np.exp(log_g * sl)
    qd = np.broadcast_to(q_dec[:, :, None], (RET_HEADS, CR, LANE))
    kd = np.broadcast_to(k_dec[:, :, None], (RET_HEADS, CR, LANE))
    return (jnp.asarray(dmat, F32), jnp.asarray(qd, F32), jnp.asarray(kd, F32),
            tuple(float(v) for v in chunk_dec))


def _rope_tables(pos):
    half = RET_DK // 2
    inv = ROPE_BASE ** (-np.arange(half, dtype=np.float64) / half)
    ang = np.asarray(pos, np.float64)[:, None] * inv[None, :]
    cos, sin = np.cos(ang), np.sin(ang)
    return (jnp.asarray(np.concatenate([cos, cos], axis=1), F32),
            jnp.asarray(np.concatenate([-sin, sin], axis=1), F32))


def _ret_kernel(q_ref, k_ref, v_ref, g_ref, cos_ref, sin_ref, dmat_ref, qd_ref, kd_ref, gn_ref, s0_ref, prev_ref,
                o_ref, so_ref, *, tb, tl, sl, cdec, nprev, unroll):
    st_ref = so_ref.at[nprev]

    @pl.when(pl.program_id(1) == 0)
    def _():
        st_ref[...] = s0_ref[...]
        if nprev:
            so_ref[0:nprev] = prev_ref[...]

    nseq = CR // sl
    scale = RET_DK ** -0.5

    def body(pc, carry):
        r0 = pl.multiple_of(pc * CR, CR)
        t0 = r0 if nseq == 1 else 0
        cos = cos_ref[pl.ds(t0, CR), :]
        sin = sin_ref[pl.ds(t0, CR), :]
        heads = [slice(h * LANE, (h + 1) * LANE) for h in range(RET_HEADS)]
        qrs, krs = [], []
        for hs in heads:
            q = q_ref[pl.ds(r0, CR), hs].astype(F32)
            k = k_ref[pl.ds(r0, CR), hs].astype(F32)
            qrs.append(q * cos + pltpu.roll(q, LANE // 2, axis=1) * sin)
            krs.append((k * cos + pltpu.roll(k, LANE // 2, axis=1) * sin) * scale)
        vs = [v_ref[pl.ds(r0, CR), hs] for hs in heads]
        atts = [(_dot_nt(qrs[h].astype(BF16), krs[h].astype(BF16)) * dmat_ref[h]).astype(BF16)
                for h in range(RET_HEADS)]
        outs = []
        for h in range(RET_HEADS):
            o = _dot(atts[h], vs[h])
            qd = qrs[h] * qd_ref[h]
            kd = krs[h] * kd_ref[h]
            ois = []
            for s in range(nseq):
                seq = pc * nseq + s if nseq > 1 else 0
                rs = slice(s * sl, (s + 1) * sl)
                st = st_ref[seq, h]
                ois.append(_dot(qd[rs].astype(BF16), st.astype(BF16)))
                v_s = vs[h] if nseq == 1 else vs[h].astype(F32)[rs].astype(BF16)
                st_ref[seq, h] = st * cdec[h] + _dot(kd[rs].T.astype(BF16), v_s)
            outs.append(o + (ois[0] if nseq == 1 else jnp.concatenate(ois, axis=0)))
        for h, hs in enumerate(heads):
            o = outs[h]
            mu = jnp.mean(o, axis=-1, keepdims=True)
            oc = o - mu
            var = jnp.mean(oc * oc, axis=-1, keepdims=True)
            on = oc * lax.rsqrt(var + EPS) * gn_ref[h:h + 1, :]
            o_ref[pl.ds(r0, CR), hs] = (on * _silu(g_ref[pl.ds(r0, CR), hs].astype(F32))).astype(BF16)
        return carry

    lax.fori_loop(0, tb * tl // CR, body, 0, unroll=unroll)


def _seq_grid(b, l, tb, tl):
    assert b % tb == 0 and l % tl == 0 and (tb == 1 or tl == l)
    nl = l // tl
    rows = tb * tl

    def row_map(col_blk):
        return lambda bi, li: (bi * nl + li, col_blk)

    return (b // tb, nl), rows, row_map


def _ret_call(proj, pos, gn, s_all, prev, layer, *, b, l, tb, tl, sl, unroll):
    grid, rows, row_map = _seq_grid(b, l, tb, tl)
    assert rows % CR == 0 and CR % sl == 0 and (sl == CR and tb == 1 or sl == tl)
    dmat, qd, kd, cdec = _ret_consts(sl)
    cos, sin = _rope_tables(pos)
    if sl < CR:
        cos, sin = jnp.tile(cos, (CR // sl, 1)), jnp.tile(sin, (CR // sl, 1))
        tab_spec = _const_spec(cos.shape)
    else:
        tab_spec = pl.BlockSpec((tl, LANE), lambda bi, li: (li, 0))
    cb = COL_RET // MIX_W
    nprev, prev, s_spec, prev_spec, so_spec, so_shape = _state_io(s_all, prev, layer, tb)
    return pl.pallas_call(
        functools.partial(_ret_kernel, tb=tb, tl=tl, sl=sl, cdec=cdec, nprev=nprev, unroll=unroll),
        grid=grid,
        in_specs=[pl.BlockSpec((rows, MIX_W), row_map(cb + i)) for i in range(4)] + [
            tab_spec, tab_spec,
            _const_spec(dmat.shape), _const_spec(qd.shape), _const_spec(kd.shape), _const_spec(gn.shape),
            s_spec, prev_spec,
        ],
        out_specs=[pl.BlockSpec((rows, MIX_W), row_map(0)), so_spec],
        out_shape=[jax.ShapeDtypeStruct((b * l, MIX_W), BF16), so_shape],
        compiler_params=_cparams(("parallel", "arbitrary")),
        name="retention",
    )(proj, proj, proj, proj, cos, sin, dmat, qd, kd, gn, s_all, prev)


def _ssd_consts(sl):
    nseq = CR // sl
    tril = np.kron(np.eye(nseq), np.tril(np.ones((sl, sl))))
    sumall = np.kron(np.eye(nseq), np.ones((sl, sl)))
    ones = np.ones((CR, CR))
    i = np.arange(CR)[:, None]
    j = np.arange(MIX_W)[None, :] % SSD_HEADDIM
    eye = (i == j).astype(np.float64)
    mask = ((i // sl == j // sl) & (i >= j)).astype(np.float64)
    return (jnp.asarray(np.concatenate([tril, sumall], axis=0), BF16), jnp.asarray(ones, BF16),
            jnp.asarray(eye, F32), jnp.asarray(mask, F32))


def _ssd_kernel(z_ref, xs_ref, bc_ref, dtx_ref, cw_ref, cb_ref, alog_ref, dtb_ref, dsk_ref, nw_ref,
                mcum_ref, ones_ref, eye_ref, mask_ref, cs0_ref, s0_ref, prev_ref,
                o_ref, cs_ref, so_ref, xpad, st, *, tb, tl, sl, nprev, unroll):
    li = pl.program_id(1)
    nl = pl.num_programs(1)
    nseq = CR // sl
    kc = SSD_CONV - 1
    off = SUBLANE - kc
    grp = 2 * SUBLANE

    @pl.when(li == 0)
    def _():
        for s in range(tb):
            for p in range(SSD_PAIRS):
                st[s, p] = s0_ref[s, p].T
        if nprev:
            so_ref[0:nprev] = prev_ref[...]
        if nseq == 1:
            xpad[0:SUBLANE, :] = jnp.zeros((SUBLANE, SSD_CONV_DIM), F32)
            xpad[off:SUBLANE, :] = cs0_ref[0]

    if nseq == 1:
        xpad[SUBLANE:SUBLANE + tl, 0:MIX_W] = xs_ref[...].astype(F32)
        xpad[SUBLANE:SUBLANE + tl, MIX_W:2 * MIX_W] = bc_ref[...].astype(F32)
    else:
        xs_new = xs_ref[...].astype(F32)
        bc_new = bc_ref[...].astype(F32)
        for s in range(tb):
            xpad[s * grp:s * grp + SUBLANE, :] = jnp.zeros((SUBLANE, SSD_CONV_DIM), F32)
            xpad[s * grp + off:s * grp + SUBLANE, :] = cs0_ref[s]
            xpad[s * grp + SUBLANE:(s + 1) * grp, 0:MIX_W] = xs_new[s * sl:(s + 1) * sl, :]
            xpad[s * grp + SUBLANE:(s + 1) * grp, MIX_W:2 * MIX_W] = bc_new[s * sl:(s + 1) * sl, :]
            cs_ref[s] = xpad[(s + 1) * grp - kc:(s + 1) * grp, :]

    a_neg = -jnp.exp(alog_ref[...])
    half = lax.broadcasted_iota(jnp.int32, (CR, LANE), 1) < SSD_HEADDIM

    def body(pc, carry):
        r0 = pl.multiple_of(pc * CR, CR)
        if nseq == 1:
            win = xpad[pl.ds(r0, CR + SUBLANE), :]
            take = lambda a: a[SUBLANE:, :]
        else:
            win = xpad[pl.ds(pl.multiple_of(pc * nseq * grp, nseq * grp), nseq * grp), :]
            take = lambda a: a.reshape(nseq, grp, SSD_CONV_DIM)[:, SUBLANE:, :].reshape(CR, SSD_CONV_DIM)
        acc = cb_ref[...] + take(pltpu.roll(win, kc, axis=0)) * cw_ref[0:1, :]
        for j in range(1, kc):
            acc = acc + take(pltpu.roll(win, kc - j, axis=0)) * cw_ref[j:j + 1, :]
        xc = _silu(acc + take(win) * cw_ref[kc:kc + 1, :])
        xs = xc[:, 0:MIX_W]
        bm = xc[:, MIX_W:MIX_W + SSD_GROUPS * SSD_STATE]
        cm = xc[:, MIX_W + SSD_GROUPS * SSD_STATE:2 * MIX_W]

        dtx = _softplus(dtx_ref[pl.ds(r0, CR), :].astype(F32) + dtb_ref[...])
        both = _sel_l(mcum_ref[...], dtx * a_neg)
        cum, clast = both[0:CR], both[CR:2 * CR]
        ecum = jnp.exp(cum)
        dend = jnp.exp(clast - cum)
        elast = jnp.exp(clast)
        rr = _sel_l(ones_ref[...], cum * eye_ref[...])
        msk = mask_ref[...]
        lmat = jnp.exp(jnp.where(msk > 0, cum - rr, 0.0)) * msk
        zz = z_ref[pl.ds(r0, CR), :].astype(F32)
        npp = SSD_PAIRS // SSD_GROUPS
        bm_gs = [bm[:, g * SSD_STATE:(g + 1) * SSD_STATE] for g in range(SSD_GROUPS)]
        cm_gs = [cm[:, g * SSD_STATE:(g + 1) * SSD_STATE] for g in range(SSD_GROUPS)]
        cm_gbs = [c_.astype(BF16) for c_ in cm_gs]
        cb2s = [_dot_nt(cm_gbs[g], jnp.concatenate([bm_gs[g], bm_gs[g]], axis=0).astype(BF16))
                for g in range(SSD_GROUPS)]
        pss = [slice(p * LANE, (p + 1) * LANE) for p in range(SSD_PAIRS)]
        xdts = [xs[:, ps] * dtx[:, ps] for ps in pss]
        xsts = [jnp.concatenate([jnp.where(half, x_, 0.0), jnp.where(half, 0.0, x_)], axis=0).astype(BF16)
                for x_ in xdts]
        ys = [_dot((cb2s[p // npp] * lmat[:, pss[p]]).astype(BF16), xsts[p]) for p in range(SSD_PAIRS)]
        xds = [xdts[p] * dend[:, pss[p]] for p in range(SSD_PAIRS)]
        yis = []
        for p in range(SSD_PAIRS):
            g = p // npp
            parts = []
            for s in range(nseq):
                seq = pc * nseq + s if nseq > 1 else 0
                if nseq == 1:
                    cm_s, bm_s, xd_s = cm_gbs[g], bm_gs[g], xds[p]
                else:
                    rs = slice(s * sl, (s + 1) * sl)
                    cm_s, bm_s, xd_s = cm_gs[g][rs].astype(BF16), bm_gs[g][rs], xds[p][rs]
                stp = st[seq, p]
                parts.append(_dot(cm_s, stp.astype(BF16)))
                st[seq, p] = stp * elast[s * sl:s * sl + 1, pss[p]] + _dot(bm_s.T.astype(BF16), xd_s.astype(BF16))
            yis.append(parts[0] if nseq == 1 else jnp.concatenate(parts, axis=0))
        gw = SSD_HEADDIM * SSD_HEADS // SSD_GROUPS
        for g in range(SSD_GROUPS):
            yg = jnp.concatenate([ys[p] + yis[p] * ecum[:, pss[p]] + xs[:, pss[p]] * dsk_ref[:, pss[p]]
                                  for p in range(g * npp, (g + 1) * npp)], axis=1)
            zs = slice(g * gw, (g + 1) * gw)
            o_ref[pl.ds(r0, CR), zs] = _rms(yg * _silu(zz[:, zs]), nw_ref[:, zs]).astype(BF16)
        return carry

    lax.fori_loop(0, tb * tl // CR, body, 0, unroll=unroll)

    if nseq == 1:
        xpad[0:SUBLANE, :] = xpad[tl:tl + SUBLANE, :]

    @pl.when(li == nl - 1)
    def _():
        if nseq == 1:
            cs_ref[0] = xpad[tl + off:tl + SUBLANE, :]
        for s in range(tb):
            for p in range(SSD_PAIRS):
                so_ref[nprev, s, p] = st[s, p].T


def _ssd_call(proj, cw, cb, alog, dtb, dsk, nw, cs0, s_all, prev, layer, *, b, l, tb, tl, sl, unroll):
    grid, rows, row_map = _seq_grid(b, l, tb, tl)
    assert rows % CR == 0 and CR % sl == 0 and (sl == CR and tb == 1 or sl == tl == SUBLANE)
    consts = _ssd_consts(sl)
    pair = lambda a: a.reshape(a.shape[:2] + (SSD_PAIRS, LANE, SSD_STATE))
    s_pairs = pair(s_all)
    nprev, prev, s_spec, prev_spec, so_spec, so_shape = _state_io(s_pairs, None if prev is None else pair(prev),
                                                                  layer, tb)
    cs_spec = pl.BlockSpec((tb, SSD_CONV - 1, SSD_CONV_DIM), lambda bi, li: (bi, 0, 0))
    params = (cw, cb, alog, dtb, dsk, nw)
    xpad_rows = SUBLANE + tl if sl == CR else tb * 2 * SUBLANE
    o, cs, so = pl.pallas_call(
        functools.partial(_ssd_kernel, tb=tb, tl=tl, sl=sl, nprev=nprev, unroll=unroll),
        grid=grid,
        in_specs=[pl.BlockSpec((rows, MIX_W), row_map(cb_)) for cb_ in
                  (COL_SZ // MIX_W, COL_XS // MIX_W, COL_BC // MIX_W, COL_DT // MIX_W)]
        + [_const_spec(a.shape) for a in params + consts] + [cs_spec, s_spec, prev_spec],
        out_specs=[pl.BlockSpec((rows, MIX_W), row_map(0)), cs_spec, so_spec],
        out_shape=[jax.ShapeDtypeStruct((b * l, MIX_W), BF16), jax.ShapeDtypeStruct(cs0.shape, F32), so_shape],
        scratch_shapes=[pltpu.VMEM((xpad_rows, SSD_CONV_DIM), F32),
                        pltpu.VMEM((tb, SSD_PAIRS, SSD_STATE, LANE), F32)],
        compiler_params=_cparams(("parallel", "arbitrary")),
        name="ssd",
    )(proj, proj, proj, proj, *params, *consts, cs0, s_pairs, prev)
    return o, cs, so.reshape((nprev + 1,) + s_all.shape[1:])


def _hg_levels(sl):
    ms = []
    m = sl // 2
    while m >= 1:
        ms.append(m)
        m //= 2
    return ms


def _hg_consts(sl):
    nseq = CR // sl
    tril = np.kron(np.eye(nseq), np.tril(np.ones((sl, sl))))
    sumall = np.kron(np.eye(nseq), np.ones((sl, sl)))
    mats = []
    masks = []
    i = np.arange(CR)
    for m in _hg_levels(sl):
        ref = (i // (2 * m)) * (2 * m) + m - 1
        if m > 1:
            mats.append(np.abs(tril - tril[ref]))
        same = (i[:, None] // (2 * m)) == (i[None, :] // (2 * m))
        qside = ((i // m) % 2 == 1)[:, None]
        kside = ((i // m) % 2 == 0)[None, :]
        masks.append((same & qside & kside).astype(np.float64))
    return (jnp.asarray(np.concatenate([tril, sumall], axis=0), BF16), jnp.asarray(np.concatenate(mats, axis=0), BF16),
            jnp.asarray(np.stack(masks), F32))


def _hg_kernel(q_ref, f_ref, i_ref, g_ref, lb_ref, nw_ref, mcum_ref, mlev_ref, mask_ref, s0_ref, prev_ref,
               o_ref, so_ref, st, qe_s, ke_s, qg_s, kd_s, el_s, dv_s, *, tb, tl, sl, nprev, unroll):
    li = pl.program_id(1)
    nl = pl.num_programs(1)

    @pl.when(li == 0)
    def _():
        for s in range(tb):
            for h in range(HG_HEADS):
                st[s, h] = s0_ref[s, h].T
        if nprev:
            so_ref[0:nprev] = prev_ref[...]

    nseq = CR // sl
    levels = _hg_levels(sl)

    nlev = len(levels)
    odd_row = (lax.broadcasted_iota(jnp.int32, (CR, MIX_W), 0) & 1) == 1

    def stage_a(pc, carry):
        r0 = pl.multiple_of(pc * CR, CR)
        lb = lb_ref[...]
        q = _silu(q_ref[pl.ds(r0, CR), :].astype(F32))
        sig = _sigmoid(f_ref[pl.ds(r0, CR), :].astype(F32))
        fdec = lb + (1.0 - lb) * sig
        logf = jnp.log(fdec)
        k = (1.0 - lb) * (1.0 - sig)
        v = i_ref[pl.ds(r0, CR), :].astype(F32)
        cums = _sel_l(mcum_ref[...], logf)
        gc = cums[0:CR]
        glast = cums[CR:2 * CR]
        if nlev > 1:
            dl = _sel2_l(mlev_ref[...], logf)
        qb, kb = q.astype(BF16), k.astype(BF16)
        for n in range(nlev):
            e = jnp.exp(dl[n * CR:(n + 1) * CR]) if n < nlev - 1 else jnp.where(odd_row, fdec, 1.0)
            e = e.astype(BF16)
            qe_s[pc, n] = qb * e
            ke_s[pc, n] = kb * e
        qg_s[pc] = (q * jnp.exp(gc)).astype(BF16)
        kd_s[pc] = (k * jnp.exp(glast - gc)).astype(BF16)
        el_s[pc] = jnp.exp(glast)
        qk = q * k
        dv_s[pc] = jnp.concatenate(
            [jnp.sum(qk[:, h * LANE:(h + 1) * LANE], axis=-1, keepdims=True) * v[:, h * LANE:(h + 1) * LANE]
             for h in range(HG_HEADS)], axis=1)
        return carry

    def stage_b(pc, carry):
        r0 = pl.multiple_of(pc * CR, CR)
        heads = [slice(h * LANE, (h + 1) * LANE) for h in range(HG_HEADS)]
        vs = [i_ref[pl.ds(r0, CR), hs] for hs in heads]
        atts = []
        for hs in heads:
            att = None
            for n in range(nlev):
                a = _dot_nt(qe_s[pc, n, :, hs], ke_s[pc, n, :, hs]) * mask_ref[n]
                att = a if att is None else att + a
            atts.append(att.astype(BF16))
        outs = []
        for h, hs in enumerate(heads):
            o = _dot(atts[h], vs[h]) + dv_s[pc, :, hs]
            ois = []
            for s in range(nseq):
                seq = pc * nseq + s if nseq > 1 else 0
                rs = slice(s * sl, (s + 1) * sl)
                stt = st[seq, h]
                if nseq == 1:
                    qg_r, kd_r, v_r = qg_s[pc, :, hs], kd_s[pc, :, hs], vs[h].astype(F32)
                else:
                    qg_r = qg_s[pc, :, hs].astype(F32)[rs].astype(BF16)
                    kd_r = kd_s[pc, :, hs].astype(F32)[rs].astype(BF16)
                    v_r = vs[h].astype(F32)[rs]
                ois.append(_dot_nt(qg_r, stt.astype(BF16)))
                st[seq, h] = stt * el_s[pc, s * sl:s * sl + 1, hs] + _dot(v_r.T.astype(BF16), kd_r)
            outs.append(o + (ois[0] if nseq == 1 else jnp.concatenate(ois, axis=0)))
        for h, hs in enumerate(heads):
            gate = _sigmoid(g_ref[pl.ds(r0, CR), hs].astype(F32))
            o_ref[pl.ds(r0, CR), hs] = (_rms(outs[h], nw_ref[...]) * gate).astype(BF16)
        return carry

    ngrp = tb * tl // CR
    lax.fori_loop(0, ngrp, stage_a, 0, unroll=unroll)
    lax.fori_loop(0, ngrp, stage_b, 0, unroll=unroll)

    @pl.when(li == nl - 1)
    def _():
        for s in range(tb):
            for h in range(HG_HEADS):
                so_ref[nprev, s, h] = st[s, h].T


def _hg_call(proj, lb, nw, s_all, prev, layer, *, b, l, tb, tl, sl, unroll):
    grid, rows, row_map = _seq_grid(b, l, tb, tl)
    assert rows % CR == 0 and CR % sl == 0 and (sl == CR and tb == 1 or sl == tl)
    consts = _hg_consts(sl)
    nlev = len(_hg_levels(sl))
    cb = COL_HG // MIX_W
    nprev, prev, s_spec, prev_spec, so_spec, so_shape = _state_io(s_all, prev, layer, tb)
    stage = lambda dt, *lead: pltpu.VMEM((rows // CR,) + lead + (CR, MIX_W), dt)
    return pl.pallas_call(
        functools.partial(_hg_kernel, tb=tb, tl=tl, sl=sl, nprev=nprev, unroll=unroll),
        grid=grid,
        in_specs=[pl.BlockSpec((rows, MIX_W), row_map(cb + i)) for i in range(4)]
        + [_const_spec(a.shape) for a in (lb, nw) + consts] + [s_spec, prev_spec],
        out_specs=[pl.BlockSpec((rows, MIX_W), row_map(0)), so_spec],
        out_shape=[jax.ShapeDtypeStruct((b * l, MIX_W), BF16), so_shape],
        scratch_shapes=[pltpu.VMEM((tb, HG_HEADS, HG_DK, HG_DK), F32), stage(BF16, nlev), stage(BF16, nlev),
                        stage(BF16), stage(BF16), stage(F32), stage(F32)],
        compiler_params=_cparams(("parallel", "arbitrary")),
        name="hgrn2",
    )(proj, proj, proj, proj, lb, nw, *consts, s_all, prev)


S5_TILES = MIX_W // LANE
S5_TLANES = S5_LANES // S5_TILES


def _s5_perm(tb, tl):
    rows = tb * tl
    p = np.zeros((rows, rows))
    t, s = np.meshgrid(np.arange(tl), np.arange(tb), indexing='ij')
    p[(s * tl + t).ravel(), (t * tb + s).ravel()] = 1.0
    return jnp.asarray(p, BF16), jnp.asarray(p.T, BF16)


def _s5_kernel(su_ref, perm_ref, permt_ref, wbu_ref, are_ref, aim_ref, wc_ref, d_ref, wglu_ref, h0r_ref, h0i_ref,
               o_ref, hr_ref, hi_ref, bur, bui, *, tb, tl):
    @pl.when(pl.program_id(1) == 0)
    def _():
        hr_ref[...] = h0r_ref[...]
        hi_ref[...] = h0i_ref[...]

    rows = tl * tb
    u = _dot(permt_ref[...], su_ref[...].reshape(rows, MIX_W))
    tiles = [slice(t * S5_TLANES, (t + 1) * S5_TLANES) for t in range(S5_TILES)]

    def bu_tile(t):
        bu = _dot(u[:, t * LANE:(t + 1) * LANE].astype(BF16), wbu_ref[t])
        bur[:, tiles[t]] = bu[:, :S5_TLANES]
        bui[:, tiles[t]] = bu[:, S5_TLANES:]

    def scan_tile(t):
        ls = tiles[t]
        ar = jnp.broadcast_to(are_ref[:, ls], (tb, S5_TLANES))
        ai = jnp.broadcast_to(aim_ref[:, ls], (tb, S5_TLANES))
        hr, hi = hr_ref[:, ls], hi_ref[:, ls]
        for i in range(tl):
            rs = slice(i * tb, (i + 1) * tb)
            hr, hi = (ar * hr - ai * hi) + bur[rs, ls], (ar * hi + ai * hr) + bui[rs, ls]
            bur[rs, ls] = hr
            bui[rs, ls] = hi
        hr_ref[:, ls] = hr
        hi_ref[:, ls] = hi

    def y_tile(t):
        hcat = jnp.concatenate([bur[:, tiles[t]], bui[:, tiles[t]]], axis=1).astype(BF16)
        cs = slice(t * LANE, (t + 1) * LANE)
        return _dot(hcat, wc_ref[t]) + d_ref[:, cs] * u[:, cs]

    ys = [None] * S5_TILES
    bu_tile(0)
    for t in range(S5_TILES):
        if t + 1 < S5_TILES:
            bu_tile(t + 1)
        scan_tile(t)
        if t >= 1:
            ys[t - 1] = y_tile(t - 1)
    ys[S5_TILES - 1] = y_tile(S5_TILES - 1)
    z = _gelu_tanh(jnp.concatenate(ys, axis=1))
    out = z * _sigmoid(_dot(z.astype(BF16), wglu_ref[...]))
    o_ref[...] = _dot(perm_ref[...], out.astype(BF16)).astype(BF16).reshape(o_ref.shape)


def _s5_call(proj, wbu, are, aim, wc, d, wglu, h0r, h0i, *, b, l, tb, tl):
    assert l % tl == 0 and b % tb == 0 and tb == SUBLANE
    perm, permt = _s5_perm(tb, tl)
    h_spec = pl.BlockSpec((tb, S5_LANES), lambda bi, li: (bi, 0))
    params = (perm, permt, wbu, are, aim, wc, d, wglu)
    cb = COL_SU // MIX_W
    if tl == l:
        su, su_spec = proj, pl.BlockSpec((tb * tl, MIX_W), lambda bi, li: (bi, cb))
        o_shape, o_spec = (b * l, MIX_W), pl.BlockSpec((tb * tl, MIX_W), lambda bi, li: (bi, 0))
    else:
        su, su_spec = proj.reshape(b, l, -1), pl.BlockSpec((tb, tl, MIX_W), lambda bi, li: (bi, li, cb))
        o_shape, o_spec = (b, l, MIX_W), pl.BlockSpec((tb, tl, MIX_W), lambda bi, li: (bi, li, 0))
    o, hr, hi = pl.pallas_call(
        functools.partial(_s5_kernel, tb=tb, tl=tl),
        grid=(b // tb, l // tl),
        in_specs=[su_spec] + [_const_spec(a.shape) for a in params] + [h_spec, h_spec],
        out_specs=[o_spec, h_spec, h_spec],
        out_shape=[jax.ShapeDtypeStruct(o_shape, BF16), jax.ShapeDtypeStruct(h0r.shape, F32),
                   jax.ShapeDtypeStruct(h0i.shape, F32)],
        scratch_shapes=[pltpu.VMEM((tl * tb, S5_LANES), F32), pltpu.VMEM((tl * tb, S5_LANES), F32)],
        compiler_params=_cparams(("parallel", "arbitrary")),
        name="s5",
    )(su, *params, h0r, h0i)
    return o.reshape(b * l, MIX_W), hr, hi


def _s5_params(p):
    a_re, a_im = p['s5_a_re'], p['s5_a_im']
    dt = jnp.exp(p['s5_log_dt'])[:, None]
    mag = jnp.exp(dt * a_re)
    ab_re, ab_im = mag * jnp.cos(dt * a_im), mag * jnp.sin(dt * a_im)
    den = a_re * a_re + a_im * a_im
    n_re, n_im = ab_re - 1.0, ab_im
    f_re = (n_re * a_re + n_im * a_im) / den
    f_im = (n_im * a_re - n_re * a_im) / den
    b_re, b_im = p['s5_b_re'], p['s5_b_im']
    bb_re = f_re[..., None] * b_re - f_im[..., None] * b_im
    bb_im = f_re[..., None] * b_im + f_im[..., None] * b_re
    gpt = S5_GROUPS // S5_TILES
    eye = jnp.eye(gpt, dtype=F32)

    def pack_b(bb):
        bt = bb.reshape(S5_TILES, gpt, S5_STATE, S5_GROUP)
        w = jnp.einsum('tgpm,gh->tgmhp', bt, eye)
        return w.reshape(S5_TILES, gpt * S5_GROUP, gpt * S5_STATE)

    def pack_c(cc):
        ct = cc.reshape(S5_TILES, gpt, S5_GROUP, S5_STATE)
        w = jnp.einsum('tgmp,gh->tgphm', ct, eye)
        return w.reshape(S5_TILES, gpt * S5_STATE, gpt * S5_GROUP)

    wbu = jnp.concatenate([pack_b(bb_re), pack_b(bb_im)], axis=2).astype(BF16)
    wc = jnp.concatenate([pack_c(p['s5_c_re']), -pack_c(p['s5_c_im'])], axis=1).astype(BF16)
    return (wbu, ab_re.reshape(1, S5_LANES), ab_im.reshape(1, S5_LANES), wc, p['s5_d'].reshape(1, MIX_W))


def _merge_kernel(x_ref, gla_ref, glb_ref, o0, o1, o2, o3, wb_ref, wo_ref, g_ref, out_ref):
    m = None
    per = GL_BLK // D_MODEL
    for n, o in enumerate((o0, o1, o2, o3)):
        t = _dot(o[...], wb_ref[n])
        gl_ref = (gla_ref, glb_ref)[n // per]
        t = _sigmoid(gl_ref[:, (n % per) * D_MODEL:(n % per + 1) * D_MODEL].astype(F32)) * t
        m = t if m is None else m + t
    mo = _dot(m.astype(BF16), wo_ref[...])
    out_ref[...] = x_ref[...] + _rms(mo, g_ref[...])


def _merge_call(x2, proj, branches, wb, wo, g, *, tm):
    t = x2.shape[0]
    row = lambda i: (i, 0)
    return pl.pallas_call(
        _merge_kernel,
        grid=(t // tm,),
        in_specs=[pl.BlockSpec((tm, D_MODEL), row)]
        + [pl.BlockSpec((tm, GL_BLK), lambda i, c=COL_GL // GL_BLK + k: (i, c)) for k in range(2)]
        + [pl.BlockSpec((tm, MIX_W), row)] * N_BRANCH
        + [_const_spec(wb.shape), _const_spec(wo.shape), _const_spec(g.shape)],
        out_specs=pl.BlockSpec((tm, D_MODEL), row),
        out_shape=jax.ShapeDtypeStruct((t, D_MODEL), F32),
        compiler_params=_cparams(("parallel",)),
        name="merge",
    )(x2, proj, proj, *branches, wb, wo, g)


def _ffn_kernel(x_ref, g1_ref, w1_ref, w2_ref, g2_ref, o_ref, h_ref, acc_ref):
    j = pl.program_id(1)

    @pl.when(j == 0)
    def _():
        h_ref[...] = _rms(x_ref[...], g1_ref[...]).astype(BF16)
        acc_ref[...] = jnp.zeros_like(acc_ref)

    u = jnp.maximum(_dot(h_ref[...], w1_ref[...]), 0.0)
    acc_ref[...] += _dot((u * u).astype(BF16), w2_ref[...])

    @pl.when(j == pl.num_programs(1) - 1)
    def _():
        o_ref[...] = x_ref[...] + _rms(acc_ref[...], g2_ref[...])


def _ffn_call(x2, g1, w1, w2, g2, *, tm, tf):
    t = x2.shape[0]
    return pl.pallas_call(
        _ffn_kernel,
        grid=(t // tm, D_FF // tf),
        in_specs=[
            pl.BlockSpec((tm, D_MODEL), lambda i, j: (i, 0)),
            pl.BlockSpec((1, D_MODEL), lambda i, j: (0, 0)),
            pl.BlockSpec((D_MODEL, tf), lambda i, j: (0, j)),
            pl.BlockSpec((tf, D_MODEL), lambda i, j: (j, 0)),
            pl.BlockSpec((1, D_MODEL), lambda i, j: (0, 0)),
        ],
        out_specs=pl.BlockSpec((tm, D_MODEL), lambda i, j: (i, 0)),
        out_shape=jax.ShapeDtypeStruct((t, D_MODEL), F32),
        scratch_shapes=[pltpu.VMEM((tm, D_MODEL), BF16), pltpu.VMEM((tm, D_MODEL), F32)],
        compiler_params=_cparams(("parallel", "arbitrary")),
        name="ffn",
    )(x2, g1, w1, w2, g2)


W_TILE = MIX_W


def _wprep_kernel(a_ref, b_ref, tail_ref, o_ref, *, first_moved, last_moved, dt_tile):
    j = pl.program_id(0)

    def emit(rows):
        o_ref[...] = rows.T.astype(BF16)

    @pl.when(j < first_moved)
    def _():
        emit(a_ref[...])

    @pl.when((j >= first_moved) & (j < last_moved))
    def _():
        emit(jnp.concatenate([a_ref[SSD_HEADS:, :], b_ref[:SSD_HEADS, :]], axis=0))

    @pl.when(j == last_moved)
    def _():
        emit(jnp.concatenate([a_ref[SSD_HEADS:, :], tail_ref[...]], axis=0))

    @pl.when(j == dt_tile)
    def _():
        d = a_ref.shape[1]
        emit(jnp.concatenate([jnp.broadcast_to(a_ref[h:h + 1, :], (SSD_HEADDIM, d)) for h in range(SSD_HEADS)],
                             axis=0))


def _wprep_call(w_in_all, layer):
    _, d, cols = w_in_all.shape
    o_sdt = 4 * MIX_W + MIX_W + SSD_CONV_DIM
    assert o_sdt == COL_HG and o_sdt % W_TILE == 0 and (cols - SSD_HEADS) % W_TILE == 0 and SSD_HEADS == SUBLANE
    first_moved = o_sdt // W_TILE
    dt_tile = (cols - SSD_HEADS) // W_TILE
    last_moved = dt_tile - 1
    assert (dt_tile + 1) * W_TILE == PROJ_COLS
    wt = jnp.swapaxes(w_in_all, 1, 2)
    tail = wt[layer, dt_tile * W_TILE:, :]
    a_map = lambda j: (layer, jnp.where(j == dt_tile, first_moved, j), 0)
    b_map = lambda j: (layer, jnp.minimum(j + 1, last_moved), 0)
    return pl.pallas_call(
        functools.partial(_wprep_kernel, first_moved=first_moved, last_moved=last_moved, dt_tile=dt_tile),
        grid=(dt_tile + 1,),
        in_specs=[pl.BlockSpec((None, W_TILE, d), a_map), pl.BlockSpec((None, W_TILE, d), b_map),
                  _const_spec(tail.shape)],
        out_specs=pl.BlockSpec((d, W_TILE), lambda j: (0, j)),
        out_shape=jax.ShapeDtypeStruct((d, PROJ_COLS), BF16),
        compiler_params=_cparams(("arbitrary",)),
        name="w_in_layout",
    )(wt, wt, tail)


def _prep_layer(p, lb, w_main):

    def head_lanes(v):
        return jnp.repeat(v.astype(F32), SSD_HEADDIM).reshape(1, MIX_W)

    q = dict(
        w_main=w_main,
        g_pre_mix=p['g_pre_mix'].reshape(1, D_MODEL),
        g_post_mix=p['g_post_mix'].reshape(1, D_MODEL),
        g_pre_ffn=p['g_pre_ffn'].reshape(1, D_MODEL),
        g_post_ffn=p['g_post_ffn'].reshape(1, D_MODEL),
        ret_gn=p['ret_gn'],
        conv_w=p['ssd_conv_w'],
        conv_b=p['ssd_conv_b'].reshape(1, SSD_CONV_DIM),
        a_log=head_lanes(p['ssd_a_log']),
        dt_bias=head_lanes(p['ssd_dt_bias']),
        d_skip=head_lanes(p['ssd_d']),
        ssd_norm=p['ssd_norm'].reshape(1, MIX_W),
        hg_lb=lb.reshape(1, MIX_W),
        hg_norm=p['hg_norm'].reshape(1, HG_DK),
        s5=_s5_params(p),
        w_glu=p['s5_w_glu'].astype(BF16),
        w_branch=p['w_branch'].astype(BF16),
        w_out=p['w_out'].astype(BF16),
        w_ff1=p['w_ff1'].astype(BF16),
        w_ff2=p['w_ff2'].astype(BF16),
    )
    return q


def _layer(x3, states, acc, layer, q, pos, cfg):
    b, l, _ = x3.shape
    x2 = x3.reshape(b * l, D_MODEL)
    tb, tl, c = cfg['tb'], cfg['tl'], cfg['c']
    proj = _proj_call(x2, q['g_pre_mix'], q['w_main'], tm=cfg['tm_proj'], tn=cfg['tn'])
    st = {n: states[n][layer] for n in ('conv', 's5_re', 's5_im')}
    o_ret, s_ret = _ret_call(proj, pos, q['ret_gn'], states['ret'], acc['ret'], layer, b=b, l=l, tb=tb, tl=tl, sl=c,
                             unroll=cfg['u_ret'])
    o_ssd, conv_new, s_ssd = _ssd_call(proj, q['conv_w'], q['conv_b'], q['a_log'], q['dt_bias'], q['d_skip'],
                                       q['ssd_norm'], st['conv'], states['ssd'], acc['ssd'], layer,
                                       b=b, l=l, tb=tb, tl=tl, sl=c, unroll=cfg['u_ssd'])
    o_hg, s_hg = _hg_call(proj, q['hg_lb'], q['hg_norm'], states['hgrn'], acc['hgrn'], layer, b=b, l=l, tb=tb, tl=tl,
                          sl=c, unroll=cfg['u_hg'])
    wbu, are, aim, wc, d5 = q['s5']
    o_s5, s5_re, s5_im = _s5_call(proj, wbu, are, aim, wc, d5, q['w_glu'],
                                  st['s5_re'].reshape(b, S5_LANES), st['s5_im'].reshape(b, S5_LANES),
                                  b=b, l=l, tb=SUBLANE, tl=cfg['s5_tl'])
    x2 = _merge_call(x2, proj, (o_ret, o_ssd, o_hg, o_s5), q['w_branch'], q['w_out'], q['g_post_mix'], tm=cfg['tm_merge'])
    x2 = _ffn_call(x2, q['g_pre_ffn'], q['w_ff1'], q['w_ff2'], q['g_post_ffn'], tm=cfg['tm'], tf=cfg['tf'])
    def push(name, new):
        return new[None] if acc[name] is None else jnp.concatenate([acc[name], new[None]], axis=0)

    new = {'ret': s_ret, 'ssd': s_ssd, 'hgrn': s_hg, 'conv': push('conv', conv_new),
           's5_re': push('s5_re', s5_re.reshape(b, S5_GROUPS, S5_STATE)),
           's5_im': push('s5_im', s5_im.reshape(b, S5_GROUPS, S5_STATE))}
    return x2.reshape(b, l, D_MODEL), new


_NAMES = ('ret', 'ssd', 'conv', 'hgrn', 's5_re', 's5_im')


def _trunk(x, states, layers, pos, cfg):
    acc = {n: None for n in _NAMES}
    for layer, q in enumerate(layers):
        x, acc = _layer(x, states, acc, layer, q, pos, cfg)
    return x, acc


def _group_cfg(b, l):
    if l % CHUNK == 0:
        return dict(tb=1, tl=min(l, 512), c=CHUNK, tm=1024, tm_proj=2048, tn=1536, tm_merge=512, tf=1024,
                    s5_tl=min(l, 64), u_ret=4, u_ssd=8, u_hg=8)
    return dict(tb=SUBLANE, tl=l, c=l, tm=min(b * l, 1024), tm_proj=min(b * l, 1024), tn=1536,
                tm_merge=min(b * l, 512), tf=1024, s5_tl=l, u_ret=1, u_ssd=1, u_hg=1)


def kernel(x_prompt, x_sample, state_ret, state_ssd, state_conv, state_hgrn, state_s5_re, state_s5_im, g_pre_mix, g_post_mix, g_pre_ffn, g_post_ffn, w_in, ret_gn, ssd_conv_w, ssd_conv_b, ssd_a_log, ssd_dt_bias, ssd_d, ssd_norm, hg_lb_logits, hg_norm, s5_a_re, s5_a_im, s5_b_re, s5_b_im, s5_c_re, s5_c_im, s5_d, s5_log_dt, s5_w_glu, w_branch, w_out, w_ff1, w_ff2):
    params = {
        'g_pre_mix': g_pre_mix, 'g_post_mix': g_post_mix, 'g_pre_ffn': g_pre_ffn, 'g_post_ffn': g_post_ffn,
        'ret_gn': ret_gn, 'ssd_conv_w': ssd_conv_w, 'ssd_conv_b': ssd_conv_b,
        'ssd_a_log': ssd_a_log, 'ssd_dt_bias': ssd_dt_bias, 'ssd_d': ssd_d, 'ssd_norm': ssd_norm,
        'hg_norm': hg_norm, 's5_a_re': s5_a_re, 's5_a_im': s5_a_im, 's5_b_re': s5_b_re, 's5_b_im': s5_b_im,
        's5_c_re': s5_c_re, 's5_c_im': s5_c_im, 's5_d': s5_d, 's5_log_dt': s5_log_dt, 's5_w_glu': s5_w_glu,
        'w_branch': w_branch, 'w_out': w_out, 'w_ff1': w_ff1, 'w_ff2': w_ff2,
    }
    depth = w_in.shape[0]
    w = jax.nn.softmax(hg_lb_logits.astype(F32), axis=0)
    lbs = jnp.cumsum(w, axis=0) - w[0]
    layers = [_prep_layer({k: v[i] for k, v in params.items()}, lbs[i], _wprep_call(w_in, i)) for i in range(depth)]

    bp, lp, _ = x_prompt.shape
    bs, ls, _ = x_sample.shape
    zero_states = {
        'ret': jnp.zeros((depth, bp) + state_ret.shape[2:], F32),
        'ssd': jnp.zeros((depth, bp) + state_ssd.shape[2:], F32),
        'conv': jnp.zeros((depth, bp) + state_conv.shape[2:], F32),
        'hgrn': jnp.zeros((depth, bp) + state_hgrn.shape[2:], F32),
        's5_re': jnp.zeros((depth, bp) + state_s5_re.shape[2:], F32),
        's5_im': jnp.zeros((depth, bp) + state_s5_im.shape[2:], F32),
    }
    sample_states = {'ret': state_ret, 'ssd': state_ssd, 'conv': state_conv, 'hgrn': state_hgrn,
                     's5_re': state_s5_re, 's5_im': state_s5_im}
    pos_p = np.arange(lp)
    pos_s = PAST_LEN + np.arange(ls)
    y_p, new_p = _trunk(x_prompt, zero_states, layers, pos_p, _group_cfg(bp, lp))
    y_s, new_s = _trunk(x_sample, sample_states, layers, pos_s, _group_cfg(bs, ls))
    return (y_p, y_s,
            new_p['ret'], new_s['ret'], new_p['ssd'], new_s['ssd'], new_p['conv'], new_s['conv'],
            new_p['hgrn'], new_s['hgrn'], new_p['s5_re'], new_s['s5_re'], new_p['s5_im'], new_s['s5_im'])
```

```python
import functools
import math

import jax
import jax.numpy as jnp
import numpy as np
from jax import lax
from jax.experimental import pallas as pl
from jax.experimental.pallas import tpu as pltpu

F32 = jnp.float32
BF16 = jnp.bfloat16

D_MODEL = 1024
DEPTH = 2
PAST_LEN = 16384
MIX_W = D_MODEL // 2
N_BRANCH = 4
RET_HEADS = 4
RET_DK = MIX_W // RET_HEADS
ROPE_BASE = 10000.0
SSD_HEADDIM = 64
SSD_HEADS = MIX_W // SSD_HEADDIM
SSD_GROUPS = 2
SSD_STATE = 128
SSD_CONV = 4
SSD_CONV_DIM = MIX_W + 2 * SSD_GROUPS * SSD_STATE
SSD_PAIRS = SSD_HEADS // 2
HG_HEADS = 4
HG_DK = MIX_W // HG_HEADS
S5_GROUP = 16
S5_GROUPS = MIX_W // S5_GROUP
S5_STATE = 64
S5_LANES = S5_GROUPS * S5_STATE
D_FF = 4 * D_MODEL
GATE_COLS = N_BRANCH * D_MODEL
CHUNK = 64
CR = 64
EPS = 1e-6

LANE = 128
SUBLANE = 8
VMEM_LIMIT = 48 * 1024 * 1024
MIXFFN_VMEM_LIMIT = 56 * 1024 * 1024

COL_RET = 0
COL_SZ = COL_RET + 4 * MIX_W
COL_XS = COL_SZ + MIX_W
COL_BC = COL_XS + MIX_W
COL_HG = COL_BC + MIX_W
COL_SU = COL_HG + 4 * MIX_W
COL_GL = COL_SU + MIX_W
COL_DT = COL_GL + GATE_COLS
PROJ_COLS = COL_DT + MIX_W
GL_BLK = GATE_COLS // 2
assert COL_GL % GL_BLK == 0


def _sigmoid(x):
    return 1.0 / (1.0 + jnp.exp(-x))


def _silu(x):
    return x * _sigmoid(x)


def _softplus(x):
    return jnp.maximum(x, 0.0) + jnp.log1p(jnp.exp(-jnp.abs(x)))


def _gelu_tanh(x):
    c = math.sqrt(2.0 / math.pi)
    return 0.5 * x * (1.0 + jnp.tanh(c * (x + 0.044715 * (x * x * x))))


def _rms(x, g):
    ms = jnp.mean(x * x, axis=-1, keepdims=True)
    return x * lax.rsqrt(ms + EPS) * g


def _dot(a, b):
    return jnp.dot(a, b, preferred_element_type=F32)


def _dot_nt(a, b):
    return lax.dot_general(a, b, (((1,), (1,)), ((), ())), preferred_element_type=F32)


def _split3(a):
    hi = a.astype(BF16)
    r1 = a - hi.astype(F32)
    mid = r1.astype(BF16)
    lo = (r1 - mid.astype(F32)).astype(BF16)
    return hi, mid, lo


def _sel_l(m, a):
    hi, mid, lo = _split3(a)
    return (_dot(m, lo) + _dot(m, mid)) + _dot(m, hi)


def _sel2_l(m, a):
    hi = a.astype(BF16)
    mid = (a - hi.astype(F32)).astype(BF16)
    return _dot(m, mid) + _dot(m, hi)


def _sel_r(a, m):
    hi, mid, lo = _split3(a)
    return (_dot(lo, m) + _dot(mid, m)) + _dot(hi, m)


def _cparams(sem):
    return pltpu.CompilerParams(dimension_semantics=sem, vmem_limit_bytes=VMEM_LIMIT)


def _const_spec(shape):
    nd = len(shape)
    return pl.BlockSpec(shape, lambda *_: (0,) * nd)


def _state_io(s_all, prev, layer, tb):
    dims = tuple(s_all.shape[2:])
    zeros = (0,) * len(dims)
    in_spec = pl.BlockSpec((None, tb) + dims, lambda bi, li: (layer, bi) + zeros)
    if prev is None:
        nprev, prev, prev_spec = 0, s_all, in_spec
    else:
        nprev = prev.shape[0]
        prev_spec = pl.BlockSpec((nprev, tb) + dims, lambda bi, li: (0, bi) + zeros)
    out_spec = pl.BlockSpec((nprev + 1, tb) + dims, lambda bi, li: (0, bi) + zeros)
    out_shape = jax.ShapeDtypeStruct((nprev + 1, s_all.shape[1]) + dims, F32)
    return nprev, prev, in_spec, prev_spec, out_spec, out_shape


def _proj_kernel(x_ref, g_ref, w_ref, o_ref, h_ref):
    @pl.when(pl.program_id(1) == 0)
    def _():
        h_ref[...] = _rms(x_ref[...], g_ref[...]).astype(BF16)

    o_ref[...] = _dot(h_ref[...], w_ref[...]).astype(BF16)


def _proj_call(x2, g, w, *, tm, tn):
    t, d = x2.shape
    n = w.shape[1]
    return pl.pallas_call(
        _proj_kernel,
        grid=(t // tm, n // tn),
        in_specs=[
            pl.BlockSpec((tm, d), lambda i, j: (i, 0)),
            pl.BlockSpec((1, d), lambda i, j: (0, 0)),
            pl.BlockSpec((d, tn), lambda i, j: (0, j)),
        ],
        out_specs=pl.BlockSpec((tm, tn), lambda i, j: (i, j)),
        out_shape=jax.ShapeDtypeStruct((t, n), BF16),
        scratch_shapes=[pltpu.VMEM((tm, d), BF16)],
        compiler_params=_cparams(("parallel", "arbitrary")),
        name="in_proj",
    )(x2, g, w)


def _ret_consts(sl):
    nseq = CR // sl
    h = np.arange(RET_HEADS, dtype=np.float64)
    log_g = np.log(1.0 - 2.0 ** (-5.0 - h))
    idx = np.arange(sl, dtype=np.float64)
    diff = idx[:, None] - idx[None, :]
    dmat = np.where(diff >= 0, np.exp(log_g[:, None, None] * np.maximum(diff, 0.0)), 0.0)
    dmat = np.stack([np.kron(np.eye(nseq), d) for d in dmat])
    q_dec = np.tile(np.exp(log_g[:, None] * (idx[None, :] + 1.0)), (1, nseq))
    k_dec = np.tile(np.exp(log_g[:, None] * (sl - 1.0 - idx[None, :])), (1, nseq))
    chunk_dec = np.exp(log_g * sl)
    qd = np.broadcast_to(q_dec[:, :, None], (RET_HEADS, CR, LANE))
    kd = np.broadcast_to(k_dec[:, :, None], (RET_HEADS, CR, LANE))
    return (jnp.asarray(dmat, F32), jnp.asarray(qd, F32), jnp.asarray(kd, F32),
            tuple(float(v) for v in chunk_dec))


def _rope_tables(pos):
    half = RET_DK // 2
    inv = ROPE_BASE ** (-np.arange(half, dtype=np.float64) / half)
    ang = np.asarray(pos, np.float64)[:, None] * inv[None, :]
    cos, sin = np.cos(ang), np.sin(ang)
    return (jnp.asarray(np.concatenate([cos, cos], axis=1), F32),
            jnp.asarray(np.concatenate([-sin, sin], axis=1), F32))


def _ret_kernel(q_ref, k_ref, v_ref, g_ref, cos_ref, sin_ref, dmat_ref, qd_ref, kd_ref, gn_ref, s0_ref, prev_ref,
                o_ref, so_ref, *, tb, tl, sl, cdec, nprev, unroll):
    st_ref = so_ref.at[nprev]

    @pl.when(pl.program_id(1) == 0)
    def _():
        st_ref[...] = s0_ref[...]
        if nprev:
            so_ref[0:nprev] = prev_ref[...]

    nseq = CR // sl
    scale = RET_DK ** -0.5

    def body(pc, carry):
        r0 = pl.multiple_of(pc * CR, CR)
        t0 = r0 if nseq == 1 else 0
        cos = cos_ref[pl.ds(t0, CR), :]
        sin = sin_ref[pl.ds(t0, CR), :]
        heads = [slice(h * LANE, (h + 1) * LANE) for h in range(RET_HEADS)]
        qrs, krs = [], []
        for hs in heads:
            q = q_ref[pl.ds(r0, CR), hs].astype(F32)
            k = k_ref[pl.ds(r0, CR), hs].astype(F32)
            qrs.append(q * cos + pltpu.roll(q, LANE // 2, axis=1) * sin)
            krs.append((k * cos + pltpu.roll(k, LANE // 2, axis=1) * sin) * scale)
        vs = [v_ref[pl.ds(r0, CR), hs] for hs in heads]
        atts = [(_dot_nt(qrs[h].astype(BF16), krs[h].astype(BF16)) * dmat_ref[h]).astype(BF16)
                for h in range(RET_HEADS)]
        outs = []
        for h in range(RET_HEADS):
            o = _dot(atts[h], vs[h])
            qd = qrs[h] * qd_ref[h]
            kd = krs[h] * kd_ref[h]
            ois = []
            for s in range(nseq):
                seq = pc * nseq + s if nseq > 1 else 0
                rs = slice(s * sl, (s + 1) * sl)
                st = st_ref[seq, h]
                ois.append(_dot(qd[rs].astype(BF16), st.astype(BF16)))
                v_s = vs[h] if nseq == 1 else vs[h].astype(F32)[rs].astype(BF16)
                st_ref[seq, h] = st * cdec[h] + _dot(kd[rs].T.astype(BF16), v_s)
            outs.append(o + (ois[0] if nseq == 1 else jnp.concatenate(ois, axis=0)))
        for h, hs in enumerate(heads):
            o = outs[h]
            mu = jnp.mean(o, axis=-1, keepdims=True)
            oc = o - mu
            var = jnp.mean(oc * oc, axis=-1, keepdims=True)
            on = oc * lax.rsqrt(var + EPS) * gn_ref[h:h + 1, :]
            o_ref[pl.ds(r0, CR), hs] = (on * _silu(g_ref[pl.ds(r0, CR), hs].astype(F32))).astype(BF16)
        return carry

    lax.fori_loop(0, tb * tl // CR, body, 0, unroll=unroll)


def _seq_grid(b, l, tb, tl):
    assert b % tb == 0 and l % tl == 0 and (tb == 1 or tl == l)
    nl = l // tl
    rows = tb * tl

    def row_map(col_blk):
        return lambda bi, li: (bi * nl + li, col_blk)

    return (b // tb, nl), rows, row_map


def _ret_call(proj, pos, gn, s_all, prev, layer, *, b, l, tb, tl, sl, unroll):
    grid, rows, row_map = _seq_grid(b, l, tb, tl)
    assert rows % CR == 0 and CR % sl == 0 and (sl == CR and tb == 1 or sl == tl)
    dmat, qd, kd, cdec = _ret_consts(sl)
    cos, sin = _rope_tables(pos)
    if sl < CR:
        cos, sin = jnp.tile(cos, (CR // sl, 1)), jnp.tile(sin, (CR // sl, 1))
        tab_spec = _const_spec(cos.shape)
    else:
        tab_spec = pl.BlockSpec((tl, LANE), lambda bi, li: (li, 0))
    cb = COL_RET // MIX_W
    nprev, prev, s_spec, prev_spec, so_spec, so_shape = _state_io(s_all, prev, layer, tb)
    return pl.pallas_call(
        functools.partial(_ret_kernel, tb=tb, tl=tl, sl=sl, cdec=cdec, nprev=nprev, unroll=unroll),
        grid=grid,
        in_specs=[pl.BlockSpec((rows, MIX_W), row_map(cb + i)) for i in range(4)] + [
            tab_spec, tab_spec,
            _const_spec(dmat.shape), _const_spec(qd.shape), _const_spec(kd.shape), _const_spec(gn.shape),
            s_spec, prev_spec,
        ],
        out_specs=[pl.BlockSpec((rows, MIX_W), row_map(0)), so_spec],
        out_shape=[jax.ShapeDtypeStruct((b * l, MIX_W), BF16), so_shape],
        compiler_params=_cparams(("parallel", "arbitrary")),
        name="retention",
    )(proj, proj, proj, proj, cos, sin, dmat, qd, kd, gn, s_all, prev)


def _ssd_consts(sl):
    nseq = CR // sl
    tril = np.kron(np.eye(nseq), np.tril(np.ones((sl, sl))))
    sumall = np.kron(np.eye(nseq), np.ones((sl, sl)))
    ones = np.ones((CR, CR))
    i = np.arange(CR)[:, None]
    j = np.arange(MIX_W)[None, :] % SSD_HEADDIM
    eye = (i == j).astype(np.float64)
    mask = ((i // sl == j // sl) & (i >= j)).astype(np.float64)
    return (jnp.asarray(np.concatenate([tril, sumall], axis=0), BF16), jnp.asarray(ones, BF16),
            jnp.asarray(eye, F32), jnp.asarray(mask, F32))


def _ssd_kernel(z_ref, xs_ref, bc_ref, dtx_ref, cw_ref, cb_ref, alog_ref, dtb_ref, dsk_ref, nw_ref,
                mcum_ref, ones_ref, eye_ref, mask_ref, cs0_ref, s0_ref, prev_ref,
                o_ref, cs_ref, so_ref, xpad, st, *, tb, tl, sl, nprev, unroll):
    li = pl.program_id(1)
    nl = pl.num_programs(1)
    nseq = CR // sl
    kc = SSD_CONV - 1
    off = SUBLANE - kc
    grp = 2 * SUBLANE

    @pl.when(li == 0)
    def _():
        for s in range(tb):
            for p in range(SSD_PAIRS):
                st[s, p] = s0_ref[s, p].T
        if nprev:
            so_ref[0:nprev] = prev_ref[...]
        if nseq == 1:
            xpad[0:SUBLANE, :] = jnp.zeros((SUBLANE, SSD_CONV_DIM), F32)
            xpad[off:SUBLANE, :] = cs0_ref[0]

    if nseq == 1:
        xpad[SUBLANE:SUBLANE + tl, 0:MIX_W] = xs_ref[...].astype(F32)
        xpad[SUBLANE:SUBLANE + tl, MIX_W:2 * MIX_W] = bc_ref[...].astype(F32)
    else:
        xs_new = xs_ref[...].astype(F32)
        bc_new = bc_ref[...].astype(F32)
        for s in range(tb):
            xpad[s * grp:s * grp + SUBLANE, :] = jnp.zeros((SUBLANE, SSD_CONV_DIM), F32)
            xpad[s * grp + off:s * grp + SUBLANE, :] = cs0_ref[s]
            xpad[s * grp + SUBLANE:(s + 1) * grp, 0:MIX_W] = xs_new[s * sl:(s + 1) * sl, :]
            xpad[s * grp + SUBLANE:(s + 1) * grp, MIX_W:2 * MIX_W] = bc_new[s * sl:(s + 1) * sl, :]
            cs_ref[s] = xpad[(s + 1) * grp - kc:(s + 1) * grp, :]

    a_neg = -jnp.exp(alog_ref[...])
    half = lax.broadcasted_iota(jnp.int32, (CR, LANE), 1) < SSD_HEADDIM

    def body(pc, carry):
        r0 = pl.multiple_of(pc * CR, CR)
        if nseq == 1:
            win = xpad[pl.ds(r0, CR + SUBLANE), :]
            take = lambda a: a[SUBLANE:, :]
        else:
            win = xpad[pl.ds(pl.multiple_of(pc * nseq * grp, nseq * grp), nseq * grp), :]
            take = lambda a: a.reshape(nseq, grp, SSD_CONV_DIM)[:, SUBLANE:, :].reshape(CR, SSD_CONV_DIM)
        acc = cb_ref[...] + take(pltpu.roll(win, kc, axis=0)) * cw_ref[0:1, :]
        for j in range(1, kc):
            acc = acc + take(pltpu.roll(win, kc - j, axis=0)) * cw_ref[j:j + 1, :]
        xc = _silu(acc + take(win) * cw_ref[kc:kc + 1, :])
        xs = xc[:, 0:MIX_W]
        bm = xc[:, MIX_W:MIX_W + SSD_GROUPS * SSD_STATE]
        cm = xc[:, MIX_W + SSD_GROUPS * SSD_STATE:2 * MIX_W]

        dtx = _softplus(dtx_ref[pl.ds(r0, CR), :].astype(F32) + dtb_ref[...])
        both = _sel_l(mcum_ref[...], dtx * a_neg)
        cum, clast = both[0:CR], both[CR:2 * CR]
        ecum = jnp.exp(cum)
        dend = jnp.exp(clast - cum)
        elast = jnp.exp(clast)
        rr = _sel_l(ones_ref[...], cum * eye_ref[...])
        msk = mask_ref[...]
        lmat = jnp.exp(jnp.where(msk > 0, cum - rr, 0.0)) * msk
        zz = z_ref[pl.ds(r0, CR), :].astype(F32)
        npp = SSD_PAIRS // SSD_GROUPS
        bm_gs = [bm[:, g * SSD_STATE:(g + 1) * SSD_STATE] for g in range(SSD_GROUPS)]
        cm_gs = [cm[:, g * SSD_STATE:(g + 1) * SSD_STATE] for g in range(SSD_GROUPS)]
        cm_gbs = [c_.astype(BF16) for c_ in cm_gs]
        cb2s = [_dot_nt(cm_gbs[g], jnp.concatenate([bm_gs[g], bm_gs[g]], axis=0).astype(BF16))
                for g in range(SSD_GROUPS)]
        pss = [slice(p * LANE, (p + 1) * LANE) for p in range(SSD_PAIRS)]
        xdts = [xs[:, ps] * dtx[:, ps] for ps in pss]
        xsts = [jnp.concatenate([jnp.where(half, x_, 0.0), jnp.where(half, 0.0, x_)], axis=0).astype(BF16)
                for x_ in xdts]
        ys = [_dot((cb2s[p // npp] * lmat[:, pss[p]]).astype(BF16), xsts[p]) for p in range(SSD_PAIRS)]
        xds = [xdts[p] * dend[:, pss[p]] for p in range(SSD_PAIRS)]
        yis = []
        for p in range(SSD_PAIRS):
            g = p // npp
            parts = []
            for s in range(nseq):
                seq = pc * nseq + s if nseq > 1 else 0
                if nseq == 1:
                    cm_s, bm_s, xd_s = cm_gbs[g], bm_gs[g], xds[p]
                else:
                    rs = slice(s * sl, (s + 1) * sl)
                    cm_s, bm_s, xd_s = cm_gs[g][rs].astype(BF16), bm_gs[g][rs], xds[p][rs]
                stp = st[seq, p]
                parts.append(_dot(cm_s, stp.astype(BF16)))
                st[seq, p] = stp * elast[s * sl:s * sl + 1, pss[p]] + _dot(bm_s.T.astype(BF16), xd_s.astype(BF16))
            yis.append(parts[0] if nseq == 1 else jnp.concatenate(parts, axis=0))
        gw = SSD_HEADDIM * SSD_HEADS // SSD_GROUPS
        for g in range(SSD_GROUPS):
            yg = jnp.concatenate([ys[p] + yis[p] * ecum[:, pss[p]] + xs[:, pss[p]] * dsk_ref[:, pss[p]]
                                  for p in range(g * npp, (g + 1) * npp)], axis=1)
            zs = slice(g * gw, (g + 1) * gw)
            o_ref[pl.ds(r0, CR), zs] = _rms(yg * _silu(zz[:, zs]), nw_ref[:, zs]).astype(BF16)
        return carry

    lax.fori_loop(0, tb * tl // CR, body, 0, unroll=unroll)

    if nseq == 1:
        xpad[0:SUBLANE, :] = xpad[tl:tl + SUBLANE, :]

    @pl.when(li == nl - 1)
    def _():
        if nseq == 1:
            cs_ref[0] = xpad[tl + off:tl + SUBLANE, :]
        for s in range(tb):
            for p in range(SSD_PAIRS):
                so_ref[nprev, s, p] = st[s, p].T


def _ssd_call(proj, cw, cb, alog, dtb, dsk, nw, cs0, s_all, prev, layer, *, b, l, tb, tl, sl, unroll):
    grid, rows, row_map = _seq_grid(b, l, tb, tl)
    assert rows % CR == 0 and CR % sl == 0 and (sl == CR and tb == 1 or sl == tl == SUBLANE)
    consts = _ssd_consts(sl)
    pair = lambda a: a.reshape(a.shape[:2] + (SSD_PAIRS, LANE, SSD_STATE))
    s_pairs = pair(s_all)
    nprev, prev, s_spec, prev_spec, so_spec, so_shape = _state_io(s_pairs, None if prev is None else pair(prev),
                                                                  layer, tb)
    cs_spec = pl.BlockSpec((tb, SSD_CONV - 1, SSD_CONV_DIM), lambda bi, li: (bi, 0, 0))
    params = (cw, cb, alog, dtb, dsk, nw)
    xpad_rows = SUBLANE + tl if sl == CR else tb * 2 * SUBLANE
    o, cs, so = pl.pallas_call(
        functools.partial(_ssd_kernel, tb=tb, tl=tl, sl=sl, nprev=nprev, unroll=unroll),
        grid=grid,
        in_specs=[pl.BlockSpec((rows, MIX_W), row_map(cb_)) for cb_ in
                  (COL_SZ // MIX_W, COL_XS // MIX_W, COL_BC // MIX_W, COL_DT // MIX_W)]
        + [_const_spec(a.shape) for a in params + consts] + [cs_spec, s_spec, prev_spec],
        out_specs=[pl.BlockSpec((rows, MIX_W), row_map(0)), cs_spec, so_spec],
        out_shape=[jax.ShapeDtypeStruct((b * l, MIX_W), BF16), jax.ShapeDtypeStruct(cs0.shape, F32), so_shape],
        scratch_shapes=[pltpu.VMEM((xpad_rows, SSD_CONV_DIM), F32),
                        pltpu.VMEM((tb, SSD_PAIRS, SSD_STATE, LANE), F32)],
        compiler_params=_cparams(("parallel", "arbitrary")),
        name="ssd",
    )(proj, proj, proj, proj, *params, *consts, cs0, s_pairs, prev)
    return o, cs, so.reshape((nprev + 1,) + s_all.shape[1:])


def _hg_levels(sl):
    ms = []
    m = sl // 2
    while m >= 1:
        ms.append(m)
        m //= 2
    return ms


def _hg_consts(sl):
    nseq = CR // sl
    tril = np.kron(np.eye(nseq), np.tril(np.ones((sl, sl))))
    sumall = np.kron(np.eye(nseq), np.ones((sl, sl)))
    mats = []
    masks = []
    i = np.arange(CR)
    for m in _hg_levels(sl):
        ref = (i // (2 * m)) * (2 * m) + m - 1
        if m > 1:
            mats.append(np.abs(tril - tril[ref]))
        same = (i[:, None] // (2 * m)) == (i[None, :] // (2 * m))
        qside = ((i // m) % 2 == 1)[:, None]
        kside = ((i // m) % 2 == 0)[None, :]
        masks.append((same & qside & kside).astype(np.float64))
    return (jnp.asarray(np.concatenate([tril, sumall], axis=0), BF16), jnp.asarray(np.concatenate(mats, axis=0), BF16),
            jnp.asarray(np.stack(masks), F32))


def _hg_kernel(q_ref, f_ref, i_ref, g_ref, lb_ref, nw_ref, mcum_ref, mlev_ref, mask_ref, s0_ref, prev_ref,
               o_ref, so_ref, st, qe_s, ke_s, qg_s, kd_s, el_s, dv_s, *, tb, tl, sl, nprev, unroll):
    li = pl.program_id(1)
    nl = pl.num_programs(1)

    @pl.when(li == 0)
    def _():
        for s in range(tb):
            for h in range(HG_HEADS):
                st[s, h] = s0_ref[s, h].T
        if nprev:
            so_ref[0:nprev] = prev_ref[...]

    nseq = CR // sl
    levels = _hg_levels(sl)

    nlev = len(levels)
    odd_row = (lax.broadcasted_iota(jnp.int32, (CR, MIX_W), 0) & 1) == 1

    def stage_a(pc, carry):
        r0 = pl.multiple_of(pc * CR, CR)
        lb = lb_ref[...]
        q = _silu(q_ref[pl.ds(r0, CR), :].astype(F32))
        sig = _sigmoid(f_ref[pl.ds(r0, CR), :].astype(F32))
        fdec = lb + (1.0 - lb) * sig
        logf = jnp.log(fdec)
        k = (1.0 - lb) * (1.0 - sig)
        v = i_ref[pl.ds(r0, CR), :].astype(F32)
        cums = _sel_l(mcum_ref[...], logf)
        gc = cums[0:CR]
        glast = cums[CR:2 * CR]
        if nlev > 1:
            dl = _sel2_l(mlev_ref[...], logf)
        qb, kb = q.astype(BF16), k.astype(BF16)
        for n in range(nlev):
            e = jnp.exp(dl[n * CR:(n + 1) * CR]) if n < nlev - 1 else jnp.where(odd_row, fdec, 1.0)
            e = e.astype(BF16)
            qe_s[pc, n] = qb * e
            ke_s[pc, n] = kb * e
        qg_s[pc] = (q * jnp.exp(gc)).astype(BF16)
        kd_s[pc] = (k * jnp.exp(glast - gc)).astype(BF16)
        el_s[pc] = jnp.exp(glast)
        qk = q * k
        dv_s[pc] = jnp.concatenate(
            [jnp.sum(qk[:, h * LANE:(h + 1) * LANE], axis=-1, keepdims=True) * v[:, h * LANE:(h + 1) * LANE]
             for h in range(HG_HEADS)], axis=1)
        return carry

    def stage_b(pc, carry):
        r0 = pl.multiple_of(pc * CR, CR)
        heads = [slice(h * LANE, (h + 1) * LANE) for h in range(HG_HEADS)]
        vs = [i_ref[pl.ds(r0, CR), hs] for hs in heads]
        atts = []
        for hs in heads:
            att = None
            for n in range(nlev):
                a = _dot_nt(qe_s[pc, n, :, hs], ke_s[pc, n, :, hs]) * mask_ref[n]
                att = a if att is None else att + a
            atts.append(att.astype(BF16))
        outs = []
        for h, hs in enumerate(heads):
            o = _dot(atts[h], vs[h]) + dv_s[pc, :, hs]
            ois = []
            for s in range(nseq):
                seq = pc * nseq + s if nseq > 1 else 0
                rs = slice(s * sl, (s + 1) * sl)
                stt = st[seq, h]
                if nseq == 1:
                    qg_r, kd_r, v_r = qg_s[pc, :, hs], kd_s[pc, :, hs], vs[h].astype(F32)
                else:
                    qg_r = qg_s[pc, :, hs].astype(F32)[rs].astype(BF16)
                    kd_r = kd_s[pc, :, hs].astype(F32)[rs].astype(BF16)
                    v_r = vs[h].astype(F32)[rs]
                ois.append(_dot_nt(qg_r, stt.astype(BF16)))
                st[seq, h] = stt * el_s[pc, s * sl:s * sl + 1, hs] + _dot(v_r.T.astype(BF16), kd_r)
            outs.append(o + (ois[0] if nseq == 1 else jnp.concatenate(ois, axis=0)))
        for h, hs in enumerate(heads):
            gate = _sigmoid(g_ref[pl.ds(r0, CR), hs].astype(F32))
            o_ref[pl.ds(r0, CR), hs] = (_rms(outs[h], nw_ref[...]) * gate).astype(BF16)
        return carry

    ngrp = tb * tl // CR
    lax.fori_loop(0, ngrp, stage_a, 0, unroll=unroll)
    lax.fori_loop(0, ngrp, stage_b, 0, unroll=unroll)

    @pl.when(li == nl - 1)
    def _():
        for s in range(tb):
            for h in range(HG_HEADS):
                so_ref[nprev, s, h] = st[s, h].T


def _hg_call(proj, lb, nw, s_all, prev, layer, *, b, l, tb, tl, sl, unroll):
    grid, rows, row_map = _seq_grid(b, l, tb, tl)
    assert rows % CR == 0 and CR % sl == 0 and (sl == CR and tb == 1 or sl == tl)
    consts = _hg_consts(sl)
    nlev = len(_hg_levels(sl))
    cb = COL_HG // MIX_W
    nprev, prev, s_spec, prev_spec, so_spec, so_shape = _state_io(s_all, prev, layer, tb)
    stage = lambda dt, *lead: pltpu.VMEM((rows // CR,) + lead + (CR, MIX_W), dt)
    return pl.pallas_call(
        functools.partial(_hg_kernel, tb=tb, tl=tl, sl=sl, nprev=nprev, unroll=unroll),
        grid=grid,
        in_specs=[pl.BlockSpec((rows, MIX_W), row_map(cb + i)) for i in range(4)]
        + [_const_spec(a.shape) for a in (lb, nw) + consts] + [s_spec, prev_spec],
        out_specs=[pl.BlockSpec((rows, MIX_W), row_map(0)), so_spec],
        out_shape=[jax.ShapeDtypeStruct((b * l, MIX_W), BF16), so_shape],
        scratch_shapes=[pltpu.VMEM((tb, HG_HEADS, HG_DK, HG_DK), F32), stage(BF16, nlev), stage(BF16, nlev),
                        stage(BF16), stage(BF16), stage(F32), stage(F32)],
        compiler_params=_cparams(("parallel", "arbitrary")),
        name="hgrn2",
    )(proj, proj, proj, proj, lb, nw, *consts, s_all, prev)


S5_TILES = MIX_W // LANE
S5_TLANES = S5_LANES // S5_TILES


def _s5_perm(tb, tl):
    rows = tb * tl
    p = np.zeros((rows, rows))
    t, s = np.meshgrid(np.arange(tl), np.arange(tb), indexing='ij')
    p[(s * tl + t).ravel(), (t * tb + s).ravel()] = 1.0
    return jnp.asarray(p, BF16), jnp.asarray(p.T, BF16)


def _s5_kernel(su_ref, perm_ref, permt_ref, wbu_ref, are_ref, aim_ref, wc_ref, d_ref, wglu_ref, h0r_ref, h0i_ref,
               o_ref, hr_ref, hi_ref, bur, bui, *, tb, tl):
    @pl.when(pl.program_id(1) == 0)
    def _():
        hr_ref[...] = h0r_ref[...]
        hi_ref[...] = h0i_ref[...]

    rows = tl * tb
    u = _dot(permt_ref[...], su_ref[...].reshape(rows, MIX_W))
    tiles = [slice(t * S5_TLANES, (t + 1) * S5_TLANES) for t in range(S5_TILES)]

    def bu_tile(t):
        bu = _dot(u[:, t * LANE:(t + 1) * LANE].astype(BF16), wbu_ref[t])
        bur[:, tiles[t]] = bu[:, :S5_TLANES]
        bui[:, tiles[t]] = bu[:, S5_TLANES:]

    def scan_tile(t):
        ls = tiles[t]
        ar = jnp.broadcast_to(are_ref[:, ls], (tb, S5_TLANES))
        ai = jnp.broadcast_to(aim_ref[:, ls], (tb, S5_TLANES))
        hr, hi = hr_ref[:, ls], hi_ref[:, ls]
        for i in range(tl):
            rs = slice(i * tb, (i + 1) * tb)
            hr, hi = (ar * hr - ai * hi) + bur[rs, ls], (ar * hi + ai * hr) + bui[rs, ls]
            bur[rs, ls] = hr
            bui[rs, ls] = hi
        hr_ref[:, ls] = hr
        hi_ref[:, ls] = hi

    def y_tile(t):
        hcat = jnp.concatenate([bur[:, tiles[t]], bui[:, tiles[t]]], axis=1).astype(BF16)
        cs = slice(t * LANE, (t + 1) * LANE)
        return _dot(hcat, wc_ref[t]) + d_ref[:, cs] * u[:, cs]

    ys = [None] * S5_TILES
    bu_tile(0)
    for t in range(S5_TILES):
        if t + 1 < S5_TILES:
            bu_tile(t + 1)
        scan_tile(t)
        if t >= 1:
            ys[t - 1] = y_tile(t - 1)
    ys[S5_TILES - 1] = y_tile(S5_TILES - 1)
    z = _gelu_tanh(jnp.concatenate(ys, axis=1))
    out = z * _sigmoid(_dot(z.astype(BF16), wglu_ref[...]))
    o_ref[...] = _dot(perm_ref[...], out.astype(BF16)).astype(BF16).reshape(o_ref.shape)


def _s5_call(proj, wbu, are, aim, wc, d, wglu, h0r, h0i, *, b, l, tb, tl):
    assert l % tl == 0 and b % tb == 0 and tb == SUBLANE
    perm, permt = _s5_perm(tb, tl)
    h_spec = pl.BlockSpec((tb, S5_LANES), lambda bi, li: (bi, 0))
    params = (perm, permt, wbu, are, aim, wc, d, wglu)
    cb = COL_SU // MIX_W
    if tl == l:
        su, su_spec = proj, pl.BlockSpec((tb * tl, MIX_W), lambda bi, li: (bi, cb))
        o_shape, o_spec = (b * l, MIX_W), pl.BlockSpec((tb * tl, MIX_W), lambda bi, li: (bi, 0))
    else:
        su, su_spec = proj.reshape(b, l, -1), pl.BlockSpec((tb, tl, MIX_W), lambda bi, li: (bi, li, cb))
        o_shape, o_spec = (b, l, MIX_W), pl.BlockSpec((tb, tl, MIX_W), lambda bi, li: (bi, li, 0))
    o, hr, hi = pl.pallas_call(
        functools.partial(_s5_kernel, tb=tb, tl=tl),
        grid=(b // tb, l // tl),
        in_specs=[su_spec] + [_const_spec(a.shape) for a in params] + [h_spec, h_spec],
        out_specs=[o_spec, h_spec, h_spec],
        out_shape=[jax.ShapeDtypeStruct(o_shape, BF16), jax.ShapeDtypeStruct(h0r.shape, F32),
                   jax.ShapeDtypeStruct(h0i.shape, F32)],
        scratch_shapes=[pltpu.VMEM((tl * tb, S5_LANES), F32), pltpu.VMEM((tl * tb, S5_LANES), F32)],
        compiler_params=_cparams(("parallel", "arbitrary")),
        name="s5",
    )(su, *params, h0r, h0i)
    return o.reshape(b * l, MIX_W), hr, hi


def _s5_params(p):
    a_re, a_im = p['s5_a_re'], p['s5_a_im']
    dt = jnp.exp(p['s5_log_dt'])[:, None]
    mag = jnp.exp(dt * a_re)
    ab_re, ab_im = mag * jnp.cos(dt * a_im), mag * jnp.sin(dt * a_im)
    den = a_re * a_re + a_im * a_im
    n_re, n_im = ab_re - 1.0, ab_im
    f_re = (n_re * a_re + n_im * a_im) / den
    f_im = (n_im * a_re - n_re * a_im) / den
    b_re, b_im = p['s5_b_re'], p['s5_b_im']
    bb_re = f_re[..., None] * b_re - f_im[..., None] * b_im
    bb_im = f_re[..., None] * b_im + f_im[..., None] * b_re
    gpt = S5_GROUPS // S5_TILES
    eye = jnp.eye(gpt, dtype=F32)

    def pack_b(bb):
        bt = bb.reshape(S5_TILES, gpt, S5_STATE, S5_GROUP)
        w = jnp.einsum('tgpm,gh->tgmhp', bt, eye)
        return w.reshape(S5_TILES, gpt * S5_GROUP, gpt * S5_STATE)

    def pack_c(cc):
        ct = cc.reshape(S5_TILES, gpt, S5_GROUP, S5_STATE)
        w = jnp.einsum('tgmp,gh->tgphm', ct, eye)
        return w.reshape(S5_TILES, gpt * S5_STATE, gpt * S5_GROUP)

    wbu = jnp.concatenate([pack_b(bb_re), pack_b(bb_im)], axis=2).astype(BF16)
    wc = jnp.concatenate([pack_c(p['s5_c_re']), -pack_c(p['s5_c_im'])], axis=1).astype(BF16)
    return (wbu, ab_re.reshape(1, S5_LANES), ab_im.reshape(1, S5_LANES), wc, p['s5_d'].reshape(1, MIX_W))


def _mixffn_kernel(x_ref, gla_ref, glb_ref, o0, o1, o2, o3, wb_ref, wo_ref, g_mix_ref, g1_ref, w1_ref, w2_ref, g2_ref,
                   out_ref, *, tf):
    m = None
    per = GL_BLK // D_MODEL
    for n, o in enumerate((o0, o1, o2, o3)):
        t = _dot(o[...], wb_ref[n])
        gl_ref = (gla_ref, glb_ref)[n // per]
        t = _sigmoid(gl_ref[:, (n % per) * D_MODEL:(n % per + 1) * D_MODEL].astype(F32)) * t
        m = t if m is None else m + t
    x1 = x_ref[...] + _rms(_dot(m.astype(BF16), wo_ref[...]), g_mix_ref[...])
    h = _rms(x1, g1_ref[...]).astype(BF16)
    acc = None
    for c in range(D_FF // tf):
        u = jnp.maximum(_dot(h, w1_ref[:, c * tf:(c + 1) * tf]), 0.0)
        part = _dot((u * u).astype(BF16), w2_ref[c * tf:(c + 1) * tf, :])
        acc = part if acc is None else acc + part
    out_ref[...] = x1 + _rms(acc, g2_ref[...])


def _mixffn_call(x2, proj, branches, wb, wo, g_mix, g1, w1, w2, g2, *, tm, tf):
    t = x2.shape[0]
    row = lambda i: (i, 0)

    def resident(a):
        nd = a.ndim
        return pl.BlockSpec(a.shape, lambda i: (0,) * nd, pipeline_mode=pl.Buffered(1))

    return pl.pallas_call(
        functools.partial(_mixffn_kernel, tf=tf),
        grid=(t // tm,),
        in_specs=[pl.BlockSpec((tm, D_MODEL), row)]
        + [pl.BlockSpec((tm, GL_BLK), lambda i, c=COL_GL // GL_BLK + k: (i, c)) for k in range(2)]
        + [pl.BlockSpec((tm, MIX_W), row)] * N_BRANCH
        + [resident(a) for a in (wb, wo, g_mix, g1, w1, w2, g2)],
        out_specs=pl.BlockSpec((tm, D_MODEL), row),
        out_shape=jax.ShapeDtypeStruct((t, D_MODEL), F32),
        compiler_params=pltpu.CompilerParams(dimension_semantics=("parallel",), vmem_limit_bytes=MIXFFN_VMEM_LIMIT),
        name="merge_ffn",
    )(x2, proj, proj, *branches, wb, wo, g_mix, g1, w1, w2, g2)


W_TILE = MIX_W


def _wprep_kernel(a_ref, b_ref, tail_ref, o_ref, *, first_moved, last_moved, dt_tile):
    j = pl.program_id(0)

    def emit(rows):
        o_ref[...] = rows.T.astype(BF16)

    @pl.when(j < first_moved)
    def _():
        emit(a_ref[...])

    @pl.when((j >= first_moved) & (j < last_moved))
    def _():
        emit(jnp.concatenate([a_ref[SSD_HEADS:, :], b_ref[:SSD_HEADS, :]], axis=0))

    @pl.when(j == last_moved)
    def _():
        emit(jnp.concatenate([a_ref[SSD_HEADS:, :], tail_ref[...]], axis=0))

    @pl.when(j == dt_tile)
    def _():
        d = a_ref.shape[1]
        emit(jnp.concatenate([jnp.broadcast_to(a_ref[h:h + 1, :], (SSD_HEADDIM, d)) for h in range(SSD_HEADS)],
                             axis=0))


def _wprep_call(w_in_all, layer):
    _, d, cols = w_in_all.shape
    o_sdt = 4 * MIX_W + MIX_W + SSD_CONV_DIM
    assert o_sdt == COL_HG and o_sdt % W_TILE == 0 and (cols - SSD_HEADS) % W_TILE == 0 and SSD_HEADS == SUBLANE
    first_moved = o_sdt // W_TILE
    dt_tile = (cols - SSD_HEADS) // W_TILE
    last_moved = dt_tile - 1
    assert (dt_tile + 1) * W_TILE == PROJ_COLS
    wt = jnp.swapaxes(w_in_all, 1, 2)
    tail = wt[layer, dt_tile * W_TILE:, :]
    a_map = lambda j: (layer, jnp.where(j == dt_tile, first_moved, j), 0)
    b_map = lambda j: (layer, jnp.minimum(j + 1, last_moved), 0)
    return pl.pallas_call(
        functools.partial(_wprep_kernel, first_moved=first_moved, last_moved=last_moved, dt_tile=dt_tile),
        grid=(dt_tile + 1,),
        in_specs=[pl.BlockSpec((None, W_TILE, d), a_map), pl.BlockSpec((None, W_TILE, d), b_map),
                  _const_spec(tail.shape)],
        out_specs=pl.BlockSpec((d, W_TILE), lambda j: (0, j)),
        out_shape=jax.ShapeDtypeStruct((d, PROJ_COLS), BF16),
        compiler_params=_cparams(("arbitrary",)),
        name="w_in_layout",
    )(wt, wt, tail)


def _prep_layer(p, lb, w_main):

    def head_lanes(v):
        return jnp.repeat(v.astype(F32), SSD_HEADDIM).reshape(1, MIX_W)

    q = dict(
        w_main=w_main,
        g_pre_mix=p['g_pre_mix'].reshape(1, D_MODEL),
        g_post_mix=p['g_post_mix'].reshape(1, D_MODEL),
        g_pre_ffn=p['g_pre_ffn'].reshape(1, D_MODEL),
        g_post_ffn=p['g_post_ffn'].reshape(1, D_MODEL),
        ret_gn=p['ret_gn'],
        conv_w=p['ssd_conv_w'],
        conv_b=p['ssd_conv_b'].reshape(1, SSD_CONV_DIM),
        a_log=head_lanes(p['ssd_a_log']),
        dt_bias=head_lanes(p['ssd_dt_bias']),
        d_skip=head_lanes(p['ssd_d']),
        ssd_norm=p['ssd_norm'].reshape(1, MIX_W),
        hg_lb=lb.reshape(1, MIX_W),
        hg_norm=p['hg_norm'].reshape(1, HG_DK),
        s5=_s5_params(p),
        w_glu=p['s5_w_glu'].astype(BF16),
        w_branch=p['w_branch'].astype(BF16),
        w_out=p['w_out'].astype(BF16),
        w_ff1=p['w_ff1'].astype(BF16),
        w_ff2=p['w_ff2'].astype(BF16),
    )
    return q


def _layer(x3, states, acc, layer, q, pos, cfg):
    b, l, _ = x3.shape
    x2 = x3.reshape(b * l, D_MODEL)
    tb, tl, c = cfg['tb'], cfg['tl'], cfg['c']
    proj = _proj_call(x2, q['g_pre_mix'], q['w_main'], tm=cfg['tm_proj'], tn=cfg['tn'])
    st = {n: states[n][layer] for n in ('conv', 's5_re', 's5_im')}
    o_ret, s_ret = _ret_call(proj, pos, q['ret_gn'], states['ret'], acc['ret'], layer, b=b, l=l, tb=tb, tl=tl, sl=c,
                             unroll=cfg['u_ret'])
    o_ssd, conv_new, s_ssd = _ssd_call(proj, q['conv_w'], q['conv_b'], q['a_log'], q['dt_bias'], q['d_skip'],
                                       q['ssd_norm'], st['conv'], states['ssd'], acc['ssd'], layer,
                                       b=b, l=l, tb=tb, tl=tl, sl=c, unroll=cfg['u_ssd'])
    o_hg, s_hg = _hg_call(proj, q['hg_lb'], q['hg_norm'], states['hgrn'], acc['hgrn'], layer, b=b, l=l, tb=tb, tl=tl,
                          sl=c, unroll=cfg['u_hg'])
    wbu, are, aim, wc, d5 = q['s5']
    o_s5, s5_re, s5_im = _s5_call(proj, wbu, are, aim, wc, d5, q['w_glu'],
                                  st['s5_re'].reshape(b, S5_LANES), st['s5_im'].reshape(b, S5_LANES),
                                  b=b, l=l, tb=SUBLANE, tl=cfg['s5_tl'])
    x2 = _mixffn_call(x2, proj, (o_ret, o_ssd, o_hg, o_s5), q['w_branch'], q['w_out'], q['g_post_mix'], q['g_pre_ffn'],
                      q['w_ff1'], q['w_ff2'], q['g_post_ffn'], tm=cfg['tm_merge'], tf=cfg['tf'])
    def push(name, new):
        return new[None] if acc[name] is None else jnp.concatenate([acc[name], new[None]], axis=0)

    new = {'ret': s_ret, 'ssd': s_ssd, 'hgrn': s_hg, 'conv': push('conv', conv_new),
           's5_re': push('s5_re', s5_re.reshape(b, S5_GROUPS, S5_STATE)),
           's5_im': push('s5_im', s5_im.reshape(b, S5_GROUPS, S5_STATE))}
    return x2.reshape(b, l, D_MODEL), new


_NAMES = ('ret', 'ssd', 'conv', 'hgrn', 's5_re', 's5_im')


def _trunk(x, states, layers, pos, cfg):
    acc = {n: None for n in _NAMES}
    for layer, q in enumerate(layers):
        x, acc = _layer(x, states, acc, layer, q, pos, cfg)
    return x, acc


def _group_cfg(b, l):
    if l % CHUNK == 0:
        return dict(tb=1, tl=min(l, 512), c=CHUNK, tm_proj=2048, tn=1536, tm_merge=512, tf=1024,
                    s5_tl=min(l, 64), u_ret=4, u_ssd=8, u_hg=8)
    return dict(tb=SUBLANE, tl=l, c=l, tm_proj=min(b * l, 1024), tn=1536,
                tm_merge=min(b * l, 512), tf=1024, s5_tl=l, u_ret=1, u_ssd=1, u_hg=1)


def kernel(x_prompt, x_sample, state_ret, state_ssd, state_conv, state_hgrn, state_s5_re, state_s5_im, g_pre_mix, g_post_mix, g_pre_ffn, g_post_ffn, w_in, ret_gn, ssd_conv_w, ssd_conv_b, ssd_a_log, ssd_dt_bias, ssd_d, ssd_norm, hg_lb_logits, hg_norm, s5_a_re, s5_a_im, s5_b_re, s5_b_im, s5_c_re, s5_c_im, s5_d, s5_log_dt, s5_w_glu, w_branch, w_out, w_ff1, w_ff2):
    params = {
        'g_pre_mix': g_pre_mix, 'g_post_mix': g_post_mix, 'g_pre_ffn': g_pre_ffn, 'g_post_ffn': g_post_ffn,
        'ret_gn': ret_gn, 'ssd_conv_w': ssd_conv_w, 'ssd_conv_b': ssd_conv_b,
        'ssd_a_log': ssd_a_log, 'ssd_dt_bias': ssd_dt_bias, 'ssd_d': ssd_d, 'ssd_norm': ssd_norm,
        'hg_norm': hg_norm, 's5_a_re': s5_a_re, 's5_a_im': s5_a_im, 's5_b_re': s5_b_re, 's5_b_im': s5_b_im,
        's5_c_re': s5_c_re, 's5_c_im': s5_c_im, 's5_d': s5_d, 's5_log_dt': s5_log_dt, 's5_w_glu': s5_w_glu,
        'w_branch': w_branch, 'w_out': w_out, 'w_ff1': w_ff1, 'w_ff2': w_ff2,
    }
    depth = w_in.shape[0]
    w = jax.nn.softmax(hg_lb_logits.astype(F32), axis=0)
    lbs = jnp.cumsum(w, axis=0) - w[0]
    layers = [_prep_layer({k: v[i] for k, v in params.items()}, lbs[i], _wprep_call(w_in, i)) for i in range(depth)]

    bp, lp, _ = x_prompt.shape
    bs, ls, _ = x_sample.shape
    zero_states = {
        'ret': jnp.zeros((depth, bp) + state_ret.shape[2:], F32),
        'ssd': jnp.zeros((depth, bp) + state_ssd.shape[2:], F32),
        'conv': jnp.zeros((depth, bp) + state_conv.shape[2:], F32),
        'hgrn': jnp.zeros((depth, bp) + state_hgrn.shape[2:], F32),
        's5_re': jnp.zeros((depth, bp) + state_s5_re.shape[2:], F32),
        's5_im': jnp.zeros((depth, bp) + state_s5_im.shape[2:], F32),
    }
    sample_states = {'ret': state_ret, 'ssd': state_ssd, 'conv': state_conv, 'hgrn': state_hgrn,
                     's5_re': state_s5_re, 's5_im': state_s5_im}
    pos_p = np.arange(lp)
    pos_s = PAST_LEN + np.arange(ls)
    y_p, new_p = _trunk(x_prompt, zero_states, layers, pos_p, _group_cfg(bp, lp))
    y_s, new_s = _trunk(x_sample, sample_states, layers, pos_s, _group_cfg(bs, ls))
    return (y_p, y_s,
            new_p['ret'], new_s['ret'], new_p['ssd'], new_s['ssd'], new_p['conv'], new_s['conv'],
            new_p['hgrn'], new_s['hgrn'], new_p['s5_re'], new_s['s5_re'], new_p['s5_im'], new_s['s5_im'])
```

```python
import functools
import math

import jax
import jax.numpy as jnp
import numpy as np
from jax import lax
from jax.experimental import pallas as pl
from jax.experimental.pallas import tpu as pltpu

F32 = jnp.float32
BF16 = jnp.bfloat16

D_MODEL = 1024
DEPTH = 2
PAST_LEN = 16384
MIX_W = D_MODEL // 2
N_BRANCH = 4
RET_HEADS = 4
RET_DK = MIX_W // RET_HEADS
ROPE_BASE = 10000.0
SSD_HEADDIM = 64
SSD_HEADS = MIX_W // SSD_HEADDIM
SSD_GROUPS = 2
SSD_STATE = 128
SSD_CONV = 4
SSD_CONV_DIM = MIX_W + 2 * SSD_GROUPS * SSD_STATE
SSD_PAIRS = SSD_HEADS // 2
HG_HEADS = 4
HG_DK = MIX_W // HG_HEADS
S5_GROUP = 16
S5_GROUPS = MIX_W // S5_GROUP
S5_STATE = 64
S5_LANES = S5_GROUPS * S5_STATE
D_FF = 4 * D_MODEL
GATE_COLS = N_BRANCH * D_MODEL
CHUNK = 64
CR = 64
EPS = 1e-6

LANE = 128
SUBLANE = 8
VMEM_LIMIT = 48 * 1024 * 1024
MIXFFN_VMEM_LIMIT = 56 * 1024 * 1024

COL_RET = 0
COL_SZ = COL_RET + 4 * MIX_W
COL_XS = COL_SZ + MIX_W
COL_BC = COL_XS + MIX_W
COL_HG = COL_BC + MIX_W
COL_SU = COL_HG + 4 * MIX_W
COL_GL = COL_SU + MIX_W
COL_DT = COL_GL + GATE_COLS
PROJ_COLS = COL_DT + MIX_W
GL_BLK = GATE_COLS // 2
assert COL_GL % GL_BLK == 0


def _sigmoid(x):
    return 1.0 / (1.0 + jnp.exp(-x))


def _silu(x):
    return x * _sigmoid(x)


def _softplus(x):
    return jnp.maximum(x, 0.0) + jnp.log1p(jnp.exp(-jnp.abs(x)))


def _gelu_tanh(x):
    c = math.sqrt(2.0 / math.pi)
    return 0.5 * x * (1.0 + jnp.tanh(c * (x + 0.044715 * (x * x * x))))


def _rms(x, g):
    ms = jnp.mean(x * x, axis=-1, keepdims=True)
    return x * lax.rsqrt(ms + EPS) * g


def _dot(a, b):
    return jnp.dot(a, b, preferred_element_type=F32)


def _dot_nt(a, b):
    return lax.dot_general(a, b, (((1,), (1,)), ((), ())), preferred_element_type=F32)


def _split3(a):
    hi = a.astype(BF16)
    r1 = a - hi.astype(F32)
    mid = r1.astype(BF16)
    lo = (r1 - mid.astype(F32)).astype(BF16)
    return hi, mid, lo


def _sel_l(m, a):
    hi, mid, lo = _split3(a)
    return (_dot(m, lo) + _dot(m, mid)) + _dot(m, hi)


def _sel2_l(m, a):
    hi = a.astype(BF16)
    mid = (a - hi.astype(F32)).astype(BF16)
    return _dot(m, mid) + _dot(m, hi)


def _sel_r(a, m):
    hi, mid, lo = _split3(a)
    return (_dot(lo, m) + _dot(mid, m)) + _dot(hi, m)


def _cparams(sem):
    return pltpu.CompilerParams(dimension_semantics=sem, vmem_limit_bytes=VMEM_LIMIT)


def _const_spec(shape):
    nd = len(shape)
    return pl.BlockSpec(shape, lambda *_: (0,) * nd)


def _state_io(s_all, prev, layer, tb):
    dims = tuple(s_all.shape[2:])
    zeros = (0,) * len(dims)
    in_spec = pl.BlockSpec((None, tb) + dims, lambda bi, li: (layer, bi) + zeros)
    if prev is None:
        nprev, prev, prev_spec = 0, s_all, in_spec
    else:
        nprev = prev.shape[0]
        prev_spec = pl.BlockSpec((nprev, tb) + dims, lambda bi, li: (0, bi) + zeros)
    out_spec = pl.BlockSpec((nprev + 1, tb) + dims, lambda bi, li: (0, bi) + zeros)
    out_shape = jax.ShapeDtypeStruct((nprev + 1, s_all.shape[1]) + dims, F32)
    return nprev, prev, in_spec, prev_spec, out_spec, out_shape


def _proj_kernel(x_ref, g_ref, w_ref, o_ref, h_ref):
    @pl.when(pl.program_id(1) == 0)
    def _():
        h_ref[...] = _rms(x_ref[...], g_ref[...]).astype(BF16)

    o_ref[...] = _dot(h_ref[...], w_ref[...]).astype(BF16)


def _proj_call(x2, g, w, *, tm, tn):
    t, d = x2.shape
    n = w.shape[1]
    return pl.pallas_call(
        _proj_kernel,
        grid=(t // tm, n // tn),
        in_specs=[
            pl.BlockSpec((tm, d), lambda i, j: (i, 0)),
            pl.BlockSpec((1, d), lambda i, j: (0, 0)),
            pl.BlockSpec((d, tn), lambda i, j: (0, j)),
        ],
        out_specs=pl.BlockSpec((tm, tn), lambda i, j: (i, j)),
        out_shape=jax.ShapeDtypeStruct((t, n), BF16),
        scratch_shapes=[pltpu.VMEM((tm, d), BF16)],
        compiler_params=_cparams(("parallel", "arbitrary")),
        name="in_proj",
    )(x2, g, w)


def _ret_consts(sl):
    nseq = CR // sl
    h = np.arange(RET_HEADS, dtype=np.float64)
    log_g = np.log(1.0 - 2.0 ** (-5.0 - h))
    idx = np.arange(sl, dtype=np.float64)
    diff = idx[:, None] - idx[None, :]
    dmat = np.where(diff >= 0, np.exp(log_g[:, None, None] * np.maximum(diff, 0.0)), 0.0)
    dmat = np.stack([np.kron(np.eye(nseq), d) for d in dmat])
    q_dec = np.tile(np.exp(log_g[:, None] * (idx[None, :] + 1.0)), (1, nseq))
    k_dec = np.tile(np.exp(log_g[:, None] * (sl - 1.0 - idx[None, :])), (1, nseq))
    chunk_dec = np.exp(log_g * sl)
    qd = np.broadcast_to(q_dec[:, :, None], (RET_HEADS, CR, LANE))
    kd = np.broadcast_to(k_dec[:, :, None], (RET_HEADS, CR, LANE))
    return (jnp.asarray(dmat, F32), jnp.asarray(qd, F32), jnp.asarray(kd, F32),
            tuple(float(v) for v in chunk_dec))


def _rope_tables(pos):
    half = RET_DK // 2
    inv = ROPE_BASE ** (-np.arange(half, dtype=np.float64) / half)
    ang = np.asarray(pos, np.float64)[:, None] * inv[None, :]
    cos, sin = np.cos(ang), np.sin(ang)
    return (jnp.asarray(np.concatenate([cos, cos], axis=1), F32),
            jnp.asarray(np.concatenate([-sin, sin], axis=1), F32))


def _ret_kernel(q_ref, k_ref, v_ref, g_ref, cos_ref, sin_ref, dmat_ref, qd_ref, kd_ref, gn_ref, s0_ref, prev_ref,
                o_ref, so_ref, *, tb, tl, sl, cdec, nprev, unroll):
    st_ref = so_ref.at[nprev]

    @pl.when(pl.program_id(1) == 0)
    def _():
        st_ref[...] = s0_ref[...]
        if nprev:
            so_ref[0:nprev] = prev_ref[...]

    nseq = CR // sl
    scale = RET_DK ** -0.5

    def body(pc, carry):
        r0 = pl.multiple_of(pc * CR, CR)
        t0 = r0 if nseq == 1 else 0
        cos = cos_ref[pl.ds(t0, CR), :]
        sin = sin_ref[pl.ds(t0, CR), :]
        heads = [slice(h * LANE, (h + 1) * LANE) for h in range(RET_HEADS)]
        qrs, krs = [], []
        for hs in heads:
            q = q_ref[pl.ds(r0, CR), hs].astype(F32)
            k = k_ref[pl.ds(r0, CR), hs].astype(F32)
            qrs.append(q * cos + pltpu.roll(q, LANE // 2, axis=1) * sin)
            krs.append((k * cos + pltpu.roll(k, LANE // 2, axis=1) * sin) * scale)
        vs = [v_ref[pl.ds(r0, CR), hs] for hs in heads]
        atts = [(_dot_nt(qrs[h].astype(BF16), krs[h].astype(BF16)) * dmat_ref[h]).astype(BF16)
                for h in range(RET_HEADS)]
        outs = []
        for h in range(RET_HEADS):
            o = _dot(atts[h], vs[h])
            qd = qrs[h] * qd_ref[h]
            kd = krs[h] * kd_ref[h]
            ois = []
            for s in range(nseq):
                seq = pc * nseq + s if nseq > 1 else 0
                rs = slice(s * sl, (s + 1) * sl)
                st = st_ref[seq, h]
                ois.append(_dot(qd[rs].astype(BF16), st.astype(BF16)))
                v_s = vs[h] if nseq == 1 else vs[h].astype(F32)[rs].astype(BF16)
                st_ref[seq, h] = st * cdec[h] + _dot(kd[rs].T.astype(BF16), v_s)
            outs.append(o + (ois[0] if nseq == 1 else jnp.concatenate(ois, axis=0)))
        for h, hs in enumerate(heads):
            o = outs[h]
            mu = jnp.mean(o, axis=-1, keepdims=True)
            oc = o - mu
            var = jnp.mean(oc * oc, axis=-1, keepdims=True)
            on = oc * lax.rsqrt(var + EPS) * gn_ref[h:h + 1, :]
            o_ref[pl.ds(r0, CR), hs] = (on * _silu(g_ref[pl.ds(r0, CR), hs].astype(F32))).astype(BF16)
        return carry

    lax.fori_loop(0, tb * tl // CR, body, 0, unroll=unroll)


def _seq_grid(b, l, tb, tl):
    assert b % tb == 0 and l % tl == 0 and (tb == 1 or tl == l)
    nl = l // tl
    rows = tb * tl

    def row_map(col_blk):
        return lambda bi, li: (bi * nl + li, col_blk)

    return (b // tb, nl), rows, row_map


def _ret_call(proj, pos, gn, s_all, prev, layer, *, b, l, tb, tl, sl, unroll):
    grid, rows, row_map = _seq_grid(b, l, tb, tl)
    assert rows % CR == 0 and CR % sl == 0 and (sl == CR and tb == 1 or sl == tl)
    dmat, qd, kd, cdec = _ret_consts(sl)
    cos, sin = _rope_tables(pos)
    if sl < CR:
        cos, sin = jnp.tile(cos, (CR // sl, 1)), jnp.tile(sin, (CR // sl, 1))
        tab_spec = _const_spec(cos.shape)
    else:
        tab_spec = pl.BlockSpec((tl, LANE), lambda bi, li: (li, 0))
    cb = COL_RET // MIX_W
    nprev, prev, s_spec, prev_spec, so_spec, so_shape = _state_io(s_all, prev, layer, tb)
    return pl.pallas_call(
        functools.partial(_ret_kernel, tb=tb, tl=tl, sl=sl, cdec=cdec, nprev=nprev, unroll=unroll),
        grid=grid,
        in_specs=[pl.BlockSpec((rows, MIX_W), row_map(cb + i)) for i in range(4)] + [
            tab_spec, tab_spec,
            _const_spec(dmat.shape), _const_spec(qd.shape), _const_spec(kd.shape), _const_spec(gn.shape),
            s_spec, prev_spec,
        ],
        out_specs=[pl.BlockSpec((rows, MIX_W), row_map(0)), so_spec],
        out_shape=[jax.ShapeDtypeStruct((b * l, MIX_W), BF16), so_shape],
        compiler_params=_cparams(("parallel", "arbitrary")),
        name="retention",
    )(proj, proj, proj, proj, cos, sin, dmat, qd, kd, gn, s_all, prev)


def _ssd_consts(sl):
    nseq = CR // sl
    tril = np.kron(np.eye(nseq), np.tril(np.ones((sl, sl))))
    sumall = np.kron(np.eye(nseq), np.ones((sl, sl)))
    ones = np.ones((CR, CR))
    i = np.arange(CR)[:, None]
    j = np.arange(MIX_W)[None, :] % SSD_HEADDIM
    eye = (i == j).astype(np.float64)
    mask = ((i // sl == j // sl) & (i >= j)).astype(np.float64)
    return (jnp.asarray(np.concatenate([tril, sumall], axis=0), BF16), jnp.asarray(ones, BF16),
            jnp.asarray(eye, F32), jnp.asarray(mask, F32))


def _ssd_kernel(z_ref, xs_ref, bc_ref, dtx_ref, cw_ref, cb_ref, alog_ref, dtb_ref, dsk_ref, nw_ref,
                mcum_ref, ones_ref, eye_ref, mask_ref, cs0_ref, s0_ref, prev_ref,
                o_ref, cs_ref, so_ref, xpad, st, *, tb, tl, sl, nprev, unroll):
    li = pl.program_id(1)
    nl = pl.num_programs(1)
    nseq = CR // sl
    kc = SSD_CONV - 1
    off = SUBLANE - kc
    grp = 2 * SUBLANE

    @pl.when(li == 0)
    def _():
        for s in range(tb):
            for p in range(SSD_PAIRS):
                st[s, p] = s0_ref[s, p].T
        if nprev:
            so_ref[0:nprev] = prev_ref[...]
        if nseq == 1:
            xpad[0:SUBLANE, :] = jnp.zeros((SUBLANE, SSD_CONV_DIM), F32)
            xpad[off:SUBLANE, :] = cs0_ref[0]

    if nseq == 1:
        xpad[SUBLANE:SUBLANE + tl, 0:MIX_W] = xs_ref[...].astype(F32)
        xpad[SUBLANE:SUBLANE + tl, MIX_W:2 * MIX_W] = bc_ref[...].astype(F32)
    else:
        xs_new = xs_ref[...].astype(F32)
        bc_new = bc_ref[...].astype(F32)
        for s in range(tb):
            xpad[s * grp:s * grp + SUBLANE, :] = jnp.zeros((SUBLANE, SSD_CONV_DIM), F32)
            xpad[s * grp + off:s * grp + SUBLANE, :] = cs0_ref[s]
            xpad[s * grp + SUBLANE:(s + 1) * grp, 0:MIX_W] = xs_new[s * sl:(s + 1) * sl, :]
            xpad[s * grp + SUBLANE:(s + 1) * grp, MIX_W:2 * MIX_W] = bc_new[s * sl:(s + 1) * sl, :]
            cs_ref[s] = xpad[(s + 1) * grp - kc:(s + 1) * grp, :]

    a_neg = -jnp.exp(alog_ref[...])
    half = lax.broadcasted_iota(jnp.int32, (CR, LANE), 1) < SSD_HEADDIM

    def body(pc, carry):
        r0 = pl.multiple_of(pc * CR, CR)
        if nseq == 1:
            win = xpad[pl.ds(r0, CR + SUBLANE), :]
            take = lambda a: a[SUBLANE:, :]
        else:
            win = xpad[pl.ds(pl.multiple_of(pc * nseq * grp, nseq * grp), nseq * grp), :]
            take = lambda a: a.reshape(nseq, grp, SSD_CONV_DIM)[:, SUBLANE:, :].reshape(CR, SSD_CONV_DIM)
        acc = cb_ref[...] + take(pltpu.roll(win, kc, axis=0)) * cw_ref[0:1, :]
        for j in range(1, kc):
            acc = acc + take(pltpu.roll(win, kc - j, axis=0)) * cw_ref[j:j + 1, :]
        xc = _silu(acc + take(win) * cw_ref[kc:kc + 1, :])
        xs = xc[:, 0:MIX_W]
        bm = xc[:, MIX_W:MIX_W + SSD_GROUPS * SSD_STATE]
        cm = xc[:, MIX_W + SSD_GROUPS * SSD_STATE:2 * MIX_W]

        dtx = _softplus(dtx_ref[pl.ds(r0, CR), :].astype(F32) + dtb_ref[...])
        both = _sel_l(mcum_ref[...], dtx * a_neg)
        cum, clast = both[0:CR], both[CR:2 * CR]
        ecum = jnp.exp(cum)
        dend = jnp.exp(clast - cum)
        elast = jnp.exp(clast)
        rr = _sel_l(ones_ref[...], cum * eye_ref[...])
        msk = mask_ref[...]
        lmat = jnp.exp(jnp.where(msk > 0, cum - rr, 0.0)) * msk
        zz = z_ref[pl.ds(r0, CR), :].astype(F32)
        npp = SSD_PAIRS // SSD_GROUPS
        bm_gs = [bm[:, g * SSD_STATE:(g + 1) * SSD_STATE] for g in range(SSD_GROUPS)]
        cm_gs = [cm[:, g * SSD_STATE:(g + 1) * SSD_STATE] for g in range(SSD_GROUPS)]
        cm_gbs = [c_.astype(BF16) for c_ in cm_gs]
        cb2s = [_dot_nt(cm_gbs[g], jnp.concatenate([bm_gs[g], bm_gs[g]], axis=0).astype(BF16))
                for g in range(SSD_GROUPS)]
        pss = [slice(p * LANE, (p + 1) * LANE) for p in range(SSD_PAIRS)]
        xdts = [xs[:, ps] * dtx[:, ps] for ps in pss]
        xsts = [jnp.concatenate([jnp.where(half, x_, 0.0), jnp.where(half, 0.0, x_)], axis=0).astype(BF16)
                for x_ in xdts]
        ys = [_dot((cb2s[p // npp] * lmat[:, pss[p]]).astype(BF16), xsts[p]) for p in range(SSD_PAIRS)]
        xds = [xdts[p] * dend[:, pss[p]] for p in range(SSD_PAIRS)]
        yis = []
        for p in range(SSD_PAIRS):
            g = p // npp
            parts = []
            for s in range(nseq):
                seq = pc * nseq + s if nseq > 1 else 0
                if nseq == 1:
                    cm_s, bm_s, xd_s = cm_gbs[g], bm_gs[g], xds[p]
                else:
                    rs = slice(s * sl, (s + 1) * sl)
                    cm_s, bm_s, xd_s = cm_gs[g][rs].astype(BF16), bm_gs[g][rs], xds[p][rs]
                stp = st[seq, p]
                parts.append(_dot(cm_s, stp.astype(BF16)))
                st[seq, p] = stp * elast[s * sl:s * sl + 1, pss[p]] + _dot(bm_s.T.astype(BF16), xd_s.astype(BF16))
            yis.append(parts[0] if nseq == 1 else jnp.concatenate(parts, axis=0))
        gw = SSD_HEADDIM * SSD_HEADS // SSD_GROUPS
        for g in range(SSD_GROUPS):
            yg = jnp.concatenate([ys[p] + yis[p] * ecum[:, pss[p]] + xs[:, pss[p]] * dsk_ref[:, pss[p]]
                                  for p in range(g * npp, (g + 1) * npp)], axis=1)
            zs = slice(g * gw, (g + 1) * gw)
            o_ref[pl.ds(r0, CR), zs] = _rms(yg * _silu(zz[:, zs]), nw_ref[:, zs]).astype(BF16)
        return carry

    lax.fori_loop(0, tb * tl // CR, body, 0, unroll=unroll)

    if nseq == 1:
        xpad[0:SUBLANE, :] = xpad[tl:tl + SUBLANE, :]

    @pl.when(li == nl - 1)
    def _():
        if nseq == 1:
            cs_ref[0] = xpad[tl + off:tl + SUBLANE, :]
        for s in range(tb):
            for p in range(SSD_PAIRS):
                so_ref[nprev, s, p] = st[s, p].T


def _ssd_call(proj, cw, cb, alog, dtb, dsk, nw, cs0, s_all, prev, layer, *, b, l, tb, tl, sl, unroll):
    grid, rows, row_map = _seq_grid(b, l, tb, tl)
    assert rows % CR == 0 and CR % sl == 0 and (sl == CR and tb == 1 or sl == tl == SUBLANE)
    consts = _ssd_consts(sl)
    pair = lambda a: a.reshape(a.shape[:2] + (SSD_PAIRS, LANE, SSD_STATE))
    s_pairs = pair(s_all)
    nprev, prev, s_spec, prev_spec, so_spec, so_shape = _state_io(s_pairs, None if prev is None else pair(prev),
                                                                  layer, tb)
    cs_spec = pl.BlockSpec((tb, SSD_CONV - 1, SSD_CONV_DIM), lambda bi, li: (bi, 0, 0))
    params = (cw, cb, alog, dtb, dsk, nw)
    xpad_rows = SUBLANE + tl if sl == CR else tb * 2 * SUBLANE
    o, cs, so = pl.pallas_call(
        functools.partial(_ssd_kernel, tb=tb, tl=tl, sl=sl, nprev=nprev, unroll=unroll),
        grid=grid,
        in_specs=[pl.BlockSpec((rows, MIX_W), row_map(cb_)) for cb_ in
                  (COL_SZ // MIX_W, COL_XS // MIX_W, COL_BC // MIX_W, COL_DT // MIX_W)]
        + [_const_spec(a.shape) for a in params + consts] + [cs_spec, s_spec, prev_spec],
        out_specs=[pl.BlockSpec((rows, MIX_W), row_map(0)), cs_spec, so_spec],
        out_shape=[jax.ShapeDtypeStruct((b * l, MIX_W), BF16), jax.ShapeDtypeStruct(cs0.shape, F32), so_shape],
        scratch_shapes=[pltpu.VMEM((xpad_rows, SSD_CONV_DIM), F32),
                        pltpu.VMEM((tb, SSD_PAIRS, SSD_STATE, LANE), F32)],
        compiler_params=_cparams(("parallel", "arbitrary")),
        name="ssd",
    )(proj, proj, proj, proj, *params, *consts, cs0, s_pairs, prev)
    return o, cs, so.reshape((nprev + 1,) + s_all.shape[1:])


def _hg_levels(sl):
    ms = []
    m = sl // 2
    while m >= 1:
        ms.append(m)
        m //= 2
    return ms


def _hg_consts(sl):
    nseq = CR // sl
    tril = np.kron(np.eye(nseq), np.tril(np.ones((sl, sl))))
    sumall = np.kron(np.eye(nseq), np.ones((sl, sl)))
    mats = []
    masks = []
    i = np.arange(CR)
    for m in _hg_levels(sl):
        ref = (i // (2 * m)) * (2 * m) + m - 1
        if m > 1:
            mats.append(np.abs(tril - tril[ref]))
        same = (i[:, None] // (2 * m)) == (i[None, :] // (2 * m))
        qside = ((i // m) % 2 == 1)[:, None]
        kside = ((i // m) % 2 == 0)[None, :]
        masks.append((same & qside & kside).astype(np.float64))
    return (jnp.asarray(np.concatenate([tril, sumall], axis=0), BF16), jnp.asarray(np.concatenate(mats, axis=0), BF16),
            jnp.asarray(np.stack(masks), F32))


def _hg_kernel(q_ref, f_ref, i_ref, g_ref, lb_ref, nw_ref, mcum_ref, mlev_ref, mask_ref, s0_ref, prev_ref,
               o_ref, so_ref, st, qe_s, ke_s, qg_s, kd_s, el_s, dv_s, *, tb, tl, sl, nprev, unroll):
    li = pl.program_id(1)
    nl = pl.num_programs(1)

    @pl.when(li == 0)
    def _():
        for s in range(tb):
            for h in range(HG_HEADS):
                st[s, h] = s0_ref[s, h].T
        if nprev:
            so_ref[0:nprev] = prev_ref[...]

    nseq = CR // sl
    levels = _hg_levels(sl)

    nlev = len(levels)
    odd_row = (lax.broadcasted_iota(jnp.int32, (CR, MIX_W), 0) & 1) == 1

    def stage_a(pc, carry):
        r0 = pl.multiple_of(pc * CR, CR)
        lb = lb_ref[...]
        q = _silu(q_ref[pl.ds(r0, CR), :].astype(F32))
        sig = _sigmoid(f_ref[pl.ds(r0, CR), :].astype(F32))
        fdec = lb + (1.0 - lb) * sig
        logf = jnp.log(fdec)
        k = (1.0 - lb) * (1.0 - sig)
        v = i_ref[pl.ds(r0, CR), :].astype(F32)
        cums = _sel_l(mcum_ref[...], logf)
        gc = cums[0:CR]
        glast = cums[CR:2 * CR]
        if nlev > 1:
            dl = _sel2_l(mlev_ref[...], logf)
        qb, kb = q.astype(BF16), k.astype(BF16)
        for n in range(nlev):
            e = jnp.exp(dl[n * CR:(n + 1) * CR]) if n < nlev - 1 else jnp.where(odd_row, fdec, 1.0)
            e = e.astype(BF16)
            qe_s[pc, n] = qb * e
            ke_s[pc, n] = kb * e
        qg_s[pc] = (q * jnp.exp(gc)).astype(BF16)
        kd_s[pc] = (k * jnp.exp(glast - gc)).astype(BF16)
        el_s[pc] = jnp.exp(glast)
        qk = q * k
        dv_s[pc] = jnp.concatenate(
            [jnp.sum(qk[:, h * LANE:(h + 1) * LANE], axis=-1, keepdims=True) * v[:, h * LANE:(h + 1) * LANE]
             for h in range(HG_HEADS)], axis=1)
        return carry

    def stage_b(pc, carry):
        r0 = pl.multiple_of(pc * CR, CR)
        heads = [slice(h * LANE, (h + 1) * LANE) for h in range(HG_HEADS)]
        vs = [i_ref[pl.ds(r0, CR), hs] for hs in heads]
        atts = []
        for hs in heads:
            att = None
            for n in range(nlev):
                a = _dot_nt(qe_s[pc, n, :, hs], ke_s[pc, n, :, hs]) * mask_ref[n]
                att = a if att is None else att + a
            atts.append(att.astype(BF16))
        outs = []
        for h, hs in enumerate(heads):
            o = _dot(atts[h], vs[h]) + dv_s[pc, :, hs]
            ois = []
            for s in range(nseq):
                seq = pc * nseq + s if nseq > 1 else 0
                rs = slice(s * sl, (s + 1) * sl)
                stt = st[seq, h]
                if nseq == 1:
                    qg_r, kd_r, v_r = qg_s[pc, :, hs], kd_s[pc, :, hs], vs[h].astype(F32)
                else:
                    qg_r = qg_s[pc, :, hs].astype(F32)[rs].astype(BF16)
                    kd_r = kd_s[pc, :, hs].astype(F32)[rs].astype(BF16)
                    v_r = vs[h].astype(F32)[rs]
                ois.append(_dot_nt(qg_r, stt.astype(BF16)))
                st[seq, h] = stt * el_s[pc, s * sl:s * sl + 1, hs] + _dot(v_r.T.astype(BF16), kd_r)
            outs.append(o + (ois[0] if nseq == 1 else jnp.concatenate(ois, axis=0)))
        for h, hs in enumerate(heads):
            gate = _sigmoid(g_ref[pl.ds(r0, CR), hs].astype(F32))
            o_ref[pl.ds(r0, CR), hs] = (_rms(outs[h], nw_ref[...]) * gate).astype(BF16)
        return carry

    ngrp = tb * tl // CR
    lax.fori_loop(0, ngrp, stage_a, 0, unroll=unroll)
    lax.fori_loop(0, ngrp, stage_b, 0, unroll=unroll)

    @pl.when(li == nl - 1)
    def _():
        for s in range(tb):
            for h in range(HG_HEADS):
                so_ref[nprev, s, h] = st[s, h].T


def _hg_call(proj, lb, nw, s_all, prev, layer, *, b, l, tb, tl, sl, unroll):
    grid, rows, row_map = _seq_grid(b, l, tb, tl)
    assert rows % CR == 0 and CR % sl == 0 and (sl == CR and tb == 1 or sl == tl)
    consts = _hg_consts(sl)
    nlev = len(_hg_levels(sl))
    cb = COL_HG // MIX_W
    nprev, prev, s_spec, prev_spec, so_spec, so_shape = _state_io(s_all, prev, layer, tb)
    stage = lambda dt, *lead: pltpu.VMEM((rows // CR,) + lead + (CR, MIX_W), dt)
    return pl.pallas_call(
        functools.partial(_hg_kernel, tb=tb, tl=tl, sl=sl, nprev=nprev, unroll=unroll),
        grid=grid,
        in_specs=[pl.BlockSpec((rows, MIX_W), row_map(cb + i)) for i in range(4)]
        + [_const_spec(a.shape) for a in (lb, nw) + consts] + [s_spec, prev_spec],
        out_specs=[pl.BlockSpec((rows, MIX_W), row_map(0)), so_spec],
        out_shape=[jax.ShapeDtypeStruct((b * l, MIX_W), BF16), so_shape],
        scratch_shapes=[pltpu.VMEM((tb, HG_HEADS, HG_DK, HG_DK), F32), stage(BF16, nlev), stage(BF16, nlev),
                        stage(BF16), stage(BF16), stage(F32), stage(F32)],
        compiler_params=_cparams(("parallel", "arbitrary")),
        name="hgrn2",
    )(proj, proj, proj, proj, lb, nw, *consts, s_all, prev)


S5_TILES = MIX_W // LANE
S5_TLANES = S5_LANES // S5_TILES


def _s5_perm(tb, tl):
    rows = tb * tl
    p = np.zeros((rows, rows))
    t, s = np.meshgrid(np.arange(tl), np.arange(tb), indexing='ij')
    p[(s * tl + t).ravel(), (t * tb + s).ravel()] = 1.0
    return jnp.asarray(p, BF16), jnp.asarray(p.T, BF16)


def _s5_kernel(su_ref, perm_ref, permt_ref, wbu_ref, are_ref, aim_ref, wc_ref, d_ref, wglu_ref, h0r_ref, h0i_ref,
               o_ref, hr_ref, hi_ref, bur, bui, *, tb, tl):
    @pl.when(pl.program_id(1) == 0)
    def _():
        hr_ref[...] = h0r_ref[...]
        hi_ref[...] = h0i_ref[...]

    rows = tl * tb
    u = _dot(permt_ref[...], su_ref[...].reshape(rows, MIX_W))
    tiles = [slice(t * S5_TLANES, (t + 1) * S5_TLANES) for t in range(S5_TILES)]

    def bu_tile(t):
        bu = _dot(u[:, t * LANE:(t + 1) * LANE].astype(BF16), wbu_ref[t])
        bur[:, tiles[t]] = bu[:, :S5_TLANES]
        bui[:, tiles[t]] = bu[:, S5_TLANES:]

    def scan_tile(t):
        ls = tiles[t]
        ar = jnp.broadcast_to(are_ref[:, ls], (tb, S5_TLANES))
        ai = jnp.broadcast_to(aim_ref[:, ls], (tb, S5_TLANES))
        hr, hi = hr_ref[:, ls], hi_ref[:, ls]
        for i in range(tl):
            rs = slice(i * tb, (i + 1) * tb)
            hr, hi = (ar * hr - ai * hi) + bur[rs, ls], (ar * hi + ai * hr) + bui[rs, ls]
            bur[rs, ls] = hr
            bui[rs, ls] = hi
        hr_ref[:, ls] = hr
        hi_ref[:, ls] = hi

    def y_tile(t):
        hcat = jnp.concatenate([bur[:, tiles[t]], bui[:, tiles[t]]], axis=1).astype(BF16)
        cs = slice(t * LANE, (t + 1) * LANE)
        return _dot(hcat, wc_ref[t]) + d_ref[:, cs] * u[:, cs]

    ys = [None] * S5_TILES
    bu_tile(0)
    for t in range(S5_TILES):
        if t + 1 < S5_TILES:
            bu_tile(t + 1)
        scan_tile(t)
        if t >= 1:
            ys[t - 1] = y_tile(t - 1)
    ys[S5_TILES - 1] = y_tile(S5_TILES - 1)
    z = _gelu_tanh(jnp.concatenate(ys, axis=1))
    out = z * _sigmoid(_dot(z.astype(BF16), wglu_ref[...]))
    o_ref[...] = _dot(perm_ref[...], out.astype(BF16)).astype(BF16).reshape(o_ref.shape)


def _s5_call(proj, wbu, are, aim, wc, d, wglu, h0r, h0i, *, b, l, tb, tl):
    assert l % tl == 0 and b % tb == 0 and tb == SUBLANE
    perm, permt = _s5_perm(tb, tl)
    h_spec = pl.BlockSpec((tb, S5_LANES), lambda bi, li: (bi, 0))
    params = (perm, permt, wbu, are, aim, wc, d, wglu)
    cb = COL_SU // MIX_W
    if tl == l:
        su, su_spec = proj, pl.BlockSpec((tb * tl, MIX_W), lambda bi, li: (bi, cb))
        o_shape, o_spec = (b * l, MIX_W), pl.BlockSpec((tb * tl, MIX_W), lambda bi, li: (bi, 0))
    else:
        su, su_spec = proj.reshape(b, l, -1), pl.BlockSpec((tb, tl, MIX_W), lambda bi, li: (bi, li, cb))
        o_shape, o_spec = (b, l, MIX_W), pl.BlockSpec((tb, tl, MIX_W), lambda bi, li: (bi, li, 0))
    o, hr, hi = pl.pallas_call(
        functools.partial(_s5_kernel, tb=tb, tl=tl),
        grid=(b // tb, l // tl),
        in_specs=[su_spec] + [_const_spec(a.shape) for a in params] + [h_spec, h_spec],
        out_specs=[o_spec, h_spec, h_spec],
        out_shape=[jax.ShapeDtypeStruct(o_shape, BF16), jax.ShapeDtypeStruct(h0r.shape, F32),
                   jax.ShapeDtypeStruct(h0i.shape, F32)],
        scratch_shapes=[pltpu.VMEM((tl * tb, S5_LANES), F32), pltpu.VMEM((tl * tb, S5_LANES), F32)],
        compiler_params=_cparams(("parallel", "arbitrary")),
        name="s5",
    )(su, *params, h0r, h0i)
    return o.reshape(b * l, MIX_W), hr, hi


def _s5_params(p):
    a_re, a_im = p['s5_a_re'], p['s5_a_im']
    dt = jnp.exp(p['s5_log_dt'])[:, None]
    mag = jnp.exp(dt * a_re)
    ab_re, ab_im = mag * jnp.cos(dt * a_im), mag * jnp.sin(dt * a_im)
    den = a_re * a_re + a_im * a_im
    n_re, n_im = ab_re - 1.0, ab_im
    f_re = (n_re * a_re + n_im * a_im) / den
    f_im = (n_im * a_re - n_re * a_im) / den
    b_re, b_im = p['s5_b_re'], p['s5_b_im']
    bb_re = f_re[..., None] * b_re - f_im[..., None] * b_im
    bb_im = f_re[..., None] * b_im + f_im[..., None] * b_re
    gpt = S5_GROUPS // S5_TILES
    eye = jnp.eye(gpt, dtype=F32)

    def pack_b(bb):
        bt = bb.reshape(S5_TILES, gpt, S5_STATE, S5_GROUP)
        w = jnp.einsum('tgpm,gh->tgmhp', bt, eye)
        return w.reshape(S5_TILES, gpt * S5_GROUP, gpt * S5_STATE)

    def pack_c(cc):
        ct = cc.reshape(S5_TILES, gpt, S5_GROUP, S5_STATE)
        w = jnp.einsum('tgmp,gh->tgphm', ct, eye)
        return w.reshape(S5_TILES, gpt * S5_STATE, gpt * S5_GROUP)

    wbu = jnp.concatenate([pack_b(bb_re), pack_b(bb_im)], axis=2).astype(BF16)
    wc = jnp.concatenate([pack_c(p['s5_c_re']), -pack_c(p['s5_c_im'])], axis=1).astype(BF16)
    return (wbu, ab_re.reshape(1, S5_LANES), ab_im.reshape(1, S5_LANES), wc, p['s5_d'].reshape(1, MIX_W))


def _mixffn_kernel(x_ref, gla_ref, glb_ref, o0, o1, o2, o3, wb_ref, wo_ref, g_mix_ref, g1_ref, w1_ref, w2_ref, g2_ref,
                   out_ref, *, tf):
    tm = x_ref.shape[0]
    nsub = 2
    sub = tm // nsub
    rss = [slice(r * sub, (r + 1) * sub) for r in range(nsub)]
    per = GL_BLK // D_MODEL
    ms = []
    for rs in rss:
        m = None
        for n, o in enumerate((o0, o1, o2, o3)):
            t = _dot(o[rs, :], wb_ref[n])
            gl_ref = (gla_ref, glb_ref)[n // per]
            t = _sigmoid(gl_ref[rs, (n % per) * D_MODEL:(n % per + 1) * D_MODEL].astype(F32)) * t
            m = t if m is None else m + t
        ms.append(m.astype(BF16))
    x1s = [x_ref[rs, :] + _rms(_dot(ms[r], wo_ref[...]), g_mix_ref[...]) for r, rs in enumerate(rss)]
    hs = [_rms(x1, g1_ref[...]).astype(BF16) for x1 in x1s]
    accs = [None] * nsub
    for c in range(D_FF // tf):
        us = [jnp.maximum(_dot(h, w1_ref[:, c * tf:(c + 1) * tf]), 0.0) for h in hs]
        for r in range(nsub):
            part = _dot((us[r] * us[r]).astype(BF16), w2_ref[c * tf:(c + 1) * tf, :])
            accs[r] = part if accs[r] is None else accs[r] + part
    for r, rs in enumerate(rss):
        out_ref[rs, :] = x1s[r] + _rms(accs[r], g2_ref[...])


def _mixffn_call(x2, proj, branches, wb, wo, g_mix, g1, w1, w2, g2, *, tm, tf):
    t = x2.shape[0]
    row = lambda i: (i, 0)

    def resident(a):
        nd = a.ndim
        return pl.BlockSpec(a.shape, lambda i: (0,) * nd, pipeline_mode=pl.Buffered(1))

    return pl.pallas_call(
        functools.partial(_mixffn_kernel, tf=tf),
        grid=(t // tm,),
        in_specs=[pl.BlockSpec((tm, D_MODEL), row)]
        + [pl.BlockSpec((tm, GL_BLK), lambda i, c=COL_GL // GL_BLK + k: (i, c)) for k in range(2)]
        + [pl.BlockSpec((tm, MIX_W), row)] * N_BRANCH
        + [resident(a) for a in (wb, wo, g_mix, g1, w1, w2, g2)],
        out_specs=pl.BlockSpec((tm, D_MODEL), row),
        out_shape=jax.ShapeDtypeStruct((t, D_MODEL), F32),
        compiler_params=pltpu.CompilerParams(dimension_semantics=("parallel",), vmem_limit_bytes=MIXFFN_VMEM_LIMIT),
        name="merge_ffn",
    )(x2, proj, proj, *branches, wb, wo, g_mix, g1, w1, w2, g2)


W_TILE = MIX_W


def _wprep_kernel(a_ref, b_ref, tail_ref, o_ref, *, first_moved, last_moved, dt_tile):
    j = pl.program_id(0)

    def emit(rows):
        o_ref[...] = rows.T.astype(BF16)

    @pl.when(j < first_moved)
    def _():
        emit(a_ref[...])

    @pl.when((j >= first_moved) & (j < last_moved))
    def _():
        emit(jnp.concatenate([a_ref[SSD_HEADS:, :], b_ref[:SSD_HEADS, :]], axis=0))

    @pl.when(j == last_moved)
    def _():
        emit(jnp.concatenate([a_ref[SSD_HEADS:, :], tail_ref[...]], axis=0))

    @pl.when(j == dt_tile)
    def _():
        d = a_ref.shape[1]
        emit(jnp.concatenate([jnp.broadcast_to(a_ref[h:h + 1, :], (SSD_HEADDIM, d)) for h in range(SSD_HEADS)],
                             axis=0))


def _wprep_call(w_in_all, layer):
    _, d, cols = w_in_all.shape
    o_sdt = 4 * MIX_W + MIX_W + SSD_CONV_DIM
    assert o_sdt == COL_HG and o_sdt % W_TILE == 0 and (cols - SSD_HEADS) % W_TILE == 0 and SSD_HEADS == SUBLANE
    first_moved = o_sdt // W_TILE
    dt_tile = (cols - SSD_HEADS) // W_TILE
    last_moved = dt_tile - 1
    assert (dt_tile + 1) * W_TILE == PROJ_COLS
    wt = jnp.swapaxes(w_in_all, 1, 2)
    tail = wt[layer, dt_tile * W_TILE:, :]
    a_map = lambda j: (layer, jnp.where(j == dt_tile, first_moved, j), 0)
    b_map = lambda j: (layer, jnp.minimum(j + 1, last_moved), 0)
    return pl.pallas_call(
        functools.partial(_wprep_kernel, first_moved=first_moved, last_moved=last_moved, dt_tile=dt_tile),
        grid=(dt_tile + 1,),
        in_specs=[pl.BlockSpec((None, W_TILE, d), a_map), pl.BlockSpec((None, W_TILE, d), b_map),
                  _const_spec(tail.shape)],
        out_specs=pl.BlockSpec((d, W_TILE), lambda j: (0, j)),
        out_shape=jax.ShapeDtypeStruct((d, PROJ_COLS), BF16),
        compiler_params=_cparams(("arbitrary",)),
        name="w_in_layout",
    )(wt, wt, tail)


def _prep_layer(p, lb, w_main):

    def head_lanes(v):
        return jnp.repeat(v.astype(F32), SSD_HEADDIM).reshape(1, MIX_W)

    q = dict(
        w_main=w_main,
        g_pre_mix=p['g_pre_mix'].reshape(1, D_MODEL),
        g_post_mix=p['g_post_mix'].reshape(1, D_MODEL),
        g_pre_ffn=p['g_pre_ffn'].reshape(1, D_MODEL),
        g_post_ffn=p['g_post_ffn'].reshape(1, D_MODEL),
        ret_gn=p['ret_gn'],
        conv_w=p['ssd_conv_w'],
        conv_b=p['ssd_conv_b'].reshape(1, SSD_CONV_DIM),
        a_log=head_lanes(p['ssd_a_log']),
        dt_bias=head_lanes(p['ssd_dt_bias']),
        d_skip=head_lanes(p['ssd_d']),
        ssd_norm=p['ssd_norm'].reshape(1, MIX_W),
        hg_lb=lb.reshape(1, MIX_W),
        hg_norm=p['hg_norm'].reshape(1, HG_DK),
        s5=_s5_params(p),
        w_glu=p['s5_w_glu'].astype(BF16),
        w_branch=p['w_branch'].astype(BF16),
        w_out=p['w_out'].astype(BF16),
        w_ff1=p['w_ff1'].astype(BF16),
        w_ff2=p['w_ff2'].astype(BF16),
    )
    return q


def _layer(x3, states, acc, layer, q, pos, cfg):
    b, l, _ = x3.shape
    x2 = x3.reshape(b * l, D_MODEL)
    tb, tl, c = cfg['tb'], cfg['tl'], cfg['c']
    proj = _proj_call(x2, q['g_pre_mix'], q['w_main'], tm=cfg['tm_proj'], tn=cfg['tn'])
    st = {n: states[n][layer] for n in ('conv', 's5_re', 's5_im')}
    o_ret, s_ret = _ret_call(proj, pos, q['ret_gn'], states['ret'], acc['ret'], layer, b=b, l=l, tb=tb, tl=tl, sl=c,
                             unroll=cfg['u_ret'])
    o_ssd, conv_new, s_ssd = _ssd_call(proj, q['conv_w'], q['conv_b'], q['a_log'], q['dt_bias'], q['d_skip'],
                                       q['ssd_norm'], st['conv'], states['ssd'], acc['ssd'], layer,
                                       b=b, l=l, tb=tb, tl=tl, sl=c, unroll=cfg['u_ssd'])
    o_hg, s_hg = _hg_call(proj, q['hg_lb'], q['hg_norm'], states['hgrn'], acc['hgrn'], layer, b=b, l=l, tb=tb, tl=tl,
                          sl=c, unroll=cfg['u_hg'])
    wbu, are, aim, wc, d5 = q['s5']
    o_s5, s5_re, s5_im = _s5_call(proj, wbu, are, aim, wc, d5, q['w_glu'],
                                  st['s5_re'].reshape(b, S5_LANES), st['s5_im'].reshape(b, S5_LANES),
                                  b=b, l=l, tb=SUBLANE, tl=cfg['s5_tl'])
    x2 = _mixffn_call(x2, proj, (o_ret, o_ssd, o_hg, o_s5), q['w_branch'], q['w_out'], q['g_post_mix'], q['g_pre_ffn'],
                      q['w_ff1'], q['w_ff2'], q['g_post_ffn'], tm=cfg['tm_merge'], tf=cfg['tf'])
    def push(name, new):
        return new[None] if acc[name] is None else jnp.concatenate([acc[name], new[None]], axis=0)

    new = {'ret': s_ret, 'ssd': s_ssd, 'hgrn': s_hg, 'conv': push('conv', conv_new),
           's5_re': push('s5_re', s5_re.reshape(b, S5_GROUPS, S5_STATE)),
           's5_im': push('s5_im', s5_im.reshape(b, S5_GROUPS, S5_STATE))}
    return x2.reshape(b, l, D_MODEL), new


_NAMES = ('ret', 'ssd', 'conv', 'hgrn', 's5_re', 's5_im')


def _trunk(x, states, layers, pos, cfg):
    acc = {n: None for n in _NAMES}
    for layer, q in enumerate(layers):
        x, acc = _layer(x, states, acc, layer, q, pos, cfg)
    return x, acc


def _group_cfg(b, l):
    if l % CHUNK == 0:
        return dict(tb=1, tl=min(l, 512), c=CHUNK, tm_proj=2048, tn=1536, tm_merge=512, tf=1024,
                    s5_tl=min(l, 64), u_ret=4, u_ssd=8, u_hg=8)
    return dict(tb=SUBLANE, tl=l, c=l, tm_proj=min(b * l, 1024), tn=1536,
                tm_merge=min(b * l, 512), tf=1024, s5_tl=l, u_ret=1, u_ssd=1, u_hg=1)


def kernel(x_prompt, x_sample, state_ret, state_ssd, state_conv, state_hgrn, state_s5_re, state_s5_im, g_pre_mix, g_post_mix, g_pre_ffn, g_post_ffn, w_in, ret_gn, ssd_conv_w, ssd_conv_b, ssd_a_log, ssd_dt_bias, ssd_d, ssd_norm, hg_lb_logits, hg_norm, s5_a_re, s5_a_im, s5_b_re, s5_b_im, s5_c_re, s5_c_im, s5_d, s5_log_dt, s5_w_glu, w_branch, w_out, w_ff1, w_ff2):
    params = {
        'g_pre_mix': g_pre_mix, 'g_post_mix': g_post_mix, 'g_pre_ffn': g_pre_ffn, 'g_post_ffn': g_post_ffn,
        'ret_gn': ret_gn, 'ssd_conv_w': ssd_conv_w, 'ssd_conv_b': ssd_conv_b,
        'ssd_a_log': ssd_a_log, 'ssd_dt_bias': ssd_dt_bias, 'ssd_d': ssd_d, 'ssd_norm': ssd_norm,
        'hg_norm': hg_norm, 's5_a_re': s5_a_re, 's5_a_im': s5_a_im, 's5_b_re': s5_b_re, 's5_b_im': s5_b_im,
        's5_c_re': s5_c_re, 's5_c_im': s5_c_im, 's5_d': s5_d, 's5_log_dt': s5_log_dt, 's5_w_glu': s5_w_glu,
        'w_branch': w_branch, 'w_out': w_out, 'w_ff1': w_ff1, 'w_ff2': w_ff2,
    }
    depth = w_in.shape[0]
    w = jax.nn.softmax(hg_lb_logits.astype(F32), axis=0)
    lbs = jnp.cumsum(w, axis=0) - w[0]
    layers = [_prep_layer({k: v[i] for k, v in params.items()}, lbs[i], _wprep_call(w_in, i)) for i in range(depth)]

    bp, lp, _ = x_prompt.shape
    bs, ls, _ = x_sample.shape
    zero_states = {
        'ret': jnp.zeros((depth, bp) + state_ret.shape[2:], F32),
        'ssd': jnp.zeros((depth, bp) + state_ssd.shape[2:], F32),
        'conv': jnp.zeros((depth, bp) + state_conv.shape[2:], F32),
        'hgrn': jnp.zeros((depth, bp) + state_hgrn.shape[2:], F32),
        's5_re': jnp.zeros((depth, bp) + state_s5_re.shape[2:], F32),
        's5_im': jnp.zeros((depth, bp) + state_s5_im.shape[2:], F32),
    }
    sample_states = {'ret': state_ret, 'ssd': state_ssd, 'conv': state_conv, 'hgrn': state_hgrn,
                     's5_re': state_s5_re, 's5_im': state_s5_im}
    pos_p = np.arange(lp)
    pos_s = PAST_LEN + np.arange(ls)
    y_p, new_p = _trunk(x_prompt, zero_states, layers, pos_p, _group_cfg(bp, lp))
    y_s, new_s = _trunk(x_sample, sample_states, layers, pos_s, _group_cfg(bs, ls))
    return (y_p, y_s,
            new_p['ret'], new_s['ret'], new_p['ssd'], new_s['ssd'], new_p['conv'], new_s['conv'],
            new_p['hgrn'], new_s['hgrn'], new_p['s5_re'], new_s['s5_re'], new_p['s5_im'], new_s['s5_im'])
```

```python
import functools
import math

import jax
import jax.numpy as jnp
import numpy as np
from jax import lax
from jax.experimental import pallas as pl
from jax.experimental.pallas import tpu as pltpu

F32 = jnp.float32
BF16 = jnp.bfloat16

D_MODEL = 1024
DEPTH = 2
PAST_LEN = 16384
MIX_W = D_MODEL // 2
N_BRANCH = 4
RET_HEADS = 4
RET_DK = MIX_W // RET_HEADS
ROPE_BASE = 10000.0
SSD_HEADDIM = 64
SSD_HEADS = MIX_W // SSD_HEADDIM
SSD_GROUPS = 2
SSD_STATE = 128
SSD_CONV = 4
SSD_CONV_DIM = MIX_W + 2 * SSD_GROUPS * SSD_STATE
SSD_PAIRS = SSD_HEADS // 2
HG_HEADS = 4
HG_DK = MIX_W // HG_HEADS
S5_GROUP = 16
S5_GROUPS = MIX_W // S5_GROUP
S5_STATE = 64
S5_LANES = S5_GROUPS * S5_STATE
D_FF = 4 * D_MODEL
GATE_COLS = N_BRANCH * D_MODEL
CHUNK = 64
CR = 64
EPS = 1e-6

LANE = 128
SUBLANE = 8
VMEM_LIMIT = 48 * 1024 * 1024
BIG_VMEM_LIMIT = 56 * 1024 * 1024

COL_RET = 0
COL_SZ = COL_RET + 4 * MIX_W
COL_XS = COL_SZ + MIX_W
COL_BC = COL_XS + MIX_W
COL_HG = COL_BC + MIX_W
COL_SU = COL_HG + 4 * MIX_W
COL_GL = COL_SU + MIX_W
COL_DT = COL_GL + GATE_COLS
PROJ_COLS = COL_DT + MIX_W
GL_BLK = GATE_COLS // 2
assert COL_GL % GL_BLK == 0


def _sigmoid(x):
    return 1.0 / (1.0 + jnp.exp(-x))


def _silu(x):
    return x * _sigmoid(x)


def _softplus(x):
    return jnp.maximum(x, 0.0) + jnp.log1p(jnp.exp(-jnp.abs(x)))


def _gelu_tanh(x):
    c = math.sqrt(2.0 / math.pi)
    return 0.5 * x * (1.0 + jnp.tanh(c * (x + 0.044715 * (x * x * x))))


def _rms(x, g):
    ms = jnp.mean(x * x, axis=-1, keepdims=True)
    return x * lax.rsqrt(ms + EPS) * g


def _dot(a, b):
    return jnp.dot(a, b, preferred_element_type=F32)


def _dot_nt(a, b):
    return lax.dot_general(a, b, (((1,), (1,)), ((), ())), preferred_element_type=F32)


def _split3(a):
    hi = a.astype(BF16)
    r1 = a - hi.astype(F32)
    mid = r1.astype(BF16)
    lo = (r1 - mid.astype(F32)).astype(BF16)
    return hi, mid, lo


def _sel_l(m, a):
    hi, mid, lo = _split3(a)
    return (_dot(m, lo) + _dot(m, mid)) + _dot(m, hi)


def _sel2_l(m, a):
    hi = a.astype(BF16)
    mid = (a - hi.astype(F32)).astype(BF16)
    return _dot(m, mid) + _dot(m, hi)


def _sel_r(a, m):
    hi, mid, lo = _split3(a)
    return (_dot(lo, m) + _dot(mid, m)) + _dot(hi, m)


def _cparams(sem):
    return pltpu.CompilerParams(dimension_semantics=sem, vmem_limit_bytes=VMEM_LIMIT)


def _const_spec(shape):
    nd = len(shape)
    return pl.BlockSpec(shape, lambda *_: (0,) * nd)


def _state_io(s_all, prev, layer, tb):
    dims = tuple(s_all.shape[2:])
    zeros = (0,) * len(dims)
    in_spec = pl.BlockSpec((None, tb) + dims, lambda bi, li: (layer, bi) + zeros)
    if prev is None:
        nprev, prev, prev_spec = 0, s_all, in_spec
    else:
        nprev = prev.shape[0]
        prev_spec = pl.BlockSpec((nprev, tb) + dims, lambda bi, li: (0, bi) + zeros)
    out_spec = pl.BlockSpec((nprev + 1, tb) + dims, lambda bi, li: (0, bi) + zeros)
    out_shape = jax.ShapeDtypeStruct((nprev + 1, s_all.shape[1]) + dims, F32)
    return nprev, prev, in_spec, prev_spec, out_spec, out_shape


def _proj_kernel(x_ref, g_ref, w_ref, o_ref, *, tn):
    tm = x_ref.shape[0]
    nsub = 2
    sub = tm // nsub
    rss = [slice(r * sub, (r + 1) * sub) for r in range(nsub)]
    hs = [_rms(x_ref[rs, :], g_ref[...]).astype(BF16) for rs in rss]
    for c in range(w_ref.shape[1] // tn):
        cs = slice(c * tn, (c + 1) * tn)
        for r, rs in enumerate(rss):
            o_ref[rs, cs] = _dot(hs[r], w_ref[:, cs]).astype(BF16)


def _proj_call(x2, g, w, *, tm, tn):
    t, d = x2.shape
    n = w.shape[1]
    assert n % tn == 0
    return pl.pallas_call(
        functools.partial(_proj_kernel, tn=tn),
        grid=(t // tm,),
        in_specs=[
            pl.BlockSpec((tm, d), lambda i: (i, 0)),
            pl.BlockSpec((1, d), lambda i: (0, 0)),
            pl.BlockSpec((d, n), lambda i: (0, 0), pipeline_mode=pl.Buffered(1)),
        ],
        out_specs=pl.BlockSpec((tm, n), lambda i: (i, 0)),
        out_shape=jax.ShapeDtypeStruct((t, n), BF16),
        compiler_params=pltpu.CompilerParams(dimension_semantics=("parallel",), vmem_limit_bytes=BIG_VMEM_LIMIT),
        name="in_proj",
    )(x2, g, w)


def _ret_consts(sl):
    nseq = CR // sl
    h = np.arange(RET_HEADS, dtype=np.float64)
    log_g = np.log(1.0 - 2.0 ** (-5.0 - h))
    idx = np.arange(sl, dtype=np.float64)
    diff = idx[:, None] - idx[None, :]
    dmat = np.where(diff >= 0, np.exp(log_g[:, None, None] * np.maximum(diff, 0.0)), 0.0)
    dmat = np.stack([np.kron(np.eye(nseq), d) for d in dmat])
    q_dec = np.tile(np.exp(log_g[:, None] * (idx[None, :] + 1.0)), (1, nseq))
    k_dec = np.tile(np.exp(log_g[:, None] * (sl - 1.0 - idx[None, :])), (1, nseq))
    chunk_dec = np.exp(log_g * sl)
    qd = np.broadcast_to(q_dec[:, :, None], (RET_HEADS, CR, LANE))
    kd = np.broadcast_to(k_dec[:, :, None], (RET_HEADS, CR, LANE))
    return (jnp.asarray(dmat, F32), jnp.asarray(qd, F32), jnp.asarray(kd, F32),
            tuple(float(v) for v in chunk_dec))


def _rope_tables(pos):
    half = RET_DK // 2
    inv = ROPE_BASE ** (-np.arange(half, dtype=np.float64) / half)
    ang = np.asarray(pos, np.float64)[:, None] * inv[None, :]
    cos, sin = np.cos(ang), np.sin(ang)
    return (jnp.asarray(np.concatenate([cos, cos], axis=1), F32),
            jnp.asarray(np.concatenate([-sin, sin], axis=1), F32))


def _ret_kernel(q_ref, k_ref, v_ref, g_ref, cos_ref, sin_ref, dmat_ref, qd_ref, kd_ref, gn_ref, s0_ref, prev_ref,
                o_ref, so_ref, *, tb, tl, sl, cdec, nprev, unroll):
    st_ref = so_ref.at[nprev]

    @pl.when(pl.program_id(1) == 0)
    def _():
        st_ref[...] = s0_ref[...]
        if nprev:
            so_ref[0:nprev] = prev_ref[...]

    nseq = CR // sl
    scale = RET_DK ** -0.5

    def body(pc, carry):
        r0 = pl.multiple_of(pc * CR, CR)
        t0 = r0 if nseq == 1 else 0
        cos = cos_ref[pl.ds(t0, CR), :]
        sin = sin_ref[pl.ds(t0, CR), :]
        heads = [slice(h * LANE, (h + 1) * LANE) for h in range(RET_HEADS)]
        qrs, krs = [], []
        for hs in heads:
            q = q_ref[pl.ds(r0, CR), hs].astype(F32)
            k = k_ref[pl.ds(r0, CR), hs].astype(F32)
            qrs.append(q * cos + pltpu.roll(q, LANE // 2, axis=1) * sin)
            krs.append((k * cos + pltpu.roll(k, LANE // 2, axis=1) * sin) * scale)
        vs = [v_ref[pl.ds(r0, CR), hs] for hs in heads]
        atts = [(_dot_nt(qrs[h].astype(BF16), krs[h].astype(BF16)) * dmat_ref[h]).astype(BF16)
                for h in range(RET_HEADS)]
        outs = []
        for h in range(RET_HEADS):
            o = _dot(atts[h], vs[h])
            qd = qrs[h] * qd_ref[h]
            kd = krs[h] * kd_ref[h]
            ois = []
            for s in range(nseq):
                seq = pc * nseq + s if nseq > 1 else 0
                rs = slice(s * sl, (s + 1) * sl)
                st = st_ref[seq, h]
                ois.append(_dot(qd[rs].astype(BF16), st.astype(BF16)))
                v_s = vs[h] if nseq == 1 else vs[h].astype(F32)[rs].astype(BF16)
                st_ref[seq, h] = st * cdec[h] + _dot(kd[rs].T.astype(BF16), v_s)
            outs.append(o + (ois[0] if nseq == 1 else jnp.concatenate(ois, axis=0)))
        for h, hs in enumerate(heads):
            o = outs[h]
            mu = jnp.mean(o, axis=-1, keepdims=True)
            oc = o - mu
            var = jnp.mean(oc * oc, axis=-1, keepdims=True)
            on = oc * lax.rsqrt(var + EPS) * gn_ref[h:h + 1, :]
            o_ref[pl.ds(r0, CR), hs] = (on * _silu(g_ref[pl.ds(r0, CR), hs].astype(F32))).astype(BF16)
        return carry

    lax.fori_loop(0, tb * tl // CR, body, 0, unroll=unroll)


def _seq_grid(b, l, tb, tl):
    assert b % tb == 0 and l % tl == 0 and (tb == 1 or tl == l)
    nl = l // tl
    rows = tb * tl

    def row_map(col_blk):
        return lambda bi, li: (bi * nl + li, col_blk)

    return (b // tb, nl), rows, row_map


def _ret_call(proj, pos, gn, s_all, prev, layer, *, b, l, tb, tl, sl, unroll):
    grid, rows, row_map = _seq_grid(b, l, tb, tl)
    assert rows % CR == 0 and CR % sl == 0 and (sl == CR and tb == 1 or sl == tl)
    dmat, qd, kd, cdec = _ret_consts(sl)
    cos, sin = _rope_tables(pos)
    if sl < CR:
        cos, sin = jnp.tile(cos, (CR // sl, 1)), jnp.tile(sin, (CR // sl, 1))
        tab_spec = _const_spec(cos.shape)
    else:
        tab_spec = pl.BlockSpec((tl, LANE), lambda bi, li: (li, 0))
    cb = COL_RET // MIX_W
    nprev, prev, s_spec, prev_spec, so_spec, so_shape = _state_io(s_all, prev, layer, tb)
    return pl.pallas_call(
        functools.partial(_ret_kernel, tb=tb, tl=tl, sl=sl, cdec=cdec, nprev=nprev, unroll=unroll),
        grid=grid,
        in_specs=[pl.BlockSpec((rows, MIX_W), row_map(cb + i)) for i in range(4)] + [
            tab_spec, tab_spec,
            _const_spec(dmat.shape), _const_spec(qd.shape), _const_spec(kd.shape), _const_spec(gn.shape),
            s_spec, prev_spec,
        ],
        out_specs=[pl.BlockSpec((rows, MIX_W), row_map(0)), so_spec],
        out_shape=[jax.ShapeDtypeStruct((b * l, MIX_W), BF16), so_shape],
        compiler_params=_cparams(("parallel", "arbitrary")),
        name="retention",
    )(proj, proj, proj, proj, cos, sin, dmat, qd, kd, gn, s_all, prev)


def _ssd_consts(sl):
    nseq = CR // sl
    tril = np.kron(np.eye(nseq), np.tril(np.ones((sl, sl))))
    sumall = np.kron(np.eye(nseq), np.ones((sl, sl)))
    ones = np.ones((CR, CR))
    i = np.arange(CR)[:, None]
    j = np.arange(MIX_W)[None, :] % SSD_HEADDIM
    eye = (i == j).astype(np.float64)
    mask = ((i // sl == j // sl) & (i >= j)).astype(np.float64)
    return (jnp.asarray(np.concatenate([tril, sumall], axis=0), BF16), jnp.asarray(ones, BF16),
            jnp.asarray(eye, F32), jnp.asarray(mask, F32))


def _ssd_kernel(z_ref, xs_ref, bc_ref, dtx_ref, cw_ref, cb_ref, alog_ref, dtb_ref, dsk_ref, nw_ref,
                mcum_ref, ones_ref, eye_ref, mask_ref, cs0_ref, s0_ref, prev_ref,
                o_ref, cs_ref, so_ref, xpad, st, *, tb, tl, sl, nprev, unroll):
    li = pl.program_id(1)
    nl = pl.num_programs(1)
    nseq = CR // sl
    kc = SSD_CONV - 1
    off = SUBLANE - kc
    grp = 2 * SUBLANE

    @pl.when(li == 0)
    def _():
        for s in range(tb):
            for p in range(SSD_PAIRS):
                st[s, p] = s0_ref[s, p].T
        if nprev:
            so_ref[0:nprev] = prev_ref[...]
        if nseq == 1:
            xpad[0:SUBLANE, :] = jnp.zeros((SUBLANE, SSD_CONV_DIM), F32)
            xpad[off:SUBLANE, :] = cs0_ref[0]

    if nseq == 1:
        xpad[SUBLANE:SUBLANE + tl, 0:MIX_W] = xs_ref[...].astype(F32)
        xpad[SUBLANE:SUBLANE + tl, MIX_W:2 * MIX_W] = bc_ref[...].astype(F32)
    else:
        xs_new = xs_ref[...].astype(F32)
        bc_new = bc_ref[...].astype(F32)
        for s in range(tb):
            xpad[s * grp:s * grp + SUBLANE, :] = jnp.zeros((SUBLANE, SSD_CONV_DIM), F32)
            xpad[s * grp + off:s * grp + SUBLANE, :] = cs0_ref[s]
            xpad[s * grp + SUBLANE:(s + 1) * grp, 0:MIX_W] = xs_new[s * sl:(s + 1) * sl, :]
            xpad[s * grp + SUBLANE:(s + 1) * grp, MIX_W:2 * MIX_W] = bc_new[s * sl:(s + 1) * sl, :]
            cs_ref[s] = xpad[(s + 1) * grp - kc:(s + 1) * grp, :]

    a_neg = -jnp.exp(alog_ref[...])
    half = lax.broadcasted_iota(jnp.int32, (CR, LANE), 1) < SSD_HEADDIM

    def body(pc, carry):
        r0 = pl.multiple_of(pc * CR, CR)
        if nseq == 1:
            win = xpad[pl.ds(r0, CR + SUBLANE), :]
            take = lambda a: a[SUBLANE:, :]
        else:
            win = xpad[pl.ds(pl.multiple_of(pc * nseq * grp, nseq * grp), nseq * grp), :]
            take = lambda a: a.reshape(nseq, grp, SSD_CONV_DIM)[:, SUBLANE:, :].reshape(CR, SSD_CONV_DIM)
        acc = cb_ref[...] + take(pltpu.roll(win, kc, axis=0)) * cw_ref[0:1, :]
        for j in range(1, kc):
            acc = acc + take(pltpu.roll(win, kc - j, axis=0)) * cw_ref[j:j + 1, :]
        xc = _silu(acc + take(win) * cw_ref[kc:kc + 1, :])
        xs = xc[:, 0:MIX_W]
        bm = xc[:, MIX_W:MIX_W + SSD_GROUPS * SSD_STATE]
        cm = xc[:, MIX_W + SSD_GROUPS * SSD_STATE:2 * MIX_W]

        dtx = _softplus(dtx_ref[pl.ds(r0, CR), :].astype(F32) + dtb_ref[...])
        both = _sel_l(mcum_ref[...], dtx * a_neg)
        cum, clast = both[0:CR], both[CR:2 * CR]
        ecum = jnp.exp(cum)
        dend = jnp.exp(clast - cum)
        elast = jnp.exp(clast)
        rr = _sel_l(ones_ref[...], cum * eye_ref[...])
        msk = mask_ref[...]
        lmat = jnp.exp(jnp.where(msk > 0, cum - rr, 0.0)) * msk
        zz = z_ref[pl.ds(r0, CR), :].astype(F32)
        npp = SSD_PAIRS // SSD_GROUPS
        bm_gs = [bm[:, g * SSD_STATE:(g + 1) * SSD_STATE] for g in range(SSD_GROUPS)]
        cm_gs = [cm[:, g * SSD_STATE:(g + 1) * SSD_STATE] for g in range(SSD_GROUPS)]
        cm_gbs = [c_.astype(BF16) for c_ in cm_gs]
        cb2s = [_dot_nt(cm_gbs[g], jnp.concatenate([bm_gs[g], bm_gs[g]], axis=0).astype(BF16))
                for g in range(SSD_GROUPS)]
        pss = [slice(p * LANE, (p + 1) * LANE) for p in range(SSD_PAIRS)]
        xdts = [xs[:, ps] * dtx[:, ps] for ps in pss]
        xsts = [jnp.concatenate([jnp.where(half, x_, 0.0), jnp.where(half, 0.0, x_)], axis=0).astype(BF16)
                for x_ in xdts]
        ys = [_dot((cb2s[p // npp] * lmat[:, pss[p]]).astype(BF16), xsts[p]) for p in range(SSD_PAIRS)]
        xds = [xdts[p] * dend[:, pss[p]] for p in range(SSD_PAIRS)]
        yis = []
        for p in range(SSD_PAIRS):
            g = p // npp
            parts = []
            for s in range(nseq):
                seq = pc * nseq + s if nseq > 1 else 0
                if nseq == 1:
                    cm_s, bm_s, xd_s = cm_gbs[g], bm_gs[g], xds[p]
                else:
                    rs = slice(s * sl, (s + 1) * sl)
                    cm_s, bm_s, xd_s = cm_gs[g][rs].astype(BF16), bm_gs[g][rs], xds[p][rs]
                stp = st[seq, p]
                parts.append(_dot(cm_s, stp.astype(BF16)))
                st[seq, p] = stp * elast[s * sl:s * sl + 1, pss[p]] + _dot(bm_s.T.astype(BF16), xd_s.astype(BF16))
            yis.append(parts[0] if nseq == 1 else jnp.concatenate(parts, axis=0))
        gw = SSD_HEADDIM * SSD_HEADS // SSD_GROUPS
        for g in range(SSD_GROUPS):
            yg = jnp.concatenate([ys[p] + yis[p] * ecum[:, pss[p]] + xs[:, pss[p]] * dsk_ref[:, pss[p]]
                                  for p in range(g * npp, (g + 1) * npp)], axis=1)
            zs = slice(g * gw, (g + 1) * gw)
            o_ref[pl.ds(r0, CR), zs] = _rms(yg * _silu(zz[:, zs]), nw_ref[:, zs]).astype(BF16)
        return carry

    lax.fori_loop(0, tb * tl // CR, body, 0, unroll=unroll)

    if nseq == 1:
        xpad[0:SUBLANE, :] = xpad[tl:tl + SUBLANE, :]

    @pl.when(li == nl - 1)
    def _():
        if nseq == 1:
            cs_ref[0] = xpad[tl + off:tl + SUBLANE, :]
        for s in range(tb):
            for p in range(SSD_PAIRS):
                so_ref[nprev, s, p] = st[s, p].T


def _ssd_call(proj, cw, cb, alog, dtb, dsk, nw, cs0, s_all, prev, layer, *, b, l, tb, tl, sl, unroll):
    grid, rows, row_map = _seq_grid(b, l, tb, tl)
    assert rows % CR == 0 and CR % sl == 0 and (sl == CR and tb == 1 or sl == tl == SUBLANE)
    consts = _ssd_consts(sl)
    pair = lambda a: a.reshape(a.shape[:2] + (SSD_PAIRS, LANE, SSD_STATE))
    s_pairs = pair(s_all)
    nprev, prev, s_spec, prev_spec, so_spec, so_shape = _state_io(s_pairs, None if prev is None else pair(prev),
                                                                  layer, tb)
    cs_spec = pl.BlockSpec((tb, SSD_CONV - 1, SSD_CONV_DIM), lambda bi, li: (bi, 0, 0))
    params = (cw, cb, alog, dtb, dsk, nw)
    xpad_rows = SUBLANE + tl if sl == CR else tb * 2 * SUBLANE
    o, cs, so = pl.pallas_call(
        functools.partial(_ssd_kernel, tb=tb, tl=tl, sl=sl, nprev=nprev, unroll=unroll),
        grid=grid,
        in_specs=[pl.BlockSpec((rows, MIX_W), row_map(cb_)) for cb_ in
                  (COL_SZ // MIX_W, COL_XS // MIX_W, COL_BC // MIX_W, COL_DT // MIX_W)]
        + [_const_spec(a.shape) for a in params + consts] + [cs_spec, s_spec, prev_spec],
        out_specs=[pl.BlockSpec((rows, MIX_W), row_map(0)), cs_spec, so_spec],
        out_shape=[jax.ShapeDtypeStruct((b * l, MIX_W), BF16), jax.ShapeDtypeStruct(cs0.shape, F32), so_shape],
        scratch_shapes=[pltpu.VMEM((xpad_rows, SSD_CONV_DIM), F32),
                        pltpu.VMEM((tb, SSD_PAIRS, SSD_STATE, LANE), F32)],
        compiler_params=_cparams(("parallel", "arbitrary")),
        name="ssd",
    )(proj, proj, proj, proj, *params, *consts, cs0, s_pairs, prev)
    return o, cs, so.reshape((nprev + 1,) + s_all.shape[1:])


def _hg_levels(sl):
    ms = []
    m = sl // 2
    while m >= 1:
        ms.append(m)
        m //= 2
    return ms


def _hg_consts(sl):
    nseq = CR // sl
    tril = np.kron(np.eye(nseq), np.tril(np.ones((sl, sl))))
    sumall = np.kron(np.eye(nseq), np.ones((sl, sl)))
    mats = []
    masks = []
    i = np.arange(CR)
    for m in _hg_levels(sl):
        ref = (i // (2 * m)) * (2 * m) + m - 1
        if m > 1:
            mats.append(np.abs(tril - tril[ref]))
        same = (i[:, None] // (2 * m)) == (i[None, :] // (2 * m))
        qside = ((i // m) % 2 == 1)[:, None]
        kside = ((i // m) % 2 == 0)[None, :]
        masks.append((same & qside & kside).astype(np.float64))
    return (jnp.asarray(np.concatenate([tril, sumall], axis=0), BF16), jnp.asarray(np.concatenate(mats, axis=0), BF16),
            jnp.asarray(np.stack(masks), F32))


def _hg_kernel(q_ref, f_ref, i_ref, g_ref, lb_ref, nw_ref, mcum_ref, mlev_ref, mask_ref, s0_ref, prev_ref,
               o_ref, so_ref, st, qe_s, ke_s, qg_s, kd_s, el_s, dv_s, *, tb, tl, sl, nprev, unroll):
    li = pl.program_id(1)
    nl = pl.num_programs(1)

    @pl.when(li == 0)
    def _():
        for s in range(tb):
            for h in range(HG_HEADS):
                st[s, h] = s0_ref[s, h].T
        if nprev:
            so_ref[0:nprev] = prev_ref[...]

    nseq = CR // sl
    levels = _hg_levels(sl)

    nlev = len(levels)
    odd_row = (lax.broadcasted_iota(jnp.int32, (CR, MIX_W), 0) & 1) == 1

    def stage_a(pc, carry):
        r0 = pl.multiple_of(pc * CR, CR)
        lb = lb_ref[...]
        q = _silu(q_ref[pl.ds(r0, CR), :].astype(F32))
        sig = _sigmoid(f_ref[pl.ds(r0, CR), :].astype(F32))
        fdec = lb + (1.0 - lb) * sig
        logf = jnp.log(fdec)
        k = (1.0 - lb) * (1.0 - sig)
        v = i_ref[pl.ds(r0, CR), :].astype(F32)
        cums = _sel_l(mcum_ref[...], logf)
        gc = cums[0:CR]
        glast = cums[CR:2 * CR]
        if nlev > 1:
            dl = _sel2_l(mlev_ref[...], logf)
        qb, kb = q.astype(BF16), k.astype(BF16)
        for n in range(nlev):
            e = jnp.exp(dl[n * CR:(n + 1) * CR]) if n < nlev - 1 else jnp.where(odd_row, fdec, 1.0)
            e = e.astype(BF16)
            qe_s[pc, n] = qb * e
            ke_s[pc, n] = kb * e
        qg_s[pc] = (q * jnp.exp(gc)).astype(BF16)
        kd_s[pc] = (k * jnp.exp(glast - gc)).astype(BF16)
        el_s[pc] = jnp.exp(glast)
        qk = q * k
        dv_s[pc] = jnp.concatenate(
            [jnp.sum(qk[:, h * LANE:(h + 1) * LANE], axis=-1, keepdims=True) * v[:, h * LANE:(h + 1) * LANE]
             for h in range(HG_HEADS)], axis=1)
        return carry

    def stage_b(pc, carry):
        r0 = pl.multiple_of(pc * CR, CR)
        heads = [slice(h * LANE, (h + 1) * LANE) for h in range(HG_HEADS)]
        vs = [i_ref[pl.ds(r0, CR), hs] for hs in heads]
        atts = []
        for hs in heads:
            att = None
            for n in range(nlev):
                a = _dot_nt(qe_s[pc, n, :, hs], ke_s[pc, n, :, hs]) * mask_ref[n]
                att = a if att is None else att + a
            atts.append(att.astype(BF16))
        outs = []
        for h, hs in enumerate(heads):
            o = _dot(atts[h], vs[h]) + dv_s[pc, :, hs]
            ois = []
            for s in range(nseq):
                seq = pc * nseq + s if nseq > 1 else 0
                rs = slice(s * sl, (s + 1) * sl)
                stt = st[seq, h]
                if nseq == 1:
                    qg_r, kd_r, v_r = qg_s[pc, :, hs], kd_s[pc, :, hs], vs[h].astype(F32)
                else:
                    qg_r = qg_s[pc, :, hs].astype(F32)[rs].astype(BF16)
                    kd_r = kd_s[pc, :, hs].astype(F32)[rs].astype(BF16)
                    v_r = vs[h].astype(F32)[rs]
                ois.append(_dot_nt(qg_r, stt.astype(BF16)))
                st[seq, h] = stt * el_s[pc, s * sl:s * sl + 1, hs] + _dot(v_r.T.astype(BF16), kd_r)
            outs.append(o + (ois[0] if nseq == 1 else jnp.concatenate(ois, axis=0)))
        for h, hs in enumerate(heads):
            gate = _sigmoid(g_ref[pl.ds(r0, CR), hs].astype(F32))
            o_ref[pl.ds(r0, CR), hs] = (_rms(outs[h], nw_ref[...]) * gate).astype(BF16)
        return carry

    ngrp = tb * tl // CR
    lax.fori_loop(0, ngrp, stage_a, 0, unroll=unroll)
    lax.fori_loop(0, ngrp, stage_b, 0, unroll=unroll)

    @pl.when(li == nl - 1)
    def _():
        for s in range(tb):
            for h in range(HG_HEADS):
                so_ref[nprev, s, h] = st[s, h].T


def _hg_call(proj, lb, nw, s_all, prev, layer, *, b, l, tb, tl, sl, unroll):
    grid, rows, row_map = _seq_grid(b, l, tb, tl)
    assert rows % CR == 0 and CR % sl == 0 and (sl == CR and tb == 1 or sl == tl)
    consts = _hg_consts(sl)
    nlev = len(_hg_levels(sl))
    cb = COL_HG // MIX_W
    nprev, prev, s_spec, prev_spec, so_spec, so_shape = _state_io(s_all, prev, layer, tb)
    stage = lambda dt, *lead: pltpu.VMEM((rows // CR,) + lead + (CR, MIX_W), dt)
    return pl.pallas_call(
        functools.partial(_hg_kernel, tb=tb, tl=tl, sl=sl, nprev=nprev, unroll=unroll),
        grid=grid,
        in_specs=[pl.BlockSpec((rows, MIX_W), row_map(cb + i)) for i in range(4)]
        + [_const_spec(a.shape) for a in (lb, nw) + consts] + [s_spec, prev_spec],
        out_specs=[pl.BlockSpec((rows, MIX_W), row_map(0)), so_spec],
        out_shape=[jax.ShapeDtypeStruct((b * l, MIX_W), BF16), so_shape],
        scratch_shapes=[pltpu.VMEM((tb, HG_HEADS, HG_DK, HG_DK), F32), stage(BF16, nlev), stage(BF16, nlev),
                        stage(BF16), stage(BF16), stage(F32), stage(F32)],
        compiler_params=_cparams(("parallel", "arbitrary")),
        name="hgrn2",
    )(proj, proj, proj, proj, lb, nw, *consts, s_all, prev)


S5_TILES = MIX_W // LANE
S5_TLANES = S5_LANES // S5_TILES


def _s5_perm(tb, tl):
    rows = tb * tl
    p = np.zeros((rows, rows))
    t, s = np.meshgrid(np.arange(tl), np.arange(tb), indexing='ij')
    p[(s * tl + t).ravel(), (t * tb + s).ravel()] = 1.0
    return jnp.asarray(p, BF16), jnp.asarray(p.T, BF16)


def _s5_kernel(su_ref, perm_ref, permt_ref, wbu_ref, are_ref, aim_ref, wc_ref, d_ref, wglu_ref, h0r_ref, h0i_ref,
               o_ref, hr_ref, hi_ref, bur, bui, *, tb, tl):
    @pl.when(pl.program_id(1) == 0)
    def _():
        hr_ref[...] = h0r_ref[...]
        hi_ref[...] = h0i_ref[...]

    rows = tl * tb
    u = _dot(permt_ref[...], su_ref[...].reshape(rows, MIX_W))
    tiles = [slice(t * S5_TLANES, (t + 1) * S5_TLANES) for t in range(S5_TILES)]

    def bu_tile(t):
        bu = _dot(u[:, t * LANE:(t + 1) * LANE].astype(BF16), wbu_ref[t])
        bur[:, tiles[t]] = bu[:, :S5_TLANES]
        bui[:, tiles[t]] = bu[:, S5_TLANES:]

    def scan_tile(t):
        ls = tiles[t]
        ar = jnp.broadcast_to(are_ref[:, ls], (tb, S5_TLANES))
        ai = jnp.broadcast_to(aim_ref[:, ls], (tb, S5_TLANES))
        hr, hi = hr_ref[:, ls], hi_ref[:, ls]
        for i in range(tl):
            rs = slice(i * tb, (i + 1) * tb)
            hr, hi = (ar * hr - ai * hi) + bur[rs, ls], (ar * hi + ai * hr) + bui[rs, ls]
            bur[rs, ls] = hr
            bui[rs, ls] = hi
        hr_ref[:, ls] = hr
        hi_ref[:, ls] = hi

    def y_tile(t):
        hcat = jnp.concatenate([bur[:, tiles[t]], bui[:, tiles[t]]], axis=1).astype(BF16)
        cs = slice(t * LANE, (t + 1) * LANE)
        return _dot(hcat, wc_ref[t]) + d_ref[:, cs] * u[:, cs]

    ys = [None] * S5_TILES
    bu_tile(0)
    for t in range(S5_TILES):
        if t + 1 < S5_TILES:
            bu_tile(t + 1)
        scan_tile(t)
        if t >= 1:
            ys[t - 1] = y_tile(t - 1)
    ys[S5_TILES - 1] = y_tile(S5_TILES - 1)
    z = _gelu_tanh(jnp.concatenate(ys, axis=1))
    out = z * _sigmoid(_dot(z.astype(BF16), wglu_ref[...]))
    o_ref[...] = _dot(perm_ref[...], out.astype(BF16)).astype(BF16).reshape(o_ref.shape)


def _s5_call(proj, wbu, are, aim, wc, d, wglu, h0r, h0i, *, b, l, tb, tl):
    assert l % tl == 0 and b % tb == 0 and tb == SUBLANE
    perm, permt = _s5_perm(tb, tl)
    h_spec = pl.BlockSpec((tb, S5_LANES), lambda bi, li: (bi, 0))
    params = (perm, permt, wbu, are, aim, wc, d, wglu)
    cb = COL_SU // MIX_W
    if tl == l:
        su, su_spec = proj, pl.BlockSpec((tb * tl, MIX_W), lambda bi, li: (bi, cb))
        o_shape, o_spec = (b * l, MIX_W), pl.BlockSpec((tb * tl, MIX_W), lambda bi, li: (bi, 0))
    else:
        su, su_spec = proj.reshape(b, l, -1), pl.BlockSpec((tb, tl, MIX_W), lambda bi, li: (bi, li, cb))
        o_shape, o_spec = (b, l, MIX_W), pl.BlockSpec((tb, tl, MIX_W), lambda bi, li: (bi, li, 0))
    o, hr, hi = pl.pallas_call(
        functools.partial(_s5_kernel, tb=tb, tl=tl),
        grid=(b // tb, l // tl),
        in_specs=[su_spec] + [_const_spec(a.shape) for a in params] + [h_spec, h_spec],
        out_specs=[o_spec, h_spec, h_spec],
        out_shape=[jax.ShapeDtypeStruct(o_shape, BF16), jax.ShapeDtypeStruct(h0r.shape, F32),
                   jax.ShapeDtypeStruct(h0i.shape, F32)],
        scratch_shapes=[pltpu.VMEM((tl * tb, S5_LANES), F32), pltpu.VMEM((tl * tb, S5_LANES), F32)],
        compiler_params=_cparams(("parallel", "arbitrary")),
        name="s5",
    )(su, *params, h0r, h0i)
    return o.reshape(b * l, MIX_W), hr, hi


def _s5_params(p):
    a_re, a_im = p['s5_a_re'], p['s5_a_im']
    dt = jnp.exp(p['s5_log_dt'])[:, None]
    mag = jnp.exp(dt * a_re)
    ab_re, ab_im = mag * jnp.cos(dt * a_im), mag * jnp.sin(dt * a_im)
    den = a_re * a_re + a_im * a_im
    n_re, n_im = ab_re - 1.0, ab_im
    f_re = (n_re * a_re + n_im * a_im) / den
    f_im = (n_im * a_re - n_re * a_im) / den
    b_re, b_im = p['s5_b_re'], p['s5_b_im']
    bb_re = f_re[..., None] * b_re - f_im[..., None] * b_im
    bb_im = f_re[..., None] * b_im + f_im[..., None] * b_re
    gpt = S5_GROUPS // S5_TILES
    eye = jnp.eye(gpt, dtype=F32)

    def pack_b(bb):
        bt = bb.reshape(S5_TILES, gpt, S5_STATE, S5_GROUP)
        w = jnp.einsum('tgpm,gh->tgmhp', bt, eye)
        return w.reshape(S5_TILES, gpt * S5_GROUP, gpt * S5_STATE)

    def pack_c(cc):
        ct = cc.reshape(S5_TILES, gpt, S5_GROUP, S5_STATE)
        w = jnp.einsum('tgmp,gh->tgphm', ct, eye)
        return w.reshape(S5_TILES, gpt * S5_STATE, gpt * S5_GROUP)

    wbu = jnp.concatenate([pack_b(bb_re), pack_b(bb_im)], axis=2).astype(BF16)
    wc = jnp.concatenate([pack_c(p['s5_c_re']), -pack_c(p['s5_c_im'])], axis=1).astype(BF16)
    return (wbu, ab_re.reshape(1, S5_LANES), ab_im.reshape(1, S5_LANES), wc, p['s5_d'].reshape(1, MIX_W))


def _mixffn_kernel(x_ref, gla_ref, glb_ref, o0, o1, o2, o3, wb_ref, wo_ref, g_mix_ref, g1_ref, w1_ref, w2_ref, g2_ref,
                   out_ref, *, tf):
    tm = x_ref.shape[0]
    nsub = 2
    sub = tm // nsub
    rss = [slice(r * sub, (r + 1) * sub) for r in range(nsub)]
    per = GL_BLK // D_MODEL
    ms = []
    for rs in rss:
        m = None
        for n, o in enumerate((o0, o1, o2, o3)):
            t = _dot(o[rs, :], wb_ref[n])
            gl_ref = (gla_ref, glb_ref)[n // per]
            t = _sigmoid(gl_ref[rs, (n % per) * D_MODEL:(n % per + 1) * D_MODEL].astype(F32)) * t
            m = t if m is None else m + t
        ms.append(m.astype(BF16))
    x1s = [x_ref[rs, :] + _rms(_dot(ms[r], wo_ref[...]), g_mix_ref[...]) for r, rs in enumerate(rss)]
    hs = [_rms(x1, g1_ref[...]).astype(BF16) for x1 in x1s]
    accs = [None] * nsub
    for c in range(D_FF // tf):
        us = [jnp.maximum(_dot(h, w1_ref[:, c * tf:(c + 1) * tf]), 0.0) for h in hs]
        for r in range(nsub):
            part = _dot((us[r] * us[r]).astype(BF16), w2_ref[c * tf:(c + 1) * tf, :])
            accs[r] = part if accs[r] is None else accs[r] + part
    for r, rs in enumerate(rss):
        out_ref[rs, :] = x1s[r] + _rms(accs[r], g2_ref[...])


def _mixffn_call(x2, proj, branches, wb, wo, g_mix, g1, w1, w2, g2, *, tm, tf):
    t = x2.shape[0]
    row = lambda i: (i, 0)

    def resident(a):
        nd = a.ndim
        return pl.BlockSpec(a.shape, lambda i: (0,) * nd, pipeline_mode=pl.Buffered(1))

    return pl.pallas_call(
        functools.partial(_mixffn_kernel, tf=tf),
        grid=(t // tm,),
        in_specs=[pl.BlockSpec((tm, D_MODEL), row)]
        + [pl.BlockSpec((tm, GL_BLK), lambda i, c=COL_GL // GL_BLK + k: (i, c)) for k in range(2)]
        + [pl.BlockSpec((tm, MIX_W), row)] * N_BRANCH
        + [resident(a) for a in (wb, wo, g_mix, g1, w1, w2, g2)],
        out_specs=pl.BlockSpec((tm, D_MODEL), row),
        out_shape=jax.ShapeDtypeStruct((t, D_MODEL), F32),
        compiler_params=pltpu.CompilerParams(dimension_semantics=("parallel",), vmem_limit_bytes=BIG_VMEM_LIMIT),
        name="merge_ffn",
    )(x2, proj, proj, *branches, wb, wo, g_mix, g1, w1, w2, g2)


W_TILE = MIX_W


def _wprep_kernel(a_ref, b_ref, tail_ref, o_ref, *, first_moved, last_moved, dt_tile):
    j = pl.program_id(0)

    def emit(rows):
        o_ref[...] = rows.T.astype(BF16)

    @pl.when(j < first_moved)
    def _():
        emit(a_ref[...])

    @pl.when((j >= first_moved) & (j < last_moved))
    def _():
        emit(jnp.concatenate([a_ref[SSD_HEADS:, :], b_ref[:SSD_HEADS, :]], axis=0))

    @pl.when(j == last_moved)
    def _():
        emit(jnp.concatenate([a_ref[SSD_HEADS:, :], tail_ref[...]], axis=0))

    @pl.when(j == dt_tile)
    def _():
        d = a_ref.shape[1]
        emit(jnp.concatenate([jnp.broadcast_to(a_ref[h:h + 1, :], (SSD_HEADDIM, d)) for h in range(SSD_HEADS)],
                             axis=0))


def _wprep_call(w_in_all, layer):
    _, d, cols = w_in_all.shape
    o_sdt = 4 * MIX_W + MIX_W + SSD_CONV_DIM
    assert o_sdt == COL_HG and o_sdt % W_TILE == 0 and (cols - SSD_HEADS) % W_TILE == 0 and SSD_HEADS == SUBLANE
    first_moved = o_sdt // W_TILE
    dt_tile = (cols - SSD_HEADS) // W_TILE
    last_moved = dt_tile - 1
    assert (dt_tile + 1) * W_TILE == PROJ_COLS
    wt = jnp.swapaxes(w_in_all, 1, 2)
    tail = wt[layer, dt_tile * W_TILE:, :]
    a_map = lambda j: (layer, jnp.where(j == dt_tile, first_moved, j), 0)
    b_map = lambda j: (layer, jnp.minimum(j + 1, last_moved), 0)
    return pl.pallas_call(
        functools.partial(_wprep_kernel, first_moved=first_moved, last_moved=last_moved, dt_tile=dt_tile),
        grid=(dt_tile + 1,),
        in_specs=[pl.BlockSpec((None, W_TILE, d), a_map), pl.BlockSpec((None, W_TILE, d), b_map),
                  _const_spec(tail.shape)],
        out_specs=pl.BlockSpec((d, W_TILE), lambda j: (0, j)),
        out_shape=jax.ShapeDtypeStruct((d, PROJ_COLS), BF16),
        compiler_params=_cparams(("arbitrary",)),
        name="w_in_layout",
    )(wt, wt, tail)


def _prep_layer(p, lb, w_main):

    def head_lanes(v):
        return jnp.repeat(v.astype(F32), SSD_HEADDIM).reshape(1, MIX_W)

    q = dict(
        w_main=w_main,
        g_pre_mix=p['g_pre_mix'].reshape(1, D_MODEL),
        g_post_mix=p['g_post_mix'].reshape(1, D_MODEL),
        g_pre_ffn=p['g_pre_ffn'].reshape(1, D_MODEL),
        g_post_ffn=p['g_post_ffn'].reshape(1, D_MODEL),
        ret_gn=p['ret_gn'],
        conv_w=p['ssd_conv_w'],
        conv_b=p['ssd_conv_b'].reshape(1, SSD_CONV_DIM),
        a_log=head_lanes(p['ssd_a_log']),
        dt_bias=head_lanes(p['ssd_dt_bias']),
        d_skip=head_lanes(p['ssd_d']),
        ssd_norm=p['ssd_norm'].reshape(1, MIX_W),
        hg_lb=lb.reshape(1, MIX_W),
        hg_norm=p['hg_norm'].reshape(1, HG_DK),
        s5=_s5_params(p),
        w_glu=p['s5_w_glu'].astype(BF16),
        w_branch=p['w_branch'].astype(BF16),
        w_out=p['w_out'].astype(BF16),
        w_ff1=p['w_ff1'].astype(BF16),
        w_ff2=p['w_ff2'].astype(BF16),
    )
    return q


def _layer(x3, states, acc, layer, q, pos, cfg):
    b, l, _ = x3.shape
    x2 = x3.reshape(b * l, D_MODEL)
    tb, tl, c = cfg['tb'], cfg['tl'], cfg['c']
    proj = _proj_call(x2, q['g_pre_mix'], q['w_main'], tm=cfg['tm_proj'], tn=cfg['tn'])
    st = {n: states[n][layer] for n in ('conv', 's5_re', 's5_im')}
    o_ret, s_ret = _ret_call(proj, pos, q['ret_gn'], states['ret'], acc['ret'], layer, b=b, l=l, tb=tb, tl=tl, sl=c,
                             unroll=cfg['u_ret'])
    o_ssd, conv_new, s_ssd = _ssd_call(proj, q['conv_w'], q['conv_b'], q['a_log'], q['dt_bias'], q['d_skip'],
                                       q['ssd_norm'], st['conv'], states['ssd'], acc['ssd'], layer,
                                       b=b, l=l, tb=tb, tl=tl, sl=c, unroll=cfg['u_ssd'])
    o_hg, s_hg = _hg_call(proj, q['hg_lb'], q['hg_norm'], states['hgrn'], acc['hgrn'], layer, b=b, l=l, tb=tb, tl=tl,
                          sl=c, unroll=cfg['u_hg'])
    wbu, are, aim, wc, d5 = q['s5']
    o_s5, s5_re, s5_im = _s5_call(proj, wbu, are, aim, wc, d5, q['w_glu'],
                                  st['s5_re'].reshape(b, S5_LANES), st['s5_im'].reshape(b, S5_LANES),
                                  b=b, l=l, tb=SUBLANE, tl=cfg['s5_tl'])
    x2 = _mixffn_call(x2, proj, (o_ret, o_ssd, o_hg, o_s5), q['w_branch'], q['w_out'], q['g_post_mix'], q['g_pre_ffn'],
                      q['w_ff1'], q['w_ff2'], q['g_post_ffn'], tm=cfg['tm_merge'], tf=cfg['tf'])
    def push(name, new):
        return new[None] if acc[name] is None else jnp.concatenate([acc[name], new[None]], axis=0)

    new = {'ret': s_ret, 'ssd': s_ssd, 'hgrn': s_hg, 'conv': push('conv', conv_new),
           's5_re': push('s5_re', s5_re.reshape(b, S5_GROUPS, S5_STATE)),
           's5_im': push('s5_im', s5_im.reshape(b, S5_GROUPS, S5_STATE))}
    return x2.reshape(b, l, D_MODEL), new


_NAMES = ('ret', 'ssd', 'conv', 'hgrn', 's5_re', 's5_im')


def _trunk(x, states, layers, pos, cfg):
    acc = {n: None for n in _NAMES}
    for layer, q in enumerate(layers):
        x, acc = _layer(x, states, acc, layer, q, pos, cfg)
    return x, acc


def _group_cfg(b, l):
    if l % CHUNK == 0:
        return dict(tb=1, tl=min(l, 512), c=CHUNK, tm_proj=512, tn=1536, tm_merge=512, tf=1024,
                    s5_tl=min(l, 64), u_ret=4, u_ssd=8, u_hg=8)
    tb = 2 * SUBLANE if b % (2 * SUBLANE) == 0 else SUBLANE
    return dict(tb=tb, tl=l, c=l, tm_proj=min(b * l, 512), tn=1536,
                tm_merge=min(b * l, 512), tf=1024, s5_tl=l, u_ret=tb // SUBLANE, u_ssd=tb // SUBLANE,
                u_hg=tb // SUBLANE)


def kernel(x_prompt, x_sample, state_ret, state_ssd, state_conv, state_hgrn, state_s5_re, state_s5_im, g_pre_mix, g_post_mix, g_pre_ffn, g_post_ffn, w_in, ret_gn, ssd_conv_w, ssd_conv_b, ssd_a_log, ssd_dt_bias, ssd_d, ssd_norm, hg_lb_logits, hg_norm, s5_a_re, s5_a_im, s5_b_re, s5_b_im, s5_c_re, s5_c_im, s5_d, s5_log_dt, s5_w_glu, w_branch, w_out, w_ff1, w_ff2):
    params = {
        'g_pre_mix': g_pre_mix, 'g_post_mix': g_post_mix, 'g_pre_ffn': g_pre_ffn, 'g_post_ffn': g_post_ffn,
        'ret_gn': ret_gn, 'ssd_conv_w': ssd_conv_w, 'ssd_conv_b': ssd_conv_b,
        'ssd_a_log': ssd_a_log, 'ssd_dt_bias': ssd_dt_bias, 'ssd_d': ssd_d, 'ssd_norm': ssd_norm,
        'hg_norm': hg_norm, 's5_a_re': s5_a_re, 's5_a_im': s5_a_im, 's5_b_re': s5_b_re, 's5_b_im': s5_b_im,
        's5_c_re': s5_c_re, 's5_c_im': s5_c_im, 's5_d': s5_d, 's5_log_dt': s5_log_dt, 's5_w_glu': s5_w_glu,
        'w_branch': w_branch, 'w_out': w_out, 'w_ff1': w_ff1, 'w_ff2': w_ff2,
    }
    depth = w_in.shape[0]
    w = jax.nn.softmax(hg_lb_logits.astype(F32), axis=0)
    lbs = jnp.cumsum(w, axis=0) - w[0]
    layers = [_prep_layer({k: v[i] for k, v in params.items()}, lbs[i], _wprep_call(w_in, i)) for i in range(depth)]

    bp, lp, _ = x_prompt.shape
    bs, ls, _ = x_sample.shape
    zero_states = {
        'ret': jnp.zeros((depth, bp) + state_ret.shape[2:], F32),
        'ssd': jnp.zeros((depth, bp) + state_ssd.shape[2:], F32),
        'conv': jnp.zeros((depth, bp) + state_conv.shape[2:], F32),
        'hgrn': jnp.zeros((depth, bp) + state_hgrn.shape[2:], F32),
        's5_re': jnp.zeros((depth, bp) + state_s5_re.shape[2:], F32),
        's5_im': jnp.zeros((depth, bp) + state_s5_im.shape[2:], F32),
    }
    sample_states = {'ret': state_ret, 'ssd': state_ssd, 'conv': state_conv, 'hgrn': state_hgrn,
                     's5_re': state_s5_re, 's5_im': state_s5_im}
    pos_p = np.arange(lp)
    pos_s = PAST_LEN + np.arange(ls)
    y_p, new_p = _trunk(x_prompt, zero_states, layers, pos_p, _group_cfg(bp, lp))
    y_s, new_s = _trunk(x_sample, sample_states, layers, pos_s, _group_cfg(bs, ls))
    return (y_p, y_s,
            new_p['ret'], new_s['ret'], new_p['ssd'], new_s['ssd'], new_p['conv'], new_s['conv'],
            new_p['hgrn'], new_s['hgrn'], new_p['s5_re'], new_s['s5_re'], new_p['s5_im'], new_s['s5_im'])
```

```python
import functools
import math

import jax
import jax.numpy as jnp
import numpy as np
from jax import lax
from jax.experimental import pallas as pl
from jax.experimental.pallas import tpu as pltpu

F32 = jnp.float32
BF16 = jnp.bfloat16

D_MODEL = 1024
DEPTH = 2
PAST_LEN = 16384
MIX_W = D_MODEL // 2
N_BRANCH = 4
RET_HEADS = 4
RET_DK = MIX_W // RET_HEADS
ROPE_BASE = 10000.0
SSD_HEADDIM = 64
SSD_HEADS = MIX_W // SSD_HEADDIM
SSD_GROUPS = 2
SSD_STATE = 128
SSD_CONV = 4
SSD_CONV_DIM = MIX_W + 2 * SSD_GROUPS * SSD_STATE
SSD_PAIRS = SSD_HEADS // 2
HG_HEADS = 4
HG_DK = MIX_W // HG_HEADS
S5_GROUP = 16
S5_GROUPS = MIX_W // S5_GROUP
S5_STATE = 64
S5_LANES = S5_GROUPS * S5_STATE
D_FF = 4 * D_MODEL
GATE_COLS = N_BRANCH * D_MODEL
CHUNK = 64
CR = 64
EPS = 1e-6

LANE = 128
SUBLANE = 8
VMEM_LIMIT = 48 * 1024 * 1024
BIG_VMEM_LIMIT = 56 * 1024 * 1024

COL_RET = 0
COL_SZ = COL_RET + 4 * MIX_W
COL_XS = COL_SZ + MIX_W
COL_BC = COL_XS + MIX_W
COL_HG = COL_BC + MIX_W
COL_SU = COL_HG + 4 * MIX_W
COL_GL = COL_SU + MIX_W
COL_DT = COL_GL + GATE_COLS
PROJ_COLS = COL_DT + MIX_W
GL_BLK = GATE_COLS // 2
assert COL_GL % GL_BLK == 0


def _sigmoid(x):
    return 1.0 / (1.0 + jnp.exp(-x))


def _silu(x):
    return x * _sigmoid(x)


def _softplus(x):
    return jnp.maximum(x, 0.0) + jnp.log1p(jnp.exp(-jnp.abs(x)))


def _gelu_tanh(x):
    c = math.sqrt(2.0 / math.pi)
    return 0.5 * x * (1.0 + jnp.tanh(c * (x + 0.044715 * (x * x * x))))


def _rms(x, g):
    ms = jnp.mean(x * x, axis=-1, keepdims=True)
    return x * lax.rsqrt(ms + EPS) * g


def _dot(a, b):
    return jnp.dot(a, b, preferred_element_type=F32)


def _dot_nt(a, b):
    return lax.dot_general(a, b, (((1,), (1,)), ((), ())), preferred_element_type=F32)


def _split3(a):
    hi = a.astype(BF16)
    r1 = a - hi.astype(F32)
    mid = r1.astype(BF16)
    lo = (r1 - mid.astype(F32)).astype(BF16)
    return hi, mid, lo


def _sel_l(m, a):
    hi, mid, lo = _split3(a)
    return (_dot(m, lo) + _dot(m, mid)) + _dot(m, hi)


def _sel2_l(m, a):
    hi = a.astype(BF16)
    mid = (a - hi.astype(F32)).astype(BF16)
    return _dot(m, mid) + _dot(m, hi)


def _cparams(sem):
    return pltpu.CompilerParams(dimension_semantics=sem, vmem_limit_bytes=VMEM_LIMIT)


def _const_spec(shape):
    nd = len(shape)
    return pl.BlockSpec(shape, lambda *_: (0,) * nd)


def _state_io(s_all, prev, layer, tb):
    dims = tuple(s_all.shape[2:])
    zeros = (0,) * len(dims)
    in_spec = pl.BlockSpec((None, tb) + dims, lambda bi, li: (layer, bi) + zeros)
    if prev is None:
        nprev, prev, prev_spec = 0, s_all, in_spec
    else:
        nprev = prev.shape[0]
        prev_spec = pl.BlockSpec((nprev, tb) + dims, lambda bi, li: (0, bi) + zeros)
    out_spec = pl.BlockSpec((nprev + 1, tb) + dims, lambda bi, li: (0, bi) + zeros)
    out_shape = jax.ShapeDtypeStruct((nprev + 1, s_all.shape[1]) + dims, F32)
    return nprev, prev, in_spec, prev_spec, out_spec, out_shape


def _proj_kernel(x_ref, g_ref, w_ref, o_ref, *, tn):
    tm = x_ref.shape[0]
    nsub = 2
    sub = tm // nsub
    rss = [slice(r * sub, (r + 1) * sub) for r in range(nsub)]
    hs = [_rms(x_ref[rs, :], g_ref[...]).astype(BF16) for rs in rss]
    for c in range(w_ref.shape[1] // tn):
        cs = slice(c * tn, (c + 1) * tn)
        for r, rs in enumerate(rss):
            o_ref[rs, cs] = _dot(hs[r], w_ref[:, cs]).astype(BF16)


def _proj_call(x2, g, w, *, tm, tn):
    t, d = x2.shape
    n = w.shape[1]
    assert n % tn == 0
    return pl.pallas_call(
        functools.partial(_proj_kernel, tn=tn),
        grid=(t // tm,),
        in_specs=[
            pl.BlockSpec((tm, d), lambda i: (i, 0)),
            pl.BlockSpec((1, d), lambda i: (0, 0)),
            pl.BlockSpec((d, n), lambda i: (0, 0), pipeline_mode=pl.Buffered(1)),
        ],
        out_specs=pl.BlockSpec((tm, n), lambda i: (i, 0)),
        out_shape=jax.ShapeDtypeStruct((t, n), BF16),
        compiler_params=pltpu.CompilerParams(dimension_semantics=("parallel",), vmem_limit_bytes=BIG_VMEM_LIMIT),
        name="in_proj",
    )(x2, g, w)


def _ret_consts(sl):
    nseq = CR // sl
    h = np.arange(RET_HEADS, dtype=np.float64)
    log_g = np.log(1.0 - 2.0 ** (-5.0 - h))
    idx = np.arange(sl, dtype=np.float64)
    diff = idx[:, None] - idx[None, :]
    dmat = np.where(diff >= 0, np.exp(log_g[:, None, None] * np.maximum(diff, 0.0)), 0.0)
    dmat = np.stack([np.kron(np.eye(nseq), d) for d in dmat])
    q_dec = np.tile(np.exp(log_g[:, None] * (idx[None, :] + 1.0)), (1, nseq))
    k_dec = np.tile(np.exp(log_g[:, None] * (sl - 1.0 - idx[None, :])), (1, nseq))
    chunk_dec = np.exp(log_g * sl)
    qd = np.broadcast_to(q_dec[:, :, None], (RET_HEADS, CR, LANE))
    kd = np.broadcast_to(k_dec[:, :, None], (RET_HEADS, CR, LANE))
    return (jnp.asarray(dmat, F32), jnp.asarray(qd, F32), jnp.asarray(kd, F32),
            tuple(float(v) for v in chunk_dec))


def _rope_tables(pos):
    half = RET_DK // 2
    inv = ROPE_BASE ** (-np.arange(half, dtype=np.float64) / half)
    ang = np.asarray(pos, np.float64)[:, None] * inv[None, :]
    cos, sin = np.cos(ang), np.sin(ang)
    return (jnp.asarray(np.concatenate([cos, cos], axis=1), F32),
            jnp.asarray(np.concatenate([-sin, sin], axis=1), F32))


def _ret_kernel(q_ref, k_ref, v_ref, g_ref, cos_ref, sin_ref, dmat_ref, qd_ref, kd_ref, gn_ref, s0_ref, prev_ref,
                o_ref, so_ref, *, tb, tl, sl, cdec, nprev, unroll):
    st_ref = so_ref.at[nprev]

    @pl.when(pl.program_id(1) == 0)
    def _():
        st_ref[...] = s0_ref[...]
        if nprev:
            so_ref[0:nprev] = prev_ref[...]

    nseq = CR // sl
    scale = RET_DK ** -0.5

    def body(pc, carry):
        r0 = pl.multiple_of(pc * CR, CR)
        t0 = r0 if nseq == 1 else 0
        cos = cos_ref[pl.ds(t0, CR), :]
        sin = sin_ref[pl.ds(t0, CR), :]
        heads = [slice(h * LANE, (h + 1) * LANE) for h in range(RET_HEADS)]
        qrs, krs = [], []
        for hs in heads:
            q = q_ref[pl.ds(r0, CR), hs].astype(F32)
            k = k_ref[pl.ds(r0, CR), hs].astype(F32)
            qrs.append(q * cos + pltpu.roll(q, LANE // 2, axis=1) * sin)
            krs.append((k * cos + pltpu.roll(k, LANE // 2, axis=1) * sin) * scale)
        vs = [v_ref[pl.ds(r0, CR), hs] for hs in heads]
        atts = [(_dot_nt(qrs[h].astype(BF16), krs[h].astype(BF16)) * dmat_ref[h]).astype(BF16)
                for h in range(RET_HEADS)]
        outs = []
        for h in range(RET_HEADS):
            o = _dot(atts[h], vs[h])
            qd = qrs[h] * qd_ref[h]
            kd = krs[h] * kd_ref[h]
            ois = []
            for s in range(nseq):
                seq = pc * nseq + s if nseq > 1 else 0
                rs = slice(s * sl, (s + 1) * sl)
                st = st_ref[seq, h]
                ois.append(_dot(qd[rs].astype(BF16), st.astype(BF16)))
                v_s = vs[h] if nseq == 1 else vs[h].astype(F32)[rs].astype(BF16)
                st_ref[seq, h] = st * cdec[h] + _dot(kd[rs].T.astype(BF16), v_s)
            outs.append(o + (ois[0] if nseq == 1 else jnp.concatenate(ois, axis=0)))
        for h, hs in enumerate(heads):
            o = outs[h]
            mu = jnp.mean(o, axis=-1, keepdims=True)
            oc = o - mu
            var = jnp.mean(oc * oc, axis=-1, keepdims=True)
            on = oc * lax.rsqrt(var + EPS) * gn_ref[h:h + 1, :]
            o_ref[pl.ds(r0, CR), hs] = (on * _silu(g_ref[pl.ds(r0, CR), hs].astype(F32))).astype(BF16)
        return carry

    lax.fori_loop(0, tb * tl // CR, body, 0, unroll=unroll)


def _seq_grid(b, l, tb, tl):
    assert b % tb == 0 and l % tl == 0 and (tb == 1 or tl == l)
    nl = l // tl
    rows = tb * tl

    def row_map(col_blk):
        return lambda bi, li: (bi * nl + li, col_blk)

    return (b // tb, nl), rows, row_map


def _ret_call(proj, pos, gn, s_all, prev, layer, *, b, l, tb, tl, sl, unroll):
    grid, rows, row_map = _seq_grid(b, l, tb, tl)
    assert rows % CR == 0 and CR % sl == 0 and (sl == CR and tb == 1 or sl == tl)
    dmat, qd, kd, cdec = _ret_consts(sl)
    cos, sin = _rope_tables(pos)
    if sl < CR:
        cos, sin = jnp.tile(cos, (CR // sl, 1)), jnp.tile(sin, (CR // sl, 1))
        tab_spec = _const_spec(cos.shape)
    else:
        tab_spec = pl.BlockSpec((tl, LANE), lambda bi, li: (li, 0))
    cb = COL_RET // MIX_W
    nprev, prev, s_spec, prev_spec, so_spec, so_shape = _state_io(s_all, prev, layer, tb)
    return pl.pallas_call(
        functools.partial(_ret_kernel, tb=tb, tl=tl, sl=sl, cdec=cdec, nprev=nprev, unroll=unroll),
        grid=grid,
        in_specs=[pl.BlockSpec((rows, MIX_W), row_map(cb + i)) for i in range(4)] + [
            tab_spec, tab_spec,
            _const_spec(dmat.shape), _const_spec(qd.shape), _const_spec(kd.shape), _const_spec(gn.shape),
            s_spec, prev_spec,
        ],
        out_specs=[pl.BlockSpec((rows, MIX_W), row_map(0)), so_spec],
        out_shape=[jax.ShapeDtypeStruct((b * l, MIX_W), BF16), so_shape],
        compiler_params=_cparams(("parallel", "arbitrary")),
        name="retention",
    )(proj, proj, proj, proj, cos, sin, dmat, qd, kd, gn, s_all, prev)


def _ssd_consts(sl):
    nseq = CR // sl
    tril = np.kron(np.eye(nseq), np.tril(np.ones((sl, sl))))
    sumall = np.kron(np.eye(nseq), np.ones((sl, sl)))
    ones = np.ones((CR, CR))
    i = np.arange(CR)[:, None]
    j = np.arange(MIX_W)[None, :] % SSD_HEADDIM
    eye = (i == j).astype(np.float64)
    mask = ((i // sl == j // sl) & (i >= j)).astype(np.float64)
    return (jnp.asarray(np.concatenate([tril, sumall], axis=0), BF16), jnp.asarray(ones, BF16),
            jnp.asarray(eye, F32), jnp.asarray(mask, F32))


def _ssd_kernel(z_ref, xs_ref, bc_ref, dtx_ref, cw_ref, cb_ref, alog_ref, dtb_ref, dsk_ref, nw_ref,
                mcum_ref, ones_ref, eye_ref, mask_ref, cs0_ref, s0_ref, prev_ref,
                o_ref, cs_ref, so_ref, xpad, st, *, tb, tl, sl, nprev, unroll):
    li = pl.program_id(1)
    nl = pl.num_programs(1)
    nseq = CR // sl
    kc = SSD_CONV - 1
    off = SUBLANE - kc
    grp = 2 * SUBLANE

    @pl.when(li == 0)
    def _():
        for s in range(tb):
            for p in range(SSD_PAIRS):
                st[s, p] = s0_ref[s, p].T
        if nprev:
            so_ref[0:nprev] = prev_ref[...]
        if nseq == 1:
            xpad[0:SUBLANE, :] = jnp.zeros((SUBLANE, SSD_CONV_DIM), F32)
            xpad[off:SUBLANE, :] = cs0_ref[0]

    if nseq == 1:
        xpad[SUBLANE:SUBLANE + tl, 0:MIX_W] = xs_ref[...].astype(F32)
        xpad[SUBLANE:SUBLANE + tl, MIX_W:2 * MIX_W] = bc_ref[...].astype(F32)
    else:
        xs_new = xs_ref[...].astype(F32)
        bc_new = bc_ref[...].astype(F32)
        for s in range(tb):
            xpad[s * grp:s * grp + SUBLANE, :] = jnp.zeros((SUBLANE, SSD_CONV_DIM), F32)
            xpad[s * grp + off:s * grp + SUBLANE, :] = cs0_ref[s]
            xpad[s * grp + SUBLANE:(s + 1) * grp, 0:MIX_W] = xs_new[s * sl:(s + 1) * sl, :]
            xpad[s * grp + SUBLANE:(s + 1) * grp, MIX_W:2 * MIX_W] = bc_new[s * sl:(s + 1) * sl, :]
            cs_ref[s] = xpad[(s + 1) * grp - kc:(s + 1) * grp, :]

    a_neg = -jnp.exp(alog_ref[...])
    half = lax.broadcasted_iota(jnp.int32, (CR, LANE), 1) < SSD_HEADDIM

    def body(pc, carry):
        r0 = pl.multiple_of(pc * CR, CR)
        if nseq == 1:
            win = xpad[pl.ds(r0, CR + SUBLANE), :]
            take = lambda a: a[SUBLANE:, :]
        else:
            win = xpad[pl.ds(pl.multiple_of(pc * nseq * grp, nseq * grp), nseq * grp), :]
            take = lambda a: a.reshape(nseq, grp, SSD_CONV_DIM)[:, SUBLANE:, :].reshape(CR, SSD_CONV_DIM)
        acc = cb_ref[...] + take(pltpu.roll(win, kc, axis=0)) * cw_ref[0:1, :]
        for j in range(1, kc):
            acc = acc + take(pltpu.roll(win, kc - j, axis=0)) * cw_ref[j:j + 1, :]
        xc = _silu(acc + take(win) * cw_ref[kc:kc + 1, :])
        xs = xc[:, 0:MIX_W]
        bm = xc[:, MIX_W:MIX_W + SSD_GROUPS * SSD_STATE]
        cm = xc[:, MIX_W + SSD_GROUPS * SSD_STATE:2 * MIX_W]

        dtx = _softplus(dtx_ref[pl.ds(r0, CR), :].astype(F32) + dtb_ref[...])
        both = _sel_l(mcum_ref[...], dtx * a_neg)
        cum, clast = both[0:CR], both[CR:2 * CR]
        ecum = jnp.exp(cum)
        dend = jnp.exp(clast - cum)
        elast = jnp.exp(clast)
        rr = _sel_l(ones_ref[...], cum * eye_ref[...])
        msk = mask_ref[...]
        lmat = jnp.exp(jnp.where(msk > 0, cum - rr, 0.0)) * msk
        zz = z_ref[pl.ds(r0, CR), :].astype(F32)
        npp = SSD_PAIRS // SSD_GROUPS
        bm_gs = [bm[:, g * SSD_STATE:(g + 1) * SSD_STATE] for g in range(SSD_GROUPS)]
        cm_gs = [cm[:, g * SSD_STATE:(g + 1) * SSD_STATE] for g in range(SSD_GROUPS)]
        cm_gbs = [c_.astype(BF16) for c_ in cm_gs]
        cb2s = [_dot_nt(cm_gbs[g], jnp.concatenate([bm_gs[g], bm_gs[g]], axis=0).astype(BF16))
                for g in range(SSD_GROUPS)]
        pss = [slice(p * LANE, (p + 1) * LANE) for p in range(SSD_PAIRS)]
        xdts = [xs[:, ps] * dtx[:, ps] for ps in pss]
        xsts = [jnp.concatenate([jnp.where(half, x_, 0.0), jnp.where(half, 0.0, x_)], axis=0).astype(BF16)
                for x_ in xdts]
        ys = [_dot((cb2s[p // npp] * lmat[:, pss[p]]).astype(BF16), xsts[p]) for p in range(SSD_PAIRS)]
        xds = [xdts[p] * dend[:, pss[p]] for p in range(SSD_PAIRS)]
        yis = []
        for p in range(SSD_PAIRS):
            g = p // npp
            parts = []
            for s in range(nseq):
                seq = pc * nseq + s if nseq > 1 else 0
                if nseq == 1:
                    cm_s, bm_s, xd_s = cm_gbs[g], bm_gs[g], xds[p]
                else:
                    rs = slice(s * sl, (s + 1) * sl)
                    cm_s, bm_s, xd_s = cm_gs[g][rs].astype(BF16), bm_gs[g][rs], xds[p][rs]
                stp = st[seq, p]
                parts.append(_dot(cm_s, stp.astype(BF16)))
                st[seq, p] = stp * elast[s * sl:s * sl + 1, pss[p]] + _dot(bm_s.T.astype(BF16), xd_s.astype(BF16))
            yis.append(parts[0] if nseq == 1 else jnp.concatenate(parts, axis=0))
        gw = SSD_HEADDIM * SSD_HEADS // SSD_GROUPS
        for g in range(SSD_GROUPS):
            yg = jnp.concatenate([ys[p] + yis[p] * ecum[:, pss[p]] + xs[:, pss[p]] * dsk_ref[:, pss[p]]
                                  for p in range(g * npp, (g + 1) * npp)], axis=1)
            zs = slice(g * gw, (g + 1) * gw)
            o_ref[pl.ds(r0, CR), zs] = _rms(yg * _silu(zz[:, zs]), nw_ref[:, zs]).astype(BF16)
        return carry

    lax.fori_loop(0, tb * tl // CR, body, 0, unroll=unroll)

    if nseq == 1:
        xpad[0:SUBLANE, :] = xpad[tl:tl + SUBLANE, :]

    @pl.when(li == nl - 1)
    def _():
        if nseq == 1:
            cs_ref[0] = xpad[tl + off:tl + SUBLANE, :]
        for s in range(tb):
            for p in range(SSD_PAIRS):
                so_ref[nprev, s, p] = st[s, p].T


def _ssd_call(proj, cw, cb, alog, dtb, dsk, nw, cs0, s_all, prev, layer, *, b, l, tb, tl, sl, unroll):
    grid, rows, row_map = _seq_grid(b, l, tb, tl)
    assert rows % CR == 0 and CR % sl == 0 and (sl == CR and tb == 1 or sl == tl == SUBLANE)
    consts = _ssd_consts(sl)
    pair = lambda a: a.reshape(a.shape[:2] + (SSD_PAIRS, LANE, SSD_STATE))
    s_pairs = pair(s_all)
    nprev, prev, s_spec, prev_spec, so_spec, so_shape = _state_io(s_pairs, None if prev is None else pair(prev),
                                                                  layer, tb)
    cs_spec = pl.BlockSpec((tb, SSD_CONV - 1, SSD_CONV_DIM), lambda bi, li: (bi, 0, 0))
    params = (cw, cb, alog, dtb, dsk, nw)
    xpad_rows = SUBLANE + tl if sl == CR else tb * 2 * SUBLANE
    o, cs, so = pl.pallas_call(
        functools.partial(_ssd_kernel, tb=tb, tl=tl, sl=sl, nprev=nprev, unroll=unroll),
        grid=grid,
        in_specs=[pl.BlockSpec((rows, MIX_W), row_map(cb_)) for cb_ in
                  (COL_SZ // MIX_W, COL_XS // MIX_W, COL_BC // MIX_W, COL_DT // MIX_W)]
        + [_const_spec(a.shape) for a in params + consts] + [cs_spec, s_spec, prev_spec],
        out_specs=[pl.BlockSpec((rows, MIX_W), row_map(0)), cs_spec, so_spec],
        out_shape=[jax.ShapeDtypeStruct((b * l, MIX_W), BF16), jax.ShapeDtypeStruct(cs0.shape, F32), so_shape],
        scratch_shapes=[pltpu.VMEM((xpad_rows, SSD_CONV_DIM), F32),
                        pltpu.VMEM((tb, SSD_PAIRS, SSD_STATE, LANE), F32)],
        compiler_params=_cparams(("parallel", "arbitrary")),
        name="ssd",
    )(proj, proj, proj, proj, *params, *consts, cs0, s_pairs, prev)
    return o, cs, so.reshape((nprev + 1,) + s_all.shape[1:])


def _hg_levels(sl):
    ms = []
    m = sl // 2
    while m >= 1:
        ms.append(m)
        m //= 2
    return ms


def _hg_consts(sl):
    nseq = CR // sl
    tril = np.kron(np.eye(nseq), np.tril(np.ones((sl, sl))))
    sumall = np.kron(np.eye(nseq), np.ones((sl, sl)))
    mats = []
    masks = []
    i = np.arange(CR)
    for m in _hg_levels(sl):
        ref = (i // (2 * m)) * (2 * m) + m - 1
        if m > 1:
            mats.append(np.abs(tril - tril[ref]))
        same = (i[:, None] // (2 * m)) == (i[None, :] // (2 * m))
        qside = ((i // m) % 2 == 1)[:, None]
        kside = ((i // m) % 2 == 0)[None, :]
        masks.append((same & qside & kside).astype(np.float64))
    return (jnp.asarray(np.concatenate([tril, sumall], axis=0), BF16), jnp.asarray(np.concatenate(mats, axis=0), BF16),
            jnp.asarray(np.stack(masks), F32))


def _hg_kernel(q_ref, f_ref, i_ref, g_ref, lb_ref, nw_ref, mcum_ref, mlev_ref, mask_ref, s0_ref, prev_ref,
               o_ref, so_ref, st, qe_s, ke_s, qg_s, kd_s, el_s, dv_s, *, tb, tl, sl, nprev, unroll):
    li = pl.program_id(1)
    nl = pl.num_programs(1)

    @pl.when(li == 0)
    def _():
        for s in range(tb):
            for h in range(HG_HEADS):
                st[s, h] = s0_ref[s, h].T
        if nprev:
            so_ref[0:nprev] = prev_ref[...]

    nseq = CR // sl
    levels = _hg_levels(sl)

    nlev = len(levels)
    odd_row = (lax.broadcasted_iota(jnp.int32, (CR, MIX_W), 0) & 1) == 1

    def stage_a(pc, carry):
        r0 = pl.multiple_of(pc * CR, CR)
        lb = lb_ref[...]
        q = _silu(q_ref[pl.ds(r0, CR), :].astype(F32))
        sig = _sigmoid(f_ref[pl.ds(r0, CR), :].astype(F32))
        fdec = lb + (1.0 - lb) * sig
        logf = jnp.log(fdec)
        k = (1.0 - lb) * (1.0 - sig)
        v = i_ref[pl.ds(r0, CR), :].astype(F32)
        cums = _sel_l(mcum_ref[...], logf)
        gc = cums[0:CR]
        glast = cums[CR:2 * CR]
        if nlev > 1:
            dl = _sel2_l(mlev_ref[...], logf)
        qb, kb = q.astype(BF16), k.astype(BF16)
        for n in range(nlev):
            e = jnp.exp(dl[n * CR:(n + 1) * CR]) if n < nlev - 1 else jnp.where(odd_row, fdec, 1.0)
            e = e.astype(BF16)
            qe_s[pc, n] = qb * e
            ke_s[pc, n] = kb * e
        qg_s[pc] = (q * jnp.exp(gc)).astype(BF16)
        kd_s[pc] = (k * jnp.exp(glast - gc)).astype(BF16)
        el_s[pc] = jnp.exp(glast)
        qk = q * k
        dv_s[pc] = jnp.concatenate(
            [jnp.sum(qk[:, h * LANE:(h + 1) * LANE], axis=-1, keepdims=True) * v[:, h * LANE:(h + 1) * LANE]
             for h in range(HG_HEADS)], axis=1)
        return carry

    def stage_b(pc, carry):
        r0 = pl.multiple_of(pc * CR, CR)
        heads = [slice(h * LANE, (h + 1) * LANE) for h in range(HG_HEADS)]
        vs = [i_ref[pl.ds(r0, CR), hs] for hs in heads]
        atts = []
        for hs in heads:
            att = None
            for n in range(nlev):
                a = _dot_nt(qe_s[pc, n, :, hs], ke_s[pc, n, :, hs]) * mask_ref[n]
                att = a if att is None else att + a
            atts.append(att.astype(BF16))
        outs = []
        for h, hs in enumerate(heads):
            o = _dot(atts[h], vs[h]) + dv_s[pc, :, hs]
            ois = []
            for s in range(nseq):
                seq = pc * nseq + s if nseq > 1 else 0
                rs = slice(s * sl, (s + 1) * sl)
                stt = st[seq, h]
                if nseq == 1:
                    qg_r, kd_r, v_r = qg_s[pc, :, hs], kd_s[pc, :, hs], vs[h].astype(F32)
                else:
                    qg_r = qg_s[pc, :, hs].astype(F32)[rs].astype(BF16)
                    kd_r = kd_s[pc, :, hs].astype(F32)[rs].astype(BF16)
                    v_r = vs[h].astype(F32)[rs]
                ois.append(_dot_nt(qg_r, stt.astype(BF16)))
                st[seq, h] = stt * el_s[pc, s * sl:s * sl + 1, hs] + _dot(v_r.T.astype(BF16), kd_r)
            outs.append(o + (ois[0] if nseq == 1 else jnp.concatenate(ois, axis=0)))
        for h, hs in enumerate(heads):
            gate = _sigmoid(g_ref[pl.ds(r0, CR), hs].astype(F32))
            o_ref[pl.ds(r0, CR), hs] = (_rms(outs[h], nw_ref[...]) * gate).astype(BF16)
        return carry

    ngrp = tb * tl // CR
    lax.fori_loop(0, ngrp, stage_a, 0, unroll=unroll)
    lax.fori_loop(0, ngrp, stage_b, 0, unroll=unroll)

    @pl.when(li == nl - 1)
    def _():
        for s in range(tb):
            for h in range(HG_HEADS):
                so_ref[nprev, s, h] = st[s, h].T


def _hg_call(proj, lb, nw, s_all, prev, layer, *, b, l, tb, tl, sl, unroll):
    grid, rows, row_map = _seq_grid(b, l, tb, tl)
    assert rows % CR == 0 and CR % sl == 0 and (sl == CR and tb == 1 or sl == tl)
    consts = _hg_consts(sl)
    nlev = len(_hg_levels(sl))
    cb = COL_HG // MIX_W
    nprev, prev, s_spec, prev_spec, so_spec, so_shape = _state_io(s_all, prev, layer, tb)
    stage = lambda dt, *lead: pltpu.VMEM((rows // CR,) + lead + (CR, MIX_W), dt)
    return pl.pallas_call(
        functools.partial(_hg_kernel, tb=tb, tl=tl, sl=sl, nprev=nprev, unroll=unroll),
        grid=grid,
        in_specs=[pl.BlockSpec((rows, MIX_W), row_map(cb + i)) for i in range(4)]
        + [_const_spec(a.shape) for a in (lb, nw) + consts] + [s_spec, prev_spec],
        out_specs=[pl.BlockSpec((rows, MIX_W), row_map(0)), so_spec],
        out_shape=[jax.ShapeDtypeStruct((b * l, MIX_W), BF16), so_shape],
        scratch_shapes=[pltpu.VMEM((tb, HG_HEADS, HG_DK, HG_DK), F32), stage(BF16, nlev), stage(BF16, nlev),
                        stage(BF16), stage(BF16), stage(F32), stage(F32)],
        compiler_params=_cparams(("parallel", "arbitrary")),
        name="hgrn2",
    )(proj, proj, proj, proj, lb, nw, *consts, s_all, prev)


S5_TILES = MIX_W // LANE
S5_TLANES = S5_LANES // S5_TILES


def _s5_perm(tb, tl):
    rows = tb * tl
    p = np.zeros((rows, rows))
    t, s = np.meshgrid(np.arange(tl), np.arange(tb), indexing='ij')
    p[(s * tl + t).ravel(), (t * tb + s).ravel()] = 1.0
    return jnp.asarray(p, BF16), jnp.asarray(p.T, BF16)


def _s5_kernel(su_ref, perm_ref, permt_ref, wbu_ref, are_ref, aim_ref, wc_ref, d_ref, wglu_ref, h0r_ref, h0i_ref,
               o_ref, hr_ref, hi_ref, bur, bui, *, tb, tl):
    @pl.when(pl.program_id(1) == 0)
    def _():
        hr_ref[...] = h0r_ref[...]
        hi_ref[...] = h0i_ref[...]

    rows = tl * tb
    u = _dot(permt_ref[...], su_ref[...].reshape(rows, MIX_W))
    tiles = [slice(t * S5_TLANES, (t + 1) * S5_TLANES) for t in range(S5_TILES)]

    def bu_tile(t):
        bu = _dot(u[:, t * LANE:(t + 1) * LANE].astype(BF16), wbu_ref[t])
        bur[:, tiles[t]] = bu[:, :S5_TLANES]
        bui[:, tiles[t]] = bu[:, S5_TLANES:]

    def scan_tile(t):
        ls = tiles[t]
        ar = jnp.broadcast_to(are_ref[:, ls], (tb, S5_TLANES))
        ai = jnp.broadcast_to(aim_ref[:, ls], (tb, S5_TLANES))
        hr, hi = hr_ref[:, ls], hi_ref[:, ls]
        for i in range(tl):
            rs = slice(i * tb, (i + 1) * tb)
            hr, hi = (ar * hr - ai * hi) + bur[rs, ls], (ar * hi + ai * hr) + bui[rs, ls]
            bur[rs, ls] = hr
            bui[rs, ls] = hi
        hr_ref[:, ls] = hr
        hi_ref[:, ls] = hi

    def y_tile(t):
        hcat = jnp.concatenate([bur[:, tiles[t]], bui[:, tiles[t]]], axis=1).astype(BF16)
        cs = slice(t * LANE, (t + 1) * LANE)
        return _dot(hcat, wc_ref[t]) + d_ref[:, cs] * u[:, cs]

    ys = [None] * S5_TILES
    bu_tile(0)
    for t in range(S5_TILES):
        if t + 1 < S5_TILES:
            bu_tile(t + 1)
        scan_tile(t)
        if t >= 1:
            ys[t - 1] = y_tile(t - 1)
    ys[S5_TILES - 1] = y_tile(S5_TILES - 1)
    z = _gelu_tanh(jnp.concatenate(ys, axis=1))
    out = z * _sigmoid(_dot(z.astype(BF16), wglu_ref[...]))
    o_ref[...] = _dot(perm_ref[...], out.astype(BF16)).astype(BF16).reshape(o_ref.shape)


def _s5_call(proj, wbu, are, aim, wc, d, wglu, h0r, h0i, *, b, l, tb, tl):
    assert l % tl == 0 and b % tb == 0 and tb == SUBLANE
    perm, permt = _s5_perm(tb, tl)
    h_spec = pl.BlockSpec((tb, S5_LANES), lambda bi, li: (bi, 0))
    params = (perm, permt, wbu, are, aim, wc, d, wglu)
    cb = COL_SU // MIX_W
    if tl == l:
        su, su_spec = proj, pl.BlockSpec((tb * tl, MIX_W), lambda bi, li: (bi, cb))
        o_shape, o_spec = (b * l, MIX_W), pl.BlockSpec((tb * tl, MIX_W), lambda bi, li: (bi, 0))
    else:
        su, su_spec = proj.reshape(b, l, -1), pl.BlockSpec((tb, tl, MIX_W), lambda bi, li: (bi, li, cb))
        o_shape, o_spec = (b, l, MIX_W), pl.BlockSpec((tb, tl, MIX_W), lambda bi, li: (bi, li, 0))
    o, hr, hi = pl.pallas_call(
        functools.partial(_s5_kernel, tb=tb, tl=tl),
        grid=(b // tb, l // tl),
        in_specs=[su_spec] + [_const_spec(a.shape) for a in params] + [h_spec, h_spec],
        out_specs=[o_spec, h_spec, h_spec],
        out_shape=[jax.ShapeDtypeStruct(o_shape, BF16), jax.ShapeDtypeStruct(h0r.shape, F32),
                   jax.ShapeDtypeStruct(h0i.shape, F32)],
        scratch_shapes=[pltpu.VMEM((tl * tb, S5_LANES), F32), pltpu.VMEM((tl * tb, S5_LANES), F32)],
        compiler_params=_cparams(("parallel", "arbitrary")),
        name="s5",
    )(su, *params, h0r, h0i)
    return o.reshape(b * l, MIX_W), hr, hi


def _s5_params(p):
    a_re, a_im = p['s5_a_re'], p['s5_a_im']
    dt = jnp.exp(p['s5_log_dt'])[:, None]
    mag = jnp.exp(dt * a_re)
    ab_re, ab_im = mag * jnp.cos(dt * a_im), mag * jnp.sin(dt * a_im)
    den = a_re * a_re + a_im * a_im
    n_re, n_im = ab_re - 1.0, ab_im
    f_re = (n_re * a_re + n_im * a_im) / den
    f_im = (n_im * a_re - n_re * a_im) / den
    b_re, b_im = p['s5_b_re'], p['s5_b_im']
    bb_re = f_re[..., None] * b_re - f_im[..., None] * b_im
    bb_im = f_re[..., None] * b_im + f_im[..., None] * b_re
    gpt = S5_GROUPS // S5_TILES
    eye = jnp.eye(gpt, dtype=F32)

    def pack_b(bb):
        bt = bb.reshape(S5_TILES, gpt, S5_STATE, S5_GROUP)
        w = jnp.einsum('tgpm,gh->tgmhp', bt, eye)
        return w.reshape(S5_TILES, gpt * S5_GROUP, gpt * S5_STATE)

    def pack_c(cc):
        ct = cc.reshape(S5_TILES, gpt, S5_GROUP, S5_STATE)
        w = jnp.einsum('tgmp,gh->tgphm', ct, eye)
        return w.reshape(S5_TILES, gpt * S5_STATE, gpt * S5_GROUP)

    wbu = jnp.concatenate([pack_b(bb_re), pack_b(bb_im)], axis=2).astype(BF16)
    wc = jnp.concatenate([pack_c(p['s5_c_re']), -pack_c(p['s5_c_im'])], axis=1).astype(BF16)
    return (wbu, ab_re.reshape(1, S5_LANES), ab_im.reshape(1, S5_LANES), wc, p['s5_d'].reshape(1, MIX_W))


def _mixffn_kernel(x_ref, gla_ref, glb_ref, o0, o1, o2, o3, wb_ref, wo_ref, g_mix_ref, g1_ref, w1_ref, w2_ref, g2_ref,
                   out_ref, *, tf):
    tm = x_ref.shape[0]
    nsub = 2
    sub = tm // nsub
    rss = [slice(r * sub, (r + 1) * sub) for r in range(nsub)]
    per = GL_BLK // D_MODEL
    ms = []
    for rs in rss:
        m = None
        for n, o in enumerate((o0, o1, o2, o3)):
            t = _dot(o[rs, :], wb_ref[n])
            gl_ref = (gla_ref, glb_ref)[n // per]
            t = _sigmoid(gl_ref[rs, (n % per) * D_MODEL:(n % per + 1) * D_MODEL].astype(F32)) * t
            m = t if m is None else m + t
        ms.append(m.astype(BF16))
    x1s = [x_ref[rs, :] + _rms(_dot(ms[r], wo_ref[...]), g_mix_ref[...]) for r, rs in enumerate(rss)]
    hs = [_rms(x1, g1_ref[...]).astype(BF16) for x1 in x1s]
    accs = [None] * nsub
    for c in range(D_FF // tf):
        us = [jnp.maximum(_dot(h, w1_ref[:, c * tf:(c + 1) * tf]), 0.0) for h in hs]
        for r in range(nsub):
            part = _dot((us[r] * us[r]).astype(BF16), w2_ref[c * tf:(c + 1) * tf, :])
            accs[r] = part if accs[r] is None else accs[r] + part
    for r, rs in enumerate(rss):
        out_ref[rs, :] = x1s[r] + _rms(accs[r], g2_ref[...])


def _mixffn_call(x2, proj, branches, wb, wo, g_mix, g1, w1, w2, g2, *, tm, tf):
    t = x2.shape[0]
    row = lambda i: (i, 0)

    def resident(a):
        nd = a.ndim
        return pl.BlockSpec(a.shape, lambda i: (0,) * nd, pipeline_mode=pl.Buffered(1))

    return pl.pallas_call(
        functools.partial(_mixffn_kernel, tf=tf),
        grid=(t // tm,),
        in_specs=[pl.BlockSpec((tm, D_MODEL), row)]
        + [pl.BlockSpec((tm, GL_BLK), lambda i, c=COL_GL // GL_BLK + k: (i, c)) for k in range(2)]
        + [pl.BlockSpec((tm, MIX_W), row)] * N_BRANCH
        + [resident(a) for a in (wb, wo, g_mix, g1, w1, w2, g2)],
        out_specs=pl.BlockSpec((tm, D_MODEL), row),
        out_shape=jax.ShapeDtypeStruct((t, D_MODEL), F32),
        compiler_params=pltpu.CompilerParams(dimension_semantics=("parallel",), vmem_limit_bytes=BIG_VMEM_LIMIT),
        name="merge_ffn",
    )(x2, proj, proj, *branches, wb, wo, g_mix, g1, w1, w2, g2)


W_TILE = MIX_W


def _wprep_kernel(a_ref, b_ref, tail_ref, o_ref, *, first_moved, last_moved, dt_tile):
    j = pl.program_id(0)

    def emit(rows):
        o_ref[...] = rows.T.astype(BF16)

    @pl.when(j < first_moved)
    def _():
        emit(a_ref[...])

    @pl.when((j >= first_moved) & (j < last_moved))
    def _():
        emit(jnp.concatenate([a_ref[SSD_HEADS:, :], b_ref[:SSD_HEADS, :]], axis=0))

    @pl.when(j == last_moved)
    def _():
        emit(jnp.concatenate([a_ref[SSD_HEADS:, :], tail_ref[...]], axis=0))

    @pl.when(j == dt_tile)
    def _():
        d = a_ref.shape[1]
        emit(jnp.concatenate([jnp.broadcast_to(a_ref[h:h + 1, :], (SSD_HEADDIM, d)) for h in range(SSD_HEADS)],
                             axis=0))


def _wprep_call(w_in_all, layer):
    _, d, cols = w_in_all.shape
    o_sdt = 4 * MIX_W + MIX_W + SSD_CONV_DIM
    assert o_sdt == COL_HG and o_sdt % W_TILE == 0 and (cols - SSD_HEADS) % W_TILE == 0 and SSD_HEADS == SUBLANE
    first_moved = o_sdt // W_TILE
    dt_tile = (cols - SSD_HEADS) // W_TILE
    last_moved = dt_tile - 1
    assert (dt_tile + 1) * W_TILE == PROJ_COLS
    wt = jnp.swapaxes(w_in_all, 1, 2)
    tail = wt[layer, dt_tile * W_TILE:, :]
    a_map = lambda j: (layer, jnp.where(j == dt_tile, first_moved, j), 0)
    b_map = lambda j: (layer, jnp.minimum(j + 1, last_moved), 0)
    return pl.pallas_call(
        functools.partial(_wprep_kernel, first_moved=first_moved, last_moved=last_moved, dt_tile=dt_tile),
        grid=(dt_tile + 1,),
        in_specs=[pl.BlockSpec((None, W_TILE, d), a_map), pl.BlockSpec((None, W_TILE, d), b_map),
                  _const_spec(tail.shape)],
        out_specs=pl.BlockSpec((d, W_TILE), lambda j: (0, j)),
        out_shape=jax.ShapeDtypeStruct((d, PROJ_COLS), BF16),
        compiler_params=_cparams(("arbitrary",)),
        name="w_in_layout",
    )(wt, wt, tail)


def _prep_layer(p, lb, w_main):

    def head_lanes(v):
        return jnp.repeat(v.astype(F32), SSD_HEADDIM).reshape(1, MIX_W)

    q = dict(
        w_main=w_main,
        g_pre_mix=p['g_pre_mix'].reshape(1, D_MODEL),
        g_post_mix=p['g_post_mix'].reshape(1, D_MODEL),
        g_pre_ffn=p['g_pre_ffn'].reshape(1, D_MODEL),
        g_post_ffn=p['g_post_ffn'].reshape(1, D_MODEL),
        ret_gn=p['ret_gn'],
        conv_w=p['ssd_conv_w'],
        conv_b=p['ssd_conv_b'].reshape(1, SSD_CONV_DIM),
        a_log=head_lanes(p['ssd_a_log']),
        dt_bias=head_lanes(p['ssd_dt_bias']),
        d_skip=head_lanes(p['ssd_d']),
        ssd_norm=p['ssd_norm'].reshape(1, MIX_W),
        hg_lb=lb.reshape(1, MIX_W),
        hg_norm=p['hg_norm'].reshape(1, HG_DK),
        s5=_s5_params(p),
        w_glu=p['s5_w_glu'].astype(BF16),
        w_branch=p['w_branch'].astype(BF16),
        w_out=p['w_out'].astype(BF16),
        w_ff1=p['w_ff1'].astype(BF16),
        w_ff2=p['w_ff2'].astype(BF16),
    )
    return q


def _layer(x3, states, acc, layer, q, pos, cfg):
    b, l, _ = x3.shape
    x2 = x3.reshape(b * l, D_MODEL)
    tb, tl, c = cfg['tb'], cfg['tl'], cfg['c']
    proj = _proj_call(x2, q['g_pre_mix'], q['w_main'], tm=cfg['tm_proj'], tn=cfg['tn'])
    st = {n: states[n][layer] for n in ('conv', 's5_re', 's5_im')}
    o_ret, s_ret = _ret_call(proj, pos, q['ret_gn'], states['ret'], acc['ret'], layer, b=b, l=l, tb=tb, tl=tl, sl=c,
                             unroll=cfg['u_ret'])
    o_ssd, conv_new, s_ssd = _ssd_call(proj, q['conv_w'], q['conv_b'], q['a_log'], q['dt_bias'], q['d_skip'],
                                       q['ssd_norm'], st['conv'], states['ssd'], acc['ssd'], layer,
                                       b=b, l=l, tb=tb, tl=tl, sl=c, unroll=cfg['u_ssd'])
    o_hg, s_hg = _hg_call(proj, q['hg_lb'], q['hg_norm'], states['hgrn'], acc['hgrn'], layer, b=b, l=l, tb=tb, tl=tl,
                          sl=c, unroll=cfg['u_hg'])
    wbu, are, aim, wc, d5 = q['s5']
    o_s5, s5_re, s5_im = _s5_call(proj, wbu, are, aim, wc, d5, q['w_glu'],
                                  st['s5_re'].reshape(b, S5_LANES), st['s5_im'].reshape(b, S5_LANES),
                                  b=b, l=l, tb=SUBLANE, tl=cfg['s5_tl'])
    x2 = _mixffn_call(x2, proj, (o_ret, o_ssd, o_hg, o_s5), q['w_branch'], q['w_out'], q['g_post_mix'], q['g_pre_ffn'],
                      q['w_ff1'], q['w_ff2'], q['g_post_ffn'], tm=cfg['tm_merge'], tf=cfg['tf'])
    def push(name, new):
        return new[None] if acc[name] is None else jnp.concatenate([acc[name], new[None]], axis=0)

    new = {'ret': s_ret, 'ssd': s_ssd, 'hgrn': s_hg, 'conv': push('conv', conv_new),
           's5_re': push('s5_re', s5_re.reshape(b, S5_GROUPS, S5_STATE)),
           's5_im': push('s5_im', s5_im.reshape(b, S5_GROUPS, S5_STATE))}
    return x2.reshape(b, l, D_MODEL), new


_NAMES = ('ret', 'ssd', 'conv', 'hgrn', 's5_re', 's5_im')


def _trunk(x, states, layers, pos, cfg):
    acc = {n: None for n in _NAMES}
    for layer, q in enumerate(layers):
        x, acc = _layer(x, states, acc, layer, q, pos, cfg)
    return x, acc


def _group_cfg(b, l):
    if l % CHUNK == 0:
        return dict(tb=1, tl=min(l, 512), c=CHUNK, tm_proj=512, tn=1536, tm_merge=512, tf=1024,
                    s5_tl=min(l, 64), u_ret=4, u_ssd=8, u_hg=8)
    tb = 2 * SUBLANE if b % (2 * SUBLANE) == 0 else SUBLANE
    return dict(tb=tb, tl=l, c=l, tm_proj=min(b * l, 512), tn=1536,
                tm_merge=min(b * l, 512), tf=1024, s5_tl=l, u_ret=tb // SUBLANE, u_ssd=tb // SUBLANE,
                u_hg=tb // SUBLANE)


def kernel(x_prompt, x_sample, state_ret, state_ssd, state_conv, state_hgrn, state_s5_re, state_s5_im, g_pre_mix, g_post_mix, g_pre_ffn, g_post_ffn, w_in, ret_gn, ssd_conv_w, ssd_conv_b, ssd_a_log, ssd_dt_bias, ssd_d, ssd_norm, hg_lb_logits, hg_norm, s5_a_re, s5_a_im, s5_b_re, s5_b_im, s5_c_re, s5_c_im, s5_d, s5_log_dt, s5_w_glu, w_branch, w_out, w_ff1, w_ff2):
    params = {
        'g_pre_mix': g_pre_mix, 'g_post_mix': g_post_mix, 'g_pre_ffn': g_pre_ffn, 'g_post_ffn': g_post_ffn,
        'ret_gn': ret_gn, 'ssd_conv_w': ssd_conv_w, 'ssd_conv_b': ssd_conv_b,
        'ssd_a_log': ssd_a_log, 'ssd_dt_bias': ssd_dt_bias, 'ssd_d': ssd_d, 'ssd_norm': ssd_norm,
        'hg_norm': hg_norm, 's5_a_re': s5_a_re, 's5_a_im': s5_a_im, 's5_b_re': s5_b_re, 's5_b_im': s5_b_im,
        's5_c_re': s5_c_re, 's5_c_im': s5_c_im, 's5_d': s5_d, 's5_log_dt': s5_log_dt, 's5_w_glu': s5_w_glu,
        'w_branch': w_branch, 'w_out': w_out, 'w_ff1': w_ff1, 'w_ff2': w_ff2,
    }
    depth = w_in.shape[0]
    w = jax.nn.softmax(hg_lb_logits.astype(F32), axis=0)
    lbs = jnp.cumsum(w, axis=0) - w[0]
    layers = [_prep_layer({k: v[i] for k, v in params.items()}, lbs[i], _wprep_call(w_in, i)) for i in range(depth)]

    bp, lp, _ = x_prompt.shape
    bs, ls, _ = x_sample.shape
    zero_states = {
        'ret': jnp.zeros((depth, bp) + state_ret.shape[2:], F32),
        'ssd': jnp.zeros((depth, bp) + state_ssd.shape[2:], F32),
        'conv': jnp.zeros((depth, bp) + state_conv.shape[2:], F32),
        'hgrn': jnp.zeros((depth, bp) + state_hgrn.shape[2:], F32),
        's5_re': jnp.zeros((depth, bp) + state_s5_re.shape[2:], F32),
        's5_im': jnp.zeros((depth, bp) + state_s5_im.shape[2:], F32),
    }
    sample_states = {'ret': state_ret, 'ssd': state_ssd, 'conv': state_conv, 'hgrn': state_hgrn,
                     's5_re': state_s5_re, 's5_im': state_s5_im}
    pos_p = np.arange(lp)
    pos_s = PAST_LEN + np.arange(ls)
    y_p, new_p = _trunk(x_prompt, zero_states, layers, pos_p, _group_cfg(bp, lp))
    y_s, new_s = _trunk(x_sample, sample_states, layers, pos_s, _group_cfg(bs, ls))
    return (y_p, y_s,
            new_p['ret'], new_s['ret'], new_p['ssd'], new_s['ssd'], new_p['conv'], new_s['conv'],
            new_p['hgrn'], new_s['hgrn'], new_p['s5_re'], new_s['s5_re'], new_p['s5_im'], new_s['s5_im'])
```
